```python
import math
import jax, jax.numpy as jnp
from jax import lax
import numpy as np

D_MODEL = 1024
BATCH = 8
SEQ = 4096
DEPTH = 4

GRID_W = 64
HEAD_DIM = 64
EPS = 1e-6
PLE_DIM = 256

NA_HEADS = 16
NA_WIDTH = NA_HEADS * HEAD_DIM
NA_WIN_ROWS = 8
NA_WIN_COLS = 16
NA_QCOLS = 16
NA_KCOLS = 32

DIL_PAIRS = ((128, 1), (512, 4), (2048, 16))
DIL_HEADS_PER_GROUP = 8
DIL_HEADS = DIL_HEADS_PER_GROUP * len(DIL_PAIRS)
DIL_WIDTH = DIL_HEADS * HEAD_DIM
DIL_OUT_WIDTH = DIL_HEADS_PER_GROUP * HEAD_DIM
ROPE_THETA = 500000.0
ROPE_DIM = HEAD_DIM // 4

SSM_INNER = 1536
SSM_HEAD_DIM = 64
SSM_HEADS = SSM_INNER // SSM_HEAD_DIM
SSM_GROUPS = 4
SSM_STATE = 128
SSM_CONV = 5
SSM_CHUNK = 128
SSM_CONV_DIM = SSM_INNER + 2 * SSM_GROUPS * SSM_STATE

IN_SPLITS = (NA_WIDTH, NA_WIDTH, NA_WIDTH, NA_WIDTH,
             DIL_WIDTH, DIL_WIDTH, DIL_WIDTH, DIL_OUT_WIDTH,
             SSM_CONV_DIM, SSM_INNER, 2 * SSM_HEADS,
             D_MODEL, D_MODEL, D_MODEL)
IN_WIDTH = sum(IN_SPLITS)

kernel_name = 'hybrid_natten_dilated_ssd_encoder'


def rms_norm(x, g):
    xf = x.astype(jnp.float32)
    y = xf * lax.rsqrt(jnp.mean(xf * xf, axis=-1, keepdims=True) + EPS)
    return (y * g.astype(jnp.float32)).astype(x.dtype)


def split_heads(t, n):
    b, s, _ = t.shape
    return t.reshape(b, s, n, HEAD_DIM)


def rotary_tables(pos):
    inv = ROPE_THETA ** (-jnp.arange(0, ROPE_DIM, 2, dtype=jnp.float32) / ROPE_DIM)
    ang = pos.astype(jnp.float32)[:, None] * inv[None, :]
    return jnp.cos(ang), jnp.sin(ang)


def apply_partial_rotary(t, cos, sin):
    half = ROPE_DIM // 2
    t1 = t[..., :half].astype(jnp.float32)
    t2 = t[..., half:ROPE_DIM].astype(jnp.float32)
    c = cos[None, :, None, :]
    s_ = sin[None, :, None, :]
    rot = jnp.concatenate([t1 * c - t2 * s_, t2 * c + t1 * s_], axis=-1).astype(t.dtype)
    return jnp.concatenate([rot, t[..., ROPE_DIM:]], axis=-1)


def neighbourhood_attention(q, k, v, rpb):
    b, s, h, dh = q.shape
    rows = s // GRID_W
    kh = min(NA_WIN_ROWS, rows)
    ncb = GRID_W // NA_QCOLS
    qcol = np.arange(GRID_W).reshape(ncb, NA_QCOLS)
    win_start = np.clip(qcol - NA_WIN_COLS // 2, 0, GRID_W - NA_WIN_COLS)
    band_start = np.clip(np.arange(ncb) * NA_QCOLS - NA_WIN_COLS // 2, 0, GRID_W - NA_KCOLS)
    kcol = band_start[:, None] + np.arange(NA_KCOLS)[None, :]
    in_win = (kcol[:, None, :] >= win_start[:, :, None]) & (kcol[:, None, :] < win_start[:, :, None] + NA_WIN_COLS)
    dcol = np.clip(kcol[:, None, :] - qcol[:, :, None] + NA_WIN_COLS - 1, 0, 2 * NA_WIN_COLS - 2)
    col_bias = jnp.where(in_win, rpb.astype(jnp.float32)[:, :, dcol], -jnp.inf)
    qg = jnp.moveaxis(q.reshape(b, rows, ncb, NA_QCOLS, h, dh), 1, 0)
    kg = k.reshape(b, rows, GRID_W, h, dh)
    vg = v.reshape(b, rows, GRID_W, h, dh)
    scale = dh ** -0.5

    def one_row(args):
        r, qr = args
        r0 = jnp.clip(r - kh // 2, 0, rows - kh)
        kb = lax.dynamic_slice_in_dim(kg, r0, kh, axis=1)[:, :, kcol]
        vb = lax.dynamic_slice_in_dim(vg, r0, kh, axis=1)[:, :, kcol]
        bias = jnp.take(col_bias, r0 + jnp.arange(kh) - r + NA_WIN_ROWS - 1, axis=1)
        sc = jnp.einsum('bjqhd,bkjchd->bhjqkc', qr, kb, preferred_element_type=jnp.float32) * scale
        sc = sc + jnp.transpose(bias, (0, 2, 3, 1, 4))
        pr = jax.nn.softmax(sc.reshape(b, h, ncb, NA_QCOLS, kh * NA_KCOLS), axis=-1).reshape(sc.shape)
        return jnp.einsum('bhjqkc,bkjchd->bjqhd', pr.astype(v.dtype), vb)

    out = lax.map(one_row, (jnp.arange(rows), qg))
    return jnp.moveaxis(out, 0, 1).reshape(b, s, h, dh)


def dilated_attention(q, k, v, window, dilation):
    b, s, h, dh = q.shape
    blk = window // (2 * dilation)
    L = s // dilation
    nb = -(-L // blk)
    lp = nb * blk

    def to_residue(t):
        return t.reshape(b, L, dilation, h, dh).transpose(0, 2, 1, 3, 4)

    qs = jnp.pad(to_residue(q), ((0, 0), (0, 0), (0, lp - L), (0, 0), (0, 0))).reshape(b, dilation, nb, blk, h, dh)
    kpad = ((0, 0), (0, 0), (blk, lp - L + blk), (0, 0), (0, 0))

    def windows(t):
        tb = jnp.pad(to_residue(t), kpad).reshape(b, dilation, nb + 2, blk, h, dh)
        return jnp.concatenate([tb[:, :, :-2], tb[:, :, 1:-1], tb[:, :, 2:]], axis=3)

    kw, vw = windows(k), windows(v)
    mpos = np.arange(lp).reshape(nb, blk)
    kpos = np.arange(nb)[:, None] * blk - blk + np.arange(3 * blk)[None, :]
    valid = ((kpos[:, None, :] >= 0) & (kpos[:, None, :] < L)
             & (np.abs(kpos[:, None, :] - mpos[:, :, None]) <= blk))
    sc = jnp.einsum('bdnqhe,bdnkhe->bdhnqk', qs, kw, preferred_element_type=jnp.float32) * (dh ** -0.5)
    sc = jnp.where(valid, sc, -jnp.inf)
    m = jnp.max(sc, axis=-1, keepdims=True)
    e = jnp.exp(sc - m)
    den = jnp.sum(e, axis=-1, keepdims=True)
    o = jnp.einsum('bdhnqk,bdnkhe->bdnqhe', (e / den).astype(v.dtype), vw)
    lse = (m + jnp.log(den))[..., 0]
    o = o.reshape(b, dilation, lp, h, dh)[:, :, :L].transpose(0, 2, 1, 3, 4).reshape(b, s, h, dh)
    lse = lse.transpose(0, 1, 3, 4, 2).reshape(b, dilation, lp, h)[:, :, :L].transpose(0, 2, 1, 3).reshape(b, s, h)
    return o, lse


def dilated_mixture(q, k, v):
    outs, lses = [], []
    for g, (window, dilation) in enumerate(DIL_PAIRS):
        sl = slice(g * DIL_HEADS_PER_GROUP, (g + 1) * DIL_HEADS_PER_GROUP)
        o, lse = dilated_attention(q[:, :, sl], k[:, :, sl], v[:, :, sl], window, dilation)
        outs.append(o)
        lses.append(lse)
    wts = jax.nn.softmax(jnp.stack(lses, axis=0), axis=0)
    out = jnp.sum(wts[..., None] * jnp.stack(outs, axis=0).astype(jnp.float32), axis=0)
    return out.astype(q.dtype)


def centred_depthwise_conv(t, w, bias):
    c = t.shape[-1]
    pad = SSM_CONV // 2
    y = lax.conv_general_dilated(t, w.reshape(SSM_CONV, 1, c).astype(t.dtype), (1,), [(pad, pad)],
                                 dimension_numbers=('NWC', 'WIO', 'NWC'), feature_group_count=c)
    return y + bias.astype(t.dtype)


def ssd_chunked(xs, dt, A, bm, cm):
    b, s, nh, hp = xs.shape
    g, n = bm.shape[-2:]
    hg = nh // g
    l = SSM_CHUNK
    c = s // l
    f32 = jnp.float32
    xdt = (xs.astype(f32) * dt[..., None]).reshape(b, c, l, g, hg, hp)
    bc = bm.astype(f32).reshape(b, c, l, g, n)
    cc = cm.astype(f32).reshape(b, c, l, g, n)
    a = (dt * A).reshape(b, c, l, g, hg).transpose(0, 3, 4, 1, 2)
    a_cum = jnp.cumsum(a, axis=-1)
    causal = np.tril(np.ones((l, l), dtype=bool))
    seg = jnp.exp(jnp.where(causal, a_cum[..., :, None] - a_cum[..., None, :], -jnp.inf))
    cb = jnp.einsum('bclgn,bcsgn->bgcls', cc, bc)
    y_diag = jnp.einsum('bgcls,bghcls,bcsghp->bclghp', cb, seg, xdt)
    decay_to_end = jnp.exp(a_cum[..., -1:] - a_cum)
    chunk_states = jnp.einsum('bclgn,bghcl,bclghp->cbghpn', bc, decay_to_end, xdt)
    chunk_decay = jnp.exp(a_cum[..., -1]).transpose(3, 0, 1, 2)

    def carry_state(state, inp):
        st, dec = inp
        return state * dec[..., None, None] + st, state

    init = jnp.zeros((b, g, hg, hp, n), f32)
    _, entering = lax.scan(carry_state, init, (chunk_states, chunk_decay))
    y_off = jnp.einsum('bclgn,cbghpn,bghcl->bclghp', cc, entering, jnp.exp(a_cum))
    return (y_diag + y_off).reshape(b, s, nh, hp)


def mamba2_bidirectional(xbc, z, dt_raw, conv_w, conv_b, a_log, dt_bias, d_skip, norm_w):
    b, s, _ = xbc.shape
    xbc = jax.nn.silu(centred_depthwise_conv(xbc, conv_w, conv_b))
    xs, bm, cm = jnp.split(xbc, [SSM_INNER, SSM_INNER + SSM_GROUPS * SSM_STATE], axis=-1)
    xs = xs.reshape(b, s, SSM_HEADS, SSM_HEAD_DIM)
    bm = bm.reshape(b, s, SSM_GROUPS, SSM_STATE)
    cm = cm.reshape(b, s, SSM_GROUPS, SSM_STATE)
    dt = jax.nn.softplus(dt_raw.astype(jnp.float32).reshape(b, s, 2, SSM_HEADS) + dt_bias.astype(jnp.float32))
    A = -jnp.exp(a_log.astype(jnp.float32))
    y_f = ssd_chunked(xs, dt[:, :, 0], A[0], bm, cm)
    y_b = jnp.flip(ssd_chunked(jnp.flip(xs, 1), jnp.flip(dt[:, :, 1], 1), A[1],
                               jnp.flip(bm, 1), jnp.flip(cm, 1)), 1)
    y = y_f + y_b + d_skip.astype(jnp.float32)[:, None] * xs.astype(jnp.float32)
    y = y.reshape(b, s, SSM_INNER) * jax.nn.silu(z.astype(jnp.float32))
    return rms_norm(y, norm_w).astype(z.dtype)


def setup_inputs(seed: int = 0) -> dict:
    key = jax.random.key(seed)
    ks = jax.random.split(key, 20)
    f32 = jnp.float32

    def nrm(k, shape, scale):
        return jax.random.normal(k, shape, f32) * scale

    x = jax.random.normal(ks[0], (BATCH, SEQ, D_MODEL), f32)
    p = jax.random.normal(ks[1], (DEPTH, BATCH, SEQ, PLE_DIM), f32)
    norm_w = 1.0 + nrm(ks[2], (DEPTH, D_MODEL), 0.01)
    w_in = nrm(ks[3], (DEPTH, D_MODEL, IN_WIDTH), D_MODEL ** -0.5)
    na_rpb = nrm(ks[4], (DEPTH, NA_HEADS, 2 * NA_WIN_ROWS - 1, 2 * NA_WIN_COLS - 1), 0.02)
    conv_w = nrm(ks[5], (DEPTH, SSM_CONV, SSM_CONV_DIM), SSM_CONV ** -0.5)
    conv_b = nrm(ks[6], (DEPTH, SSM_CONV_DIM), 0.01)
    a_log = jnp.log(jax.random.uniform(ks[7], (DEPTH, 2, SSM_HEADS), f32, 1.0, 16.0))
    dt0 = jnp.exp(jax.random.uniform(ks[8], (DEPTH, 2, SSM_HEADS), f32, math.log(1e-3), math.log(1e-1)))
    dt_bias = dt0 + jnp.log(-jnp.expm1(-dt0))
    d_skip = 1.0 + nrm(ks[9], (DEPTH, SSM_HEADS), 0.01)
    ssm_norm_w = 1.0 + nrm(ks[10], (DEPTH, SSM_INNER), 0.01)
    w_oa = nrm(ks[11], (DEPTH, NA_WIDTH, D_MODEL), NA_WIDTH ** -0.5)
    w_ob = nrm(ks[12], (DEPTH, DIL_OUT_WIDTH, D_MODEL), DIL_OUT_WIDTH ** -0.5)
    w_oc = nrm(ks[13], (DEPTH, SSM_INNER, D_MODEL), SSM_INNER ** -0.5)
    w_out = nrm(ks[14], (DEPTH, D_MODEL, D_MODEL), D_MODEL ** -0.5)
    ple_norm_w = 1.0 + nrm(ks[15], (DEPTH, D_MODEL), 0.01)
    w_ple = nrm(ks[16], (DEPTH, PLE_DIM, D_MODEL), PLE_DIM ** -0.5)
    w_ple_gate = nrm(ks[17], (DEPTH, D_MODEL, D_MODEL), D_MODEL ** -0.5)
    final_norm_w = 1.0 + nrm(ks[18], (D_MODEL,), 0.01)
    return {'x': x, 'p': p, 'norm_w': norm_w, 'w_in': w_in, 'na_rpb': na_rpb,
            'conv_w': conv_w, 'conv_b': conv_b, 'a_log': a_log, 'dt_bias': dt_bias,
            'd_skip': d_skip, 'ssm_norm_w': ssm_norm_w, 'w_oa': w_oa, 'w_ob': w_ob,
            'w_oc': w_oc, 'w_out': w_out, 'ple_norm_w': ple_norm_w, 'w_ple': w_ple,
            'w_ple_gate': w_ple_gate, 'final_norm_w': final_norm_w}


def reference(x, p, norm_w, w_in, na_rpb, conv_w, conv_b, a_log, dt_bias, d_skip, ssm_norm_w,
              w_oa, w_ob, w_oc, w_out, ple_norm_w, w_ple, w_ple_gate, final_norm_w):
    b, s, _ = x.shape
    cos, sin = rotary_tables(jnp.arange(s))
    split_idx = [int(v) for v in np.cumsum(IN_SPLITS)[:-1]]
    for i in range(DEPTH):
        h = rms_norm(x, norm_w[i])
        (qa, ka, va, ga, qb, kb, vb, gb, xbc, z, dt_raw, ua, ub, uc) = [
            h @ w for w in jnp.split(w_in[i], split_idx, axis=1)]
        ya = neighbourhood_attention(split_heads(qa, NA_HEADS), split_heads(ka, NA_HEADS),
                                     split_heads(va, NA_HEADS), na_rpb[i]).reshape(b, s, NA_WIDTH)
        ya = (ya * jax.nn.silu(ga)) @ w_oa[i]
        qbh = apply_partial_rotary(split_heads(qb, DIL_HEADS), cos, sin)
        kbh = apply_partial_rotary(split_heads(kb, DIL_HEADS), cos, sin)
        yb = dilated_mixture(qbh, kbh, split_heads(vb, DIL_HEADS)).reshape(b, s, DIL_OUT_WIDTH)
        yb = (yb * jax.nn.silu(gb)) @ w_ob[i]
        yc = mamba2_bidirectional(xbc, z, dt_raw, conv_w[i], conv_b[i], a_log[i], dt_bias[i],
                                  d_skip[i], ssm_norm_w[i]) @ w_oc[i]
        merged = jax.nn.sigmoid(ua) * ya + jax.nn.sigmoid(ub) * yb + jax.nn.sigmoid(uc) * yc
        x = x + merged @ w_out[i]
        gate = jax.nn.sigmoid(rms_norm(x, ple_norm_w[i]) @ w_ple_gate[i])
        x = x + (p[i] @ w_ple[i]) * gate
    return rms_norm(x, final_norm_w)
```

```python
import functools
import math

import numpy as np
import jax
import jax.numpy as jnp
from jax import lax
from jax.experimental import pallas as pl
from jax.experimental.pallas import tpu as pltpu

F32 = jnp.float32
BF16 = jnp.bfloat16

LANES = 128
VMEM_LIMIT_BYTES = 56 * 1024 * 1024

D_MODEL = 1024
GRID_W = 64
HEAD_DIM = 64
EPS = 1e-6
PLE_DIM = 256

NA_HEADS = 16
NA_WIDTH = NA_HEADS * HEAD_DIM
NA_WIN_ROWS = 8
NA_WIN_COLS = 16

DIL_PAIRS = ((128, 1), (512, 4), (2048, 16))
DIL_HEADS_PER_GROUP = 8
DIL_GROUP_WIDTH = DIL_HEADS_PER_GROUP * HEAD_DIM
DIL_WIDTH = DIL_GROUP_WIDTH * len(DIL_PAIRS)
DIL_BLK = 64
ROPE_THETA = 500000.0
ROPE_DIM = HEAD_DIM // 4
ROPE_HALF = ROPE_DIM // 2

SSM_INNER = 1536
SSM_HEADS = 24
SSM_GROUPS = 4
SSM_HEADS_PER_GROUP = SSM_HEADS // SSM_GROUPS
SSM_GROUP_WIDTH = SSM_HEADS_PER_GROUP * HEAD_DIM
SSM_STATE = 128
SSM_CONV = 5
SSM_CHUNK = 128
SSM_CONV_DIM = SSM_INNER + 2 * SSM_GROUPS * SSM_STATE
SSM_XBC_WIDTH = SSM_CONV_DIM + SSM_GROUPS * LANES

IN_SPLITS = (NA_WIDTH, NA_WIDTH, NA_WIDTH, NA_WIDTH,
             DIL_WIDTH, DIL_WIDTH, DIL_WIDTH, DIL_GROUP_WIDTH,
             SSM_CONV_DIM, SSM_INNER, 2 * SSM_HEADS,
             D_MODEL, D_MODEL, D_MODEL)

HEAD_SCALE = HEAD_DIM ** -0.5


def _params(*semantics):
    return pltpu.CompilerParams(dimension_semantics=semantics,
                                vmem_limit_bytes=VMEM_LIMIT_BYTES)


def _rms(x, g):
    return x * lax.rsqrt(jnp.mean(x * x, axis=-1, keepdims=True) + EPS) * g


def _silu(x):
    return x * jax.nn.sigmoid(x)


def _proj_kernel(x_ref, g_ref, w_ref, o_ref, h_ref):
    @pl.when(pl.program_id(1) == 0)
    def _():
        h_ref[...] = _rms(x_ref[...], g_ref[...]).astype(BF16)

    o_ref[...] = jnp.dot(h_ref[...], w_ref[...],
                         preferred_element_type=F32).astype(o_ref.dtype)


def _proj(x2d, g, w, tn, out_dtype, tm=1024):
    m, k = x2d.shape
    n = w.shape[1]
    return pl.pallas_call(
        _proj_kernel,
        grid=(m // tm, n // tn),
        in_specs=[pl.BlockSpec((tm, k), lambda i, j: (i, 0)),
                  pl.BlockSpec((1, k), lambda i, j: (0, 0)),
                  pl.BlockSpec((k, tn), lambda i, j: (0, j))],
        out_specs=pl.BlockSpec((tm, tn), lambda i, j: (i, j)),
        out_shape=jax.ShapeDtypeStruct((m, n), out_dtype),
        scratch_shapes=[pltpu.VMEM((tm, k), BF16)],
        compiler_params=_params("parallel", "arbitrary"),
        name="proj",
    )(x2d, g, w)


def _proj_dil_kernel(x_ref, g_ref, w_ref, c_ref, s1_ref, s2_ref, o_ref, hn_ref, hp_ref, *, d):
    tm = x_ref.shape[1]
    rows = tm // d

    @pl.when(pl.program_id(2) == 0)
    def _():
        if d == 1:
            hp_ref[...] = _rms(x_ref[0], g_ref[...]).astype(BF16)
        else:
            h = _rms(x_ref[0], g_ref[...])
            for cb in range(h.shape[1] // LANES):
                hn_ref[cb] = h[:, cb * LANES:(cb + 1) * LANES]
            for rho in range(d):
                for cb in range(h.shape[1] // LANES):
                    hp_ref[rho * rows:(rho + 1) * rows, cb * LANES:(cb + 1) * LANES] = (
                        hn_ref[cb, pl.ds(rho, rows, stride=d), :].astype(BF16))

    acc = jnp.dot(hp_ref[...], w_ref[...], preferred_element_type=F32)
    tn = acc.shape[1]

    @pl.when(pl.program_id(2) < 2)
    def _():
        c = c_ref[0].reshape(tm, LANES)
        s1 = s1_ref[0].reshape(tm, LANES)
        s2 = s2_ref[0].reshape(tm, LANES)
        for cb in range(tn // LANES):
            blk = acc[:, cb * LANES:(cb + 1) * LANES]
            rot = (blk * c + pltpu.roll(blk, LANES - ROPE_HALF, 1) * s1
                   + pltpu.roll(blk, ROPE_HALF, 1) * s2)
            o_ref[0, :, :, cb * LANES:(cb + 1) * LANES] = (
                rot.astype(o_ref.dtype).reshape(d, rows, LANES))

    @pl.when(pl.program_id(2) >= 2)
    def _():
        o_ref[0] = acc.astype(o_ref.dtype).reshape(d, rows, tn)


def _proj_dil(x3d, g, w, tabs, d, tm=1024):
    b, s, k = x3d.shape
    n = w.shape[1]
    tn = DIL_GROUP_WIDTH
    l = s // d
    rows = tm // d
    c, s1, s2 = tabs
    tab_spec = pl.BlockSpec((1, d, rows, LANES), lambda bi, i, j: (0, 0, i, 0))
    return pl.pallas_call(
        functools.partial(_proj_dil_kernel, d=d),
        grid=(b, s // tm, n // tn),
        in_specs=[pl.BlockSpec((1, tm, k), lambda bi, i, j: (bi, i, 0)),
                  pl.BlockSpec((1, k), lambda bi, i, j: (0, 0)),
                  pl.BlockSpec((k, tn), lambda bi, i, j: (0, j)),
                  tab_spec, tab_spec, tab_spec],
        out_specs=pl.BlockSpec((1, d, rows, tn), lambda bi, i, j: (bi, 0, i, j)),
        out_shape=jax.ShapeDtypeStruct((b, d, l, n), BF16),
        scratch_shapes=[pltpu.VMEM((k // LANES, tm, LANES), F32), pltpu.VMEM((tm, k), BF16)],
        compiler_params=_params("parallel", "parallel", "arbitrary"),
        name=f"proj_dil{d}",
    )(x3d, g, w, c, s1, s2)


def _rotary_tables(s, d):
    inv = ROPE_THETA ** (-jnp.arange(0, ROPE_DIM, 2, dtype=F32) / ROPE_DIM)
    ang = jnp.arange(s).astype(F32)[:, None] * inv[None, :]
    cos, sin = jnp.cos(ang), jnp.sin(ang)
    rest = HEAD_DIM - ROPE_DIM
    zeros8 = jnp.zeros_like(sin)
    c = jnp.concatenate([cos, cos, jnp.ones((s, rest), F32)], axis=1)
    s1 = jnp.concatenate([-sin, zeros8, jnp.zeros((s, rest), F32)], axis=1)
    s2 = jnp.concatenate([zeros8, sin, jnp.zeros((s, rest), F32)], axis=1)

    def arrange(t):
        t = jnp.tile(t, (1, LANES // HEAD_DIM))
        return t.reshape(s // d, d, LANES).transpose(1, 0, 2)[None]

    return arrange(c), arrange(s1), arrange(s2)


def _na_kernel(q_ref, k_ref, v_ref, g_ref, tb_ref, o_ref):
    rows = q_ref.shape[1] // GRID_W
    nkeys = NA_WIN_ROWS * GRID_W
    lane = lax.broadcasted_iota(jnp.int32, (GRID_W, LANES), 1)
    first = lane < HEAD_DIM

    def body(r, carry):
        r0 = jnp.clip(r - NA_WIN_ROWS // 2, 0, rows - NA_WIN_ROWS)
        st = r0 - r + NA_WIN_ROWS - 1
        qs = pl.ds(pl.multiple_of(r * GRID_W, GRID_W), GRID_W)
        ks = pl.ds(pl.multiple_of(r0 * GRID_W, GRID_W), nkeys)
        q = q_ref[0, qs, :] * HEAD_SCALE
        kw = k_ref[0, ks, :]
        vw = v_ref[0, ks, :]
        zero = jnp.zeros_like(q)
        outs = []
        for h in range(2):
            qh = jnp.where(first, q, zero) if h == 0 else jnp.where(first, zero, q)
            sc = lax.dot_general(qh, kw, (((1,), (1,)), ((), ())),
                                 preferred_element_type=F32)
            sc = sc + tb_ref[0, h, st]
            mx = jnp.max(sc, axis=-1, keepdims=True)
            e = jnp.exp(sc - mx)
            den = jnp.sum(e, axis=-1, keepdims=True)
            p = (e / den).astype(BF16)
            outs.append(jnp.dot(p, vw, preferred_element_type=F32))
        o = jnp.where(first, outs[0], outs[1])
        o_ref[0, qs, :] = (o * _silu(g_ref[0, qs, :])).astype(o_ref.dtype)
        return carry

    lax.fori_loop(0, rows, body, 0)


def _na_bias_table(rpb):
    h = rpb.shape[0]
    cq = np.arange(GRID_W)[:, None]
    ck = np.arange(GRID_W)[None, :]
    ws = np.clip(cq - NA_WIN_COLS // 2, 0, GRID_W - NA_WIN_COLS)
    in_win = (ck >= ws) & (ck < ws + NA_WIN_COLS)
    dcol = np.clip(ck - cq + NA_WIN_COLS - 1, 0, 2 * NA_WIN_COLS - 2)
    colb = jnp.where(in_win, rpb.astype(F32)[:, :, dcol], -jnp.inf)
    st = np.arange(NA_WIN_ROWS)[:, None] + np.arange(NA_WIN_ROWS)[None, :]
    tb = colb[:, st]
    tb = tb.transpose(0, 1, 3, 2, 4).reshape(h // 2, 2, NA_WIN_ROWS, GRID_W, NA_WIN_ROWS * GRID_W)
    return tb


def _na_attention(qkvg, tb):
    b, s, _ = qkvg.shape
    npair = NA_WIDTH // LANES

    def col(off):
        return pl.BlockSpec((1, s, LANES), lambda bi, hp: (bi, 0, off + hp))

    return pl.pallas_call(
        _na_kernel,
        grid=(b, npair),
        in_specs=[col(0), col(npair), col(2 * npair), col(3 * npair),
                  pl.BlockSpec((1,) + tb.shape[1:], lambda bi, hp: (hp, 0, 0, 0, 0))],
        out_specs=pl.BlockSpec((1, s, LANES), lambda bi, hp: (bi, 0, hp)),
        out_shape=jax.ShapeDtypeStruct((b, s, NA_WIDTH), BF16),
        compiler_params=_params("parallel", "parallel"),
        name="na_attention",
    )(qkvg, qkvg, qkvg, qkvg, tb)


DIL_QB = 2 * DIL_BLK
DIL_KB = 4 * DIL_BLK


def _dil_kernel(q0, k0, v0, q1, k1, v1, q2, k2, v2, g_ref, o_ref, acc_ref, m_ref, w_ref):
    s = o_ref.shape[1]
    lane = lax.broadcasted_iota(jnp.int32, (DIL_QB, LANES), 1)
    first = lane < HEAD_DIM
    qi = lax.broadcasted_iota(jnp.int32, (DIL_QB, DIL_KB), 0)
    ki = lax.broadcasted_iota(jnp.int32, (DIL_QB, DIL_KB), 1)

    for g, (q_ref, k_ref, v_ref) in enumerate(((q0, k0, v0), (q1, k1, v1), (q2, k2, v2))):
        d = DIL_PAIRS[g][1]
        l = s // d
        nu = l // DIL_QB

        def unit(idx, carry, g=g, d=d, l=l, nu=nu, q_ref=q_ref, k_ref=k_ref, v_ref=v_ref):
            rho = idx // nu
            u = idx % nu
            q_start = pl.multiple_of(u * DIL_QB, DIL_QB)
            k_start = pl.multiple_of(jnp.clip(u * DIL_QB - DIL_BLK, 0, l - DIL_KB), DIL_BLK)
            q = q_ref[0, rho, pl.ds(q_start, DIL_QB), :] * HEAD_SCALE
            kw = k_ref[0, rho, pl.ds(k_start, DIL_KB), :]
            vw = v_ref[0, rho, pl.ds(k_start, DIL_KB), :]
            valid = jnp.abs((ki + k_start) - (qi + q_start)) <= DIL_BLK
            zero = jnp.zeros_like(q)
            outs, lses = [], []
            for h in range(2):
                qh = jnp.where(first, q, zero) if h == 0 else jnp.where(first, zero, q)
                sc = lax.dot_general(qh, kw, (((1,), (1,)), ((), ())),
                                     preferred_element_type=F32)
                sc = jnp.where(valid, sc, -jnp.inf)
                mx = jnp.max(sc, axis=-1, keepdims=True)
                e = jnp.exp(sc - mx)
                den = jnp.sum(e, axis=-1, keepdims=True)
                p = (e / den).astype(BF16)
                outs.append(jnp.dot(p, vw, preferred_element_type=F32))
                lses.append(mx + jnp.log(den))
            o = jnp.where(first, outs[0], outs[1])
            lse = jnp.where(first, lses[0], lses[1])
            if d == 1:
                rows = pl.ds(q_start, DIL_QB)
            else:
                rows = pl.ds(rho + d * q_start, DIL_QB, stride=d)
            if g == 0:
                acc_ref[rows, :] = o
                m_ref[rows, :] = lse
                w_ref[rows, :] = jnp.ones_like(lse)
            else:
                m_old = m_ref[rows, :]
                m_new = jnp.maximum(m_old, lse)
                a = jnp.exp(m_old - m_new)
                bw = jnp.exp(lse - m_new)
                acc_ref[rows, :] = acc_ref[rows, :] * a + o * bw
                w_ref[rows, :] = w_ref[rows, :] * a + bw
                m_ref[rows, :] = m_new
            return carry

        lax.fori_loop(0, d * nu, unit, 0)

    def finish(i, carry):
        rows = pl.ds(pl.multiple_of(i * 256, 256), 256)
        y = acc_ref[rows, :] / w_ref[rows, :]
        o_ref[0, rows, :] = (y * _silu(g_ref[0, rows, :])).astype(o_ref.dtype)
        return carry

    lax.fori_loop(0, s // 256, finish, 0)


def _dil_attention(qkvs, gate, gate_off):
    b = gate.shape[0]
    s = gate.shape[1]
    npair = DIL_GROUP_WIDTH // LANES
    in_specs = []
    args = []
    for arr in qkvs:
        d, l = arr.shape[1], arr.shape[2]
        for part in range(3):
            in_specs.append(pl.BlockSpec((1, d, l, LANES),
                                         lambda bi, sp, part=part: (bi, 0, 0, part * npair + sp)))
            args.append(arr)
    in_specs.append(pl.BlockSpec((1, s, LANES), lambda bi, sp: (bi, 0, gate_off + sp)))
    args.append(gate)
    return pl.pallas_call(
        _dil_kernel,
        grid=(b, npair),
        in_specs=in_specs,
        out_specs=pl.BlockSpec((1, s, LANES), lambda bi, sp: (bi, 0, sp)),
        out_shape=jax.ShapeDtypeStruct((b, s, DIL_GROUP_WIDTH), BF16),
        scratch_shapes=[pltpu.VMEM((s, LANES), F32)] * 3,
        compiler_params=_params("parallel", "parallel"),
        name="dil_attention",
    )(*args)


def _split3(x):
    hi = x.astype(BF16)
    r1 = x - hi.astype(F32)
    mid = r1.astype(BF16)
    lo = (r1 - mid.astype(F32)).astype(BF16)
    return hi, mid, lo


def _ssd_kernel(xs_ref, bm_ref, cm_ref, dt_ref, wx_ref, wb_ref, wc_ref, bx_ref, bb_ref, bc_ref,
                dtb_ref, a_ref, dsk_ref, y_ref, xs_s, b_s, c_s, st_s):
    s = xs_ref.shape[1]
    cl = SSM_CHUNK
    nchunk = s // cl
    hpg = SSM_HEADS_PER_GROUP
    npair = SSM_GROUP_WIDTH // LANES

    def conv_chunk(c, carry):
        base = pl.multiple_of(c * cl, cl)
        prev_start = pl.multiple_of(jnp.maximum(base - 8, 0), 8)
        next_start = pl.multiple_of(jnp.minimum(base + cl, s - 8), 8)
        has_prev = (c > 0).astype(F32)
        has_next = (c < nchunk - 1).astype(F32)
        for src, w_ref, bias_ref, dst in ((xs_ref, wx_ref, bx_ref, xs_s),
                                          (bm_ref, wb_ref, bb_ref, b_s),
                                          (cm_ref, wc_ref, bc_ref, c_s)):
            ext = jnp.concatenate([src[0, pl.ds(prev_start, 8), :] * has_prev,
                                   src[0, pl.ds(base, cl), :],
                                   src[0, pl.ds(next_start, 8), :] * has_next], axis=0)
            acc = bias_ref[0]
            off = 8 - SSM_CONV // 2
            for tap in range(SSM_CONV):
                acc = acc + ext[off + tap:off + tap + cl, :] * w_ref[0, tap:tap + 1, :]
            dst[pl.ds(base, cl), :] = _silu(acc).astype(dst.dtype)
        return carry

    lax.fori_loop(0, nchunk, conv_chunk, 0)

    ri = lax.broadcasted_iota(jnp.int32, (cl, cl), 0)
    ci = lax.broadcasted_iota(jnp.int32, (cl, cl), 1)
    lane = lax.broadcasted_iota(jnp.int32, (cl, LANES), 1)
    first = lane < HEAD_DIM
    dt_bias = dtb_ref[0]
    a_neg = a_ref[0]

    for direction in range(2):
        if direction == 0:
            keep = ci <= ri
            end_row = cl - 1
        else:
            keep = ci >= ri
            end_row = 0
        tri = jnp.where(keep, 1.0, 0.0).astype(BF16)

        st_s[...] = jnp.zeros_like(st_s)

        def chunk(step, carry, direction=direction, keep=keep, tri=tri, end_row=end_row):
            c = step if direction == 0 else nchunk - 1 - step
            base = pl.multiple_of(c * cl, cl)
            rows = pl.ds(base, cl)
            dt = jax.nn.softplus(dt_ref[0, rows, :] + dt_bias)
            a = dt * a_neg
            ah, am, al = _split3(a)
            acum = (jnp.dot(tri, ah, preferred_element_type=F32)
                    + jnp.dot(tri, am, preferred_element_type=F32)
                    + jnp.dot(tri, al, preferred_element_type=F32))
            acum_t = acum.T
            a_end = acum[end_row:end_row + 1, :]
            e_acum = jnp.exp(acum)
            w_state = dt * jnp.exp(a_end - acum)

            xs = xs_s[rows, :]
            bmat = b_s[rows, :]
            cmat = c_s[rows, :]
            cb = lax.dot_general(cmat, bmat, (((1,), (1,)), ((), ())),
                                 preferred_element_type=F32)

            xdt_blocks, xst_blocks, scale_blocks, decay_blocks = [], [], [], []
            y_blocks = []
            for pr in range(npair):
                hcol = [direction * hpg + 2 * pr, direction * hpg + 2 * pr + 1]
                dt_b = [jnp.broadcast_to(dt[:, j:j + 1], (cl, LANES)) for j in hcol]
                ws_b = [jnp.broadcast_to(w_state[:, j:j + 1], (cl, LANES)) for j in hcol]
                ea_b = [jnp.broadcast_to(e_acum[:, j:j + 1], (cl, LANES)) for j in hcol]
                xblk = xs[:, pr * LANES:(pr + 1) * LANES]
                xdt = (xblk * jnp.where(first, dt_b[0], dt_b[1])).astype(BF16)
                xdt_blocks.append(xdt)
                xst_blocks.append((xblk * jnp.where(first, ws_b[0], ws_b[1])).astype(BF16))
                scale = jnp.where(first, ea_b[0], ea_b[1])
                scale_blocks.append(scale)
                decay_blocks.append(scale[end_row:end_row + 1, :])
                yh = []
                for j in hcol:
                    col = jnp.broadcast_to(acum[:, j:j + 1], (cl, cl))
                    row = jnp.broadcast_to(acum_t[j:j + 1, :], (cl, cl))
                    seg = jnp.exp(jnp.where(keep, col - row, -jnp.inf))
                    yh.append(jnp.dot((cb * seg).astype(BF16), xdt, preferred_element_type=F32))
                y_blocks.append(jnp.where(first, yh[0], yh[1]))

            state = st_s[...]
            y_off = jnp.dot(cmat, state.astype(BF16), preferred_element_type=F32)
            y = jnp.concatenate(y_blocks, axis=1) + y_off * jnp.concatenate(scale_blocks, axis=1)
            new = lax.dot_general(bmat, jnp.concatenate(xst_blocks, axis=1),
                                  (((0,), (0,)), ((), ())), preferred_element_type=F32)
            st_s[...] = state * jnp.concatenate(decay_blocks, axis=1) + new
            if direction == 0:
                y_ref[0, rows, :] = y + dsk_ref[0] * xs
            else:
                y_ref[0, rows, :] = y_ref[0, rows, :] + y
            return carry

        lax.fori_loop(0, nchunk, chunk, 0)


def _ssd(xbc, conv_w, conv_b, dt_bias, a_neg, d_skip):
    b, s, _ = xbc.shape
    gw = SSM_GROUP_WIDTH
    b_off = SSM_INNER // LANES
    c_off = b_off + SSM_GROUPS
    dt_off = SSM_CONV_DIM // LANES
    once = pl.Buffered(1)

    def seq(width, off):
        return pl.BlockSpec((1, s, width), lambda bi, g: (bi, 0, off + g), pipeline_mode=once)

    def par(rows, width, off):
        return pl.BlockSpec((1, rows, width), lambda bi, g: (0, 0, off + g))

    return pl.pallas_call(
        _ssd_kernel,
        grid=(b, SSM_GROUPS),
        in_specs=[seq(gw, 0), seq(LANES, b_off), seq(LANES, c_off), seq(LANES, dt_off),
                  par(SSM_CONV, gw, 0), par(SSM_CONV, LANES, b_off), par(SSM_CONV, LANES, c_off),
                  par(1, gw, 0), par(1, LANES, b_off), par(1, LANES, c_off),
                  pl.BlockSpec((1, 1, LANES), lambda bi, g: (g, 0, 0)),
                  pl.BlockSpec((1, 1, LANES), lambda bi, g: (g, 0, 0)),
                  pl.BlockSpec((1, 1, gw), lambda bi, g: (g, 0, 0))],
        out_specs=pl.BlockSpec((1, s, gw), lambda bi, g: (bi, 0, g)),
        out_shape=jax.ShapeDtypeStruct((b, s, SSM_INNER), F32),
        scratch_shapes=[pltpu.VMEM((s, gw), F32), pltpu.VMEM((s, LANES), BF16),
                        pltpu.VMEM((s, LANES), BF16), pltpu.VMEM((SSM_STATE, gw), F32)],
        compiler_params=_params("parallel", "parallel"),
        name="ssd",
    )(xbc, xbc, xbc, xbc, conv_w, conv_w, conv_w, conv_b, conv_b, conv_b, dt_bias, a_neg, d_skip)


def _tail_kernel(x_ref, ya_ref, yb_ref, yc_ref, z_ref, ua_ref, ub_ref, uc_ref, p_ref,
                 nw_ref, woa_ref, wob_ref, woc_ref, wout_ref, pg_ref, wpg_ref, wple_ref, fin_ref,
                 o_ref, *, final):
    def mm(a, w_ref):
        return jnp.dot(a, w_ref[...], preferred_element_type=F32)

    ya = mm(ya_ref[...], woa_ref)
    yb = mm(yb_ref[...], wob_ref)
    yc_in = _rms(yc_ref[...] * _silu(z_ref[...]), nw_ref[...]).astype(BF16)
    yc = mm(yc_in, woc_ref)
    merged = (jax.nn.sigmoid(ua_ref[...]) * ya + jax.nn.sigmoid(ub_ref[...]) * yb
              + jax.nn.sigmoid(uc_ref[...]) * yc)
    x1 = x_ref[...] + mm(merged.astype(BF16), wout_ref)
    gate = jax.nn.sigmoid(mm(_rms(x1, pg_ref[...]).astype(BF16), wpg_ref))
    x2 = x1 + mm(p_ref[...].astype(BF16), wple_ref) * gate
    if final:
        x2 = _rms(x2, fin_ref[...])
    o_ref[...] = x2


MISC_WIDTH = SSM_INNER + DIL_GROUP_WIDTH + 3 * D_MODEL
MISC_GB_LANE_BLOCK = SSM_INNER // LANES
MISC_U_BLOCK = (SSM_INNER + DIL_GROUP_WIDTH) // D_MODEL


def _tail(x2d, ya, yb, yc, misc, p2d, nw, woa, wob, woc, wout, pg, wpg, wple, fin, final, tm=256):
    m = x2d.shape[0]

    def rows(width, off=0):
        return pl.BlockSpec((tm, width), lambda i: (i, off))

    def whole(arr):
        return pl.BlockSpec(arr.shape, lambda i: (0, 0), pipeline_mode=pl.Buffered(1))

    ub = MISC_U_BLOCK
    return pl.pallas_call(
        functools.partial(_tail_kernel, final=final),
        grid=(m // tm,),
        in_specs=[rows(D_MODEL), rows(NA_WIDTH), rows(DIL_GROUP_WIDTH), rows(SSM_INNER),
                  rows(SSM_INNER, 0), rows(D_MODEL, ub), rows(D_MODEL, ub + 1), rows(D_MODEL, ub + 2),
                  rows(PLE_DIM),
                  whole(nw), whole(woa), whole(wob), whole(woc), whole(wout), whole(pg), whole(wpg),
                  whole(wple), whole(fin)],
        out_specs=rows(D_MODEL),
        out_shape=jax.ShapeDtypeStruct((m, D_MODEL), F32),
        compiler_params=_params("parallel"),
        name="tail",
    )(x2d, ya, yb, yc, misc, misc, misc, misc, p2d, nw, woa, wob, woc, wout, pg, wpg, wple, fin)


def _prep_weights(w_in, conv_w, conv_b, a_log, dt_bias, d_skip):
    depth = w_in.shape[0]
    offs = np.concatenate([[0], np.cumsum(IN_SPLITS)])
    (qa, ka, va, ga, qb, kb, vb, gb, xbc, z, dtr, ua, ub, uc) = [
        w_in[:, :, int(offs[i]):int(offs[i + 1])] for i in range(len(IN_SPLITS))]
    w_a = jnp.concatenate([qa, ka, va, ga], axis=2).astype(BF16)
    gw = DIL_GROUP_WIDTH
    w_b = [jnp.concatenate([t[:, :, g * gw:(g + 1) * gw] for t in (qb, kb, vb)], axis=2).astype(BF16)
           for g in range(len(DIL_PAIRS))]
    w_misc = jnp.concatenate([z, gb, ua, ub, uc], axis=2).astype(BF16)

    hpg = SSM_HEADS_PER_GROUP

    def per_group(t):
        lead = t.shape[:-2]
        t = t.reshape(lead + (2, SSM_GROUPS, hpg))
        t = jnp.moveaxis(t, -2, -3).reshape(lead + (SSM_GROUPS, 2 * hpg))
        pad = [(0, 0)] * (t.ndim - 1) + [(0, LANES - 2 * hpg)]
        return jnp.pad(t, pad).reshape(lead + (SSM_GROUPS * LANES,))

    w_dt = per_group(dtr.reshape(depth, D_MODEL, 2, SSM_HEADS))
    w_xbc = jnp.concatenate([xbc, w_dt], axis=2).astype(BF16)
    dtb = per_group(dt_bias.astype(F32)).reshape(depth, SSM_GROUPS, 1, LANES)
    a_neg = per_group(-jnp.exp(a_log.astype(F32))).reshape(depth, SSM_GROUPS, 1, LANES)
    dsk = jnp.repeat(d_skip.astype(F32), HEAD_DIM, axis=1).reshape(depth, SSM_GROUPS, 1, SSM_GROUP_WIDTH)
    cw = conv_w.astype(F32).reshape(depth, 1, SSM_CONV, SSM_CONV_DIM)
    cbias = conv_b.astype(F32).reshape(depth, 1, 1, SSM_CONV_DIM)
    return w_a, w_b, w_misc, w_xbc, dtb, a_neg, dsk, cw, cbias


def kernel(x, p, norm_w, w_in, na_rpb, conv_w, conv_b, a_log, dt_bias, d_skip, ssm_norm_w,
           w_oa, w_ob, w_oc, w_out, ple_norm_w, w_ple, w_ple_gate, final_norm_w):
    b, s, dm = x.shape
    depth = w_in.shape[0]
    m = b * s
    w_a, w_b, w_misc, w_xbc, dtb, a_neg, dsk, cw, cbias = _prep_weights(
        w_in, conv_w, conv_b, a_log, dt_bias, d_skip)
    tabs = [_rotary_tables(s, d) for _, d in DIL_PAIRS]
    row = lambda v: v.astype(F32).reshape(1, -1)
    fin = row(final_norm_w)

    x2d = x.reshape(m, dm)
    for i in range(depth):
        g = row(norm_w[i])
        x3d = x2d.reshape(b, s, dm)
        qkvg = _proj(x2d, g, w_a[i], 1024, BF16).reshape(b, s, 4 * NA_WIDTH)
        misc = _proj(x2d, g, w_misc[i], 1024, F32)
        xbc = _proj(x2d, g, w_xbc[i], 1024, F32).reshape(b, s, SSM_XBC_WIDTH)
        qkvs = [_proj_dil(x3d, g, w_b[gi][i], tabs[gi], d) for gi, (_, d) in enumerate(DIL_PAIRS)]

        ya = _na_attention(qkvg, _na_bias_table(na_rpb[i])).reshape(m, NA_WIDTH)
        yb = _dil_attention(qkvs, misc.reshape(b, s, MISC_WIDTH), MISC_GB_LANE_BLOCK)
        yb = yb.reshape(m, DIL_GROUP_WIDTH)
        yc = _ssd(xbc, cw[i], cbias[i], dtb[i], a_neg[i], dsk[i]).reshape(m, SSM_INNER)

        x2d = _tail(x2d, ya, yb, yc, misc, p[i].reshape(m, PLE_DIM), row(ssm_norm_w[i]),
                    w_oa[i].astype(BF16), w_ob[i].astype(BF16), w_oc[i].astype(BF16),
                    w_out[i].astype(BF16), row(ple_norm_w[i]), w_ple_gate[i].astype(BF16),
                    w_ple[i].astype(BF16), fin, final=(i == depth - 1))
    return x2d.reshape(b, s, dm)
```

```python
import functools
import math

import numpy as np
import jax
import jax.numpy as jnp
from jax import lax
from jax.experimental import pallas as pl
from jax.experimental.pallas import tpu as pltpu

F32 = jnp.float32
BF16 = jnp.bfloat16

LANES = 128
VMEM_LIMIT_BYTES = 56 * 1024 * 1024

D_MODEL = 1024
GRID_W = 64
HEAD_DIM = 64
EPS = 1e-6
PLE_DIM = 256

NA_HEADS = 16
NA_WIDTH = NA_HEADS * HEAD_DIM
NA_WIN_ROWS = 8
NA_WIN_COLS = 16

DIL_PAIRS = ((128, 1), (512, 4), (2048, 16))
DIL_HEADS_PER_GROUP = 8
DIL_GROUP_WIDTH = DIL_HEADS_PER_GROUP * HEAD_DIM
DIL_WIDTH = DIL_GROUP_WIDTH * len(DIL_PAIRS)
DIL_BLK = 64
ROPE_THETA = 500000.0
ROPE_DIM = HEAD_DIM // 4
ROPE_HALF = ROPE_DIM // 2

SSM_INNER = 1536
SSM_HEADS = 24
SSM_GROUPS = 4
SSM_HEADS_PER_GROUP = SSM_HEADS // SSM_GROUPS
SSM_GROUP_WIDTH = SSM_HEADS_PER_GROUP * HEAD_DIM
SSM_STATE = 128
SSM_CONV = 5
SSM_CHUNK = 128
SSM_CONV_DIM = SSM_INNER + 2 * SSM_GROUPS * SSM_STATE
SSM_XBC_WIDTH = SSM_CONV_DIM + SSM_GROUPS * LANES

IN_SPLITS = (NA_WIDTH, NA_WIDTH, NA_WIDTH, NA_WIDTH,
             DIL_WIDTH, DIL_WIDTH, DIL_WIDTH, DIL_GROUP_WIDTH,
             SSM_CONV_DIM, SSM_INNER, 2 * SSM_HEADS,
             D_MODEL, D_MODEL, D_MODEL)

HEAD_SCALE = HEAD_DIM ** -0.5


def _params(*semantics):
    return pltpu.CompilerParams(dimension_semantics=semantics,
                                vmem_limit_bytes=VMEM_LIMIT_BYTES)


def _rms(x, g):
    return x * lax.rsqrt(jnp.mean(x * x, axis=-1, keepdims=True) + EPS) * g


def _silu(x):
    return x * jax.nn.sigmoid(x)


def _proj_kernel(x_ref, g_ref, w_ref, o_ref, h_ref):
    @pl.when(pl.program_id(1) == 0)
    def _():
        h_ref[...] = _rms(x_ref[...], g_ref[...]).astype(BF16)

    o_ref[...] = jnp.dot(h_ref[...], w_ref[...],
                         preferred_element_type=F32).astype(o_ref.dtype)


def _proj(x2d, g, w, tn, out_dtype, tm=1024):
    m, k = x2d.shape
    n = w.shape[1]
    return pl.pallas_call(
        _proj_kernel,
        grid=(m // tm, n // tn),
        in_specs=[pl.BlockSpec((tm, k), lambda i, j: (i, 0)),
                  pl.BlockSpec((1, k), lambda i, j: (0, 0)),
                  pl.BlockSpec((k, tn), lambda i, j: (0, j))],
        out_specs=pl.BlockSpec((tm, tn), lambda i, j: (i, j)),
        out_shape=jax.ShapeDtypeStruct((m, n), out_dtype),
        scratch_shapes=[pltpu.VMEM((tm, k), BF16)],
        compiler_params=_params("parallel", "arbitrary"),
        name="proj",
    )(x2d, g, w)


def _proj_dil_kernel(x_ref, g_ref, w_ref, c_ref, s1_ref, s2_ref, o_ref, hn_ref, hp_ref, *, d):
    tm = x_ref.shape[1]
    rows = tm // d

    @pl.when(pl.program_id(2) == 0)
    def _():
        if d == 1:
            hp_ref[...] = _rms(x_ref[0], g_ref[...]).astype(BF16)
        else:
            h = _rms(x_ref[0], g_ref[...])
            for cb in range(h.shape[1] // LANES):
                hn_ref[cb] = h[:, cb * LANES:(cb + 1) * LANES]
            for rho in range(d):
                for cb in range(h.shape[1] // LANES):
                    hp_ref[rho * rows:(rho + 1) * rows, cb * LANES:(cb + 1) * LANES] = (
                        hn_ref[cb, pl.ds(rho, rows, stride=d), :].astype(BF16))

    acc = jnp.dot(hp_ref[...], w_ref[...], preferred_element_type=F32)
    tn = acc.shape[1]

    @pl.when(pl.program_id(2) < 2)
    def _():
        c = c_ref[0].reshape(tm, LANES)
        s1 = s1_ref[0].reshape(tm, LANES)
        s2 = s2_ref[0].reshape(tm, LANES)
        for cb in range(tn // LANES):
            blk = acc[:, cb * LANES:(cb + 1) * LANES]
            rot = (blk * c + pltpu.roll(blk, LANES - ROPE_HALF, 1) * s1
                   + pltpu.roll(blk, ROPE_HALF, 1) * s2)
            o_ref[0, :, :, cb * LANES:(cb + 1) * LANES] = (
                rot.astype(o_ref.dtype).reshape(d, rows, LANES))

    @pl.when(pl.program_id(2) >= 2)
    def _():
        o_ref[0] = acc.astype(o_ref.dtype).reshape(d, rows, tn)


def _proj_dil(x3d, g, w, tabs, d, tm=1024):
    b, s, k = x3d.shape
    n = w.shape[1]
    tn = DIL_GROUP_WIDTH
    l = s // d
    rows = tm // d
    c, s1, s2 = tabs
    tab_spec = pl.BlockSpec((1, d, rows, LANES), lambda bi, i, j: (0, 0, i, 0))
    return pl.pallas_call(
        functools.partial(_proj_dil_kernel, d=d),
        grid=(b, s // tm, n // tn),
        in_specs=[pl.BlockSpec((1, tm, k), lambda bi, i, j: (bi, i, 0)),
                  pl.BlockSpec((1, k), lambda bi, i, j: (0, 0)),
                  pl.BlockSpec((k, tn), lambda bi, i, j: (0, j)),
                  tab_spec, tab_spec, tab_spec],
        out_specs=pl.BlockSpec((1, d, rows, tn), lambda bi, i, j: (bi, 0, i, j)),
        out_shape=jax.ShapeDtypeStruct((b, d, l, n), BF16),
        scratch_shapes=[pltpu.VMEM((k // LANES, tm, LANES), F32), pltpu.VMEM((tm, k), BF16)],
        compiler_params=_params("parallel", "parallel", "arbitrary"),
        name=f"proj_dil{d}",
    )(x3d, g, w, c, s1, s2)


def _rotary_tables(s, d):
    inv = ROPE_THETA ** (-jnp.arange(0, ROPE_DIM, 2, dtype=F32) / ROPE_DIM)
    ang = jnp.arange(s).astype(F32)[:, None] * inv[None, :]
    cos, sin = jnp.cos(ang), jnp.sin(ang)
    rest = HEAD_DIM - ROPE_DIM
    zeros8 = jnp.zeros_like(sin)
    c = jnp.concatenate([cos, cos, jnp.ones((s, rest), F32)], axis=1)
    s1 = jnp.concatenate([-sin, zeros8, jnp.zeros((s, rest), F32)], axis=1)
    s2 = jnp.concatenate([zeros8, sin, jnp.zeros((s, rest), F32)], axis=1)

    def arrange(t):
        t = jnp.tile(t, (1, LANES // HEAD_DIM))
        return t.reshape(s // d, d, LANES).transpose(1, 0, 2)[None]

    return arrange(c), arrange(s1), arrange(s2)


NA_UNROLL = 8


def _na_kernel(q_ref, k_ref, v_ref, g_ref, tb_ref, o_ref):
    rows = q_ref.shape[1] // GRID_W
    nkeys = NA_WIN_ROWS * GRID_W
    lane = lax.broadcasted_iota(jnp.int32, (GRID_W, LANES), 1)
    first = lane < HEAD_DIM

    def body(step, carry):
        units = []
        for i in range(NA_UNROLL):
            r = step * NA_UNROLL + i
            r0 = jnp.clip(r - NA_WIN_ROWS // 2, 0, rows - NA_WIN_ROWS)
            st = r0 - r + NA_WIN_ROWS - 1
            qs = pl.ds(pl.multiple_of(r * GRID_W, GRID_W), GRID_W)
            ks = pl.ds(pl.multiple_of(r0 * GRID_W, GRID_W), nkeys)
            q = q_ref[0, qs, :] * HEAD_SCALE
            kw = k_ref[0, ks, :]
            zero = jnp.zeros_like(q)
            for h in range(2):
                qh = jnp.where(first, q, zero) if h == 0 else jnp.where(first, zero, q)
                sc = lax.dot_general(qh, kw, (((1,), (1,)), ((), ())),
                                     preferred_element_type=F32)
                units.append((qs, ks, h, sc + tb_ref[0, h, st]))
        probs = []
        for qs, ks, h, sc in units:
            mx = jnp.max(sc, axis=-1, keepdims=True)
            e = jnp.exp(sc - mx)
            den = jnp.sum(e, axis=-1, keepdims=True)
            probs.append((e / den).astype(BF16))
        for i in range(NA_UNROLL):
            qs, ks = units[2 * i][0], units[2 * i][1]
            vw = v_ref[0, ks, :]
            o0 = jnp.dot(probs[2 * i], vw, preferred_element_type=F32)
            o1 = jnp.dot(probs[2 * i + 1], vw, preferred_element_type=F32)
            o = jnp.where(first, o0, o1)
            o_ref[0, qs, :] = (o * _silu(g_ref[0, qs, :])).astype(o_ref.dtype)
        return carry

    lax.fori_loop(0, rows // NA_UNROLL, body, 0)


def _na_bias_table(rpb):
    h = rpb.shape[0]
    cq = np.arange(GRID_W)[:, None]
    ck = np.arange(GRID_W)[None, :]
    ws = np.clip(cq - NA_WIN_COLS // 2, 0, GRID_W - NA_WIN_COLS)
    in_win = (ck >= ws) & (ck < ws + NA_WIN_COLS)
    dcol = np.clip(ck - cq + NA_WIN_COLS - 1, 0, 2 * NA_WIN_COLS - 2)
    colb = jnp.where(in_win, rpb.astype(F32)[:, :, dcol], -jnp.inf)
    st = np.arange(NA_WIN_ROWS)[:, None] + np.arange(NA_WIN_ROWS)[None, :]
    tb = colb[:, st]
    tb = tb.transpose(0, 1, 3, 2, 4).reshape(h // 2, 2, NA_WIN_ROWS, GRID_W, NA_WIN_ROWS * GRID_W)
    return tb


def _na_attention(qkvg, tb):
    b, s, _ = qkvg.shape
    npair = NA_WIDTH // LANES

    def col(off):
        return pl.BlockSpec((1, s, LANES), lambda bi, hp: (bi, 0, off + hp))

    return pl.pallas_call(
        _na_kernel,
        grid=(b, npair),
        in_specs=[col(0), col(npair), col(2 * npair), col(3 * npair),
                  pl.BlockSpec((1,) + tb.shape[1:], lambda bi, hp: (hp, 0, 0, 0, 0))],
        out_specs=pl.BlockSpec((1, s, LANES), lambda bi, hp: (bi, 0, hp)),
        out_shape=jax.ShapeDtypeStruct((b, s, NA_WIDTH), BF16),
        compiler_params=_params("parallel", "parallel"),
        name="na_attention",
    )(qkvg, qkvg, qkvg, qkvg, tb)


DIL_QB = 2 * DIL_BLK
DIL_KB = 4 * DIL_BLK
DIL_UNROLL = 4


def _dil_kernel(q0, k0, v0, q1, k1, v1, q2, k2, v2, g_ref, o_ref, acc_ref, m_ref, w_ref):
    s = o_ref.shape[1]
    lane = lax.broadcasted_iota(jnp.int32, (DIL_QB, LANES), 1)
    first = lane < HEAD_DIM
    qi = lax.broadcasted_iota(jnp.int32, (DIL_QB, DIL_KB), 0)
    ki = lax.broadcasted_iota(jnp.int32, (DIL_QB, DIL_KB), 1)

    for g, (q_ref, k_ref, v_ref) in enumerate(((q0, k0, v0), (q1, k1, v1), (q2, k2, v2))):
        d = DIL_PAIRS[g][1]
        l = s // d
        nu = l // DIL_QB

        def trip(step, carry, g=g, d=d, l=l, nu=nu, q_ref=q_ref, k_ref=k_ref, v_ref=v_ref):
            units = []
            for i in range(DIL_UNROLL):
                idx = step * DIL_UNROLL + i
                rho = idx // nu
                u = idx % nu
                q_start = pl.multiple_of(u * DIL_QB, DIL_QB)
                k_start = pl.multiple_of(jnp.clip(u * DIL_QB - DIL_BLK, 0, l - DIL_KB), DIL_BLK)
                q = q_ref[0, rho, pl.ds(q_start, DIL_QB), :] * HEAD_SCALE
                kw = k_ref[0, rho, pl.ds(k_start, DIL_KB), :]
                valid = jnp.abs((ki + k_start) - (qi + q_start)) <= DIL_BLK
                zero = jnp.zeros_like(q)
                scs = []
                for h in range(2):
                    qh = jnp.where(first, q, zero) if h == 0 else jnp.where(first, zero, q)
                    sc = lax.dot_general(qh, kw, (((1,), (1,)), ((), ())),
                                         preferred_element_type=F32)
                    scs.append(jnp.where(valid, sc, -jnp.inf))
                units.append((rho, q_start, k_start, scs))
            soft = []
            for rho, q_start, k_start, scs in units:
                ps, lses = [], []
                for sc in scs:
                    mx = jnp.max(sc, axis=-1, keepdims=True)
                    e = jnp.exp(sc - mx)
                    den = jnp.sum(e, axis=-1, keepdims=True)
                    ps.append((e / den).astype(BF16))
                    lses.append(mx + jnp.log(den))
                soft.append((ps, jnp.where(first, lses[0], lses[1])))
            for (rho, q_start, k_start, _), (ps, lse) in zip(units, soft):
                vw = v_ref[0, rho, pl.ds(k_start, DIL_KB), :]
                o = jnp.where(first, jnp.dot(ps[0], vw, preferred_element_type=F32),
                              jnp.dot(ps[1], vw, preferred_element_type=F32))
                if d == 1:
                    rows = pl.ds(q_start, DIL_QB)
                else:
                    rows = pl.ds(rho + d * q_start, DIL_QB, stride=d)
                if g == 0:
                    acc_ref[rows, :] = o
                    m_ref[rows, :] = lse
                    w_ref[rows, :] = jnp.ones_like(lse)
                else:
                    m_old = m_ref[rows, :]
                    m_new = jnp.maximum(m_old, lse)
                    a = jnp.exp(m_old - m_new)
                    bw = jnp.exp(lse - m_new)
                    acc_ref[rows, :] = acc_ref[rows, :] * a + o * bw
                    w_ref[rows, :] = w_ref[rows, :] * a + bw
                    m_ref[rows, :] = m_new
            return carry

        lax.fori_loop(0, d * nu // DIL_UNROLL, trip, 0)

    def finish(i, carry):
        rows = pl.ds(pl.multiple_of(i * 256, 256), 256)
        y = acc_ref[rows, :] / w_ref[rows, :]
        o_ref[0, rows, :] = (y * _silu(g_ref[0, rows, :])).astype(o_ref.dtype)
        return carry

    lax.fori_loop(0, s // 256, finish, 0)


def _dil_attention(qkvs, gate, gate_off):
    b = gate.shape[0]
    s = gate.shape[1]
    npair = DIL_GROUP_WIDTH // LANES
    in_specs = []
    args = []
    for arr in qkvs:
        d, l = arr.shape[1], arr.shape[2]
        for part in range(3):
            in_specs.append(pl.BlockSpec((1, d, l, LANES),
                                         lambda bi, sp, part=part: (bi, 0, 0, part * npair + sp)))
            args.append(arr)
    in_specs.append(pl.BlockSpec((1, s, LANES), lambda bi, sp: (bi, 0, gate_off + sp)))
    args.append(gate)
    return pl.pallas_call(
        _dil_kernel,
        grid=(b, npair),
        in_specs=in_specs,
        out_specs=pl.BlockSpec((1, s, LANES), lambda bi, sp: (bi, 0, sp)),
        out_shape=jax.ShapeDtypeStruct((b, s, DIL_GROUP_WIDTH), BF16),
        scratch_shapes=[pltpu.VMEM((s, LANES), F32)] * 3,
        compiler_params=_params("parallel", "parallel"),
        name="dil_attention",
    )(*args)


def _split3(x):
    hi = x.astype(BF16)
    r1 = x - hi.astype(F32)
    mid = r1.astype(BF16)
    lo = (r1 - mid.astype(F32)).astype(BF16)
    return hi, mid, lo


def _ssd_kernel(xs_ref, bm_ref, cm_ref, dt_ref, wx_ref, wb_ref, wc_ref, bx_ref, bb_ref, bc_ref,
                dtb_ref, a_ref, dsk_ref, y_ref, xs_s, b_s, c_s, st_s):
    s = xs_ref.shape[1]
    cl = SSM_CHUNK
    nchunk = s // cl
    hpg = SSM_HEADS_PER_GROUP
    npair = SSM_GROUP_WIDTH // LANES

    def conv_chunk(c, carry):
        base = pl.multiple_of(c * cl, cl)
        prev_start = pl.multiple_of(jnp.maximum(base - 8, 0), 8)
        next_start = pl.multiple_of(jnp.minimum(base + cl, s - 8), 8)
        has_prev = (c > 0).astype(F32)
        has_next = (c < nchunk - 1).astype(F32)
        for src, w_ref, bias_ref, dst in ((xs_ref, wx_ref, bx_ref, xs_s),
                                          (bm_ref, wb_ref, bb_ref, b_s),
                                          (cm_ref, wc_ref, bc_ref, c_s)):
            ext = jnp.concatenate([src[0, pl.ds(prev_start, 8), :] * has_prev,
                                   src[0, pl.ds(base, cl), :],
                                   src[0, pl.ds(next_start, 8), :] * has_next], axis=0)
            acc = bias_ref[0]
            off = 8 - SSM_CONV // 2
            for tap in range(SSM_CONV):
                acc = acc + ext[off + tap:off + tap + cl, :] * w_ref[0, tap:tap + 1, :]
            dst[pl.ds(base, cl), :] = _silu(acc).astype(dst.dtype)
        return carry

    lax.fori_loop(0, nchunk, conv_chunk, 0)

    ri = lax.broadcasted_iota(jnp.int32, (cl, cl), 0)
    ci = lax.broadcasted_iota(jnp.int32, (cl, cl), 1)
    lane = lax.broadcasted_iota(jnp.int32, (cl, LANES), 1)
    first = lane < HEAD_DIM
    dt_bias = dtb_ref[0]
    a_neg = a_ref[0]

    for direction in range(2):
        if direction == 0:
            keep = ci <= ri
            end_row = cl - 1
        else:
            keep = ci >= ri
            end_row = 0
        tri = jnp.where(keep, 1.0, 0.0).astype(BF16)

        st_s[...] = jnp.zeros_like(st_s)

        def chunk(step, carry, direction=direction, keep=keep, tri=tri, end_row=end_row):
            c = step if direction == 0 else nchunk - 1 - step
            base = pl.multiple_of(c * cl, cl)
            rows = pl.ds(base, cl)
            dt = jax.nn.softplus(dt_ref[0, rows, :] + dt_bias)
            a = dt * a_neg
            ah, am, al = _split3(a)
            acum = (jnp.dot(tri, ah, preferred_element_type=F32)
                    + jnp.dot(tri, am, preferred_element_type=F32)
                    + jnp.dot(tri, al, preferred_element_type=F32))
            acum_t = acum.T
            a_end = acum[end_row:end_row + 1, :]
            e_acum = jnp.exp(acum)
            w_state = dt * jnp.exp(a_end - acum)

            xs = xs_s[rows, :]
            bmat = b_s[rows, :]
            cmat = c_s[rows, :]
            cb = lax.dot_general(cmat, bmat, (((1,), (1,)), ((), ())),
                                 preferred_element_type=F32)

            xdt_blocks, xst_blocks, scale_blocks, decay_blocks = [], [], [], []
            y_blocks = []
            for pr in range(npair):
                hcol = [direction * hpg + 2 * pr, direction * hpg + 2 * pr + 1]
                dt_b = [jnp.broadcast_to(dt[:, j:j + 1], (cl, LANES)) for j in hcol]
                ws_b = [jnp.broadcast_to(w_state[:, j:j + 1], (cl, LANES)) for j in hcol]
                ea_b = [jnp.broadcast_to(e_acum[:, j:j + 1], (cl, LANES)) for j in hcol]
                xblk = xs[:, pr * LANES:(pr + 1) * LANES]
                xdt = (xblk * jnp.where(first, dt_b[0], dt_b[1])).astype(BF16)
                xdt_blocks.append(xdt)
                xst_blocks.append((xblk * jnp.where(first, ws_b[0], ws_b[1])).astype(BF16))
                scale = jnp.where(first, ea_b[0], ea_b[1])
                scale_blocks.append(scale)
                decay_blocks.append(scale[end_row:end_row + 1, :])
                yh = []
                for j in hcol:
                    col = jnp.broadcast_to(acum[:, j:j + 1], (cl, cl))
                    row = jnp.broadcast_to(acum_t[j:j + 1, :], (cl, cl))
                    seg = jnp.exp(jnp.where(keep, col - row, -jnp.inf))
                    yh.append(jnp.dot((cb * seg).astype(BF16), xdt, preferred_element_type=F32))
                y_blocks.append(jnp.where(first, yh[0], yh[1]))

            state = st_s[...]
            y_off = jnp.dot(cmat, state.astype(BF16), preferred_element_type=F32)
            y = jnp.concatenate(y_blocks, axis=1) + y_off * jnp.concatenate(scale_blocks, axis=1)
            new = lax.dot_general(bmat, jnp.concatenate(xst_blocks, axis=1),
                                  (((0,), (0,)), ((), ())), preferred_element_type=F32)
            st_s[...] = state * jnp.concatenate(decay_blocks, axis=1) + new
            if direction == 0:
                y_ref[0, rows, :] = y + dsk_ref[0] * xs
            else:
                y_ref[0, rows, :] = y_ref[0, rows, :] + y
            return carry

        lax.fori_loop(0, nchunk, chunk, 0)


def _ssd(xbc, conv_w, conv_b, dt_bias, a_neg, d_skip):
    b, s, _ = xbc.shape
    gw = SSM_GROUP_WIDTH
    b_off = SSM_INNER // LANES
    c_off = b_off + SSM_GROUPS
    dt_off = SSM_CONV_DIM // LANES
    once = pl.Buffered(1)

    def seq(width, off):
        return pl.BlockSpec((1, s, width), lambda bi, g: (bi, 0, off + g), pipeline_mode=once)

    def par(rows, width, off):
        return pl.BlockSpec((1, rows, width), lambda bi, g: (0, 0, off + g))

    return pl.pallas_call(
        _ssd_kernel,
        grid=(b, SSM_GROUPS),
        in_specs=[seq(gw, 0), seq(LANES, b_off), seq(LANES, c_off), seq(LANES, dt_off),
                  par(SSM_CONV, gw, 0), par(SSM_CONV, LANES, b_off), par(SSM_CONV, LANES, c_off),
                  par(1, gw, 0), par(1, LANES, b_off), par(1, LANES, c_off),
                  pl.BlockSpec((1, 1, LANES), lambda bi, g: (g, 0, 0)),
                  pl.BlockSpec((1, 1, LANES), lambda bi, g: (g, 0, 0)),
                  pl.BlockSpec((1, 1, gw), lambda bi, g: (g, 0, 0))],
        out_specs=pl.BlockSpec((1, s, gw), lambda bi, g: (bi, 0, g)),
        out_shape=jax.ShapeDtypeStruct((b, s, SSM_INNER), F32),
        scratch_shapes=[pltpu.VMEM((s, gw), F32), pltpu.VMEM((s, LANES), BF16),
                        pltpu.VMEM((s, LANES), BF16), pltpu.VMEM((SSM_STATE, gw), F32)],
        compiler_params=_params("parallel", "parallel"),
        name="ssd",
    )(xbc, xbc, xbc, xbc, conv_w, conv_w, conv_w, conv_b, conv_b, conv_b, dt_bias, a_neg, d_skip)


def _tail_kernel(x_ref, ya_ref, yb_ref, yc_ref, z_ref, ua_ref, ub_ref, uc_ref, p_ref,
                 nw_ref, woa_ref, wob_ref, woc_ref, wout_ref, pg_ref, wpg_ref, wple_ref, fin_ref,
                 o_ref, *, final):
    def mm(a, w_ref):
        return jnp.dot(a, w_ref[...], preferred_element_type=F32)

    ya = mm(ya_ref[...], woa_ref)
    yb = mm(yb_ref[...], wob_ref)
    yc_in = _rms(yc_ref[...] * _silu(z_ref[...]), nw_ref[...]).astype(BF16)
    yc = mm(yc_in, woc_ref)
    merged = (jax.nn.sigmoid(ua_ref[...]) * ya + jax.nn.sigmoid(ub_ref[...]) * yb
              + jax.nn.sigmoid(uc_ref[...]) * yc)
    x1 = x_ref[...] + mm(merged.astype(BF16), wout_ref)
    gate = jax.nn.sigmoid(mm(_rms(x1, pg_ref[...]).astype(BF16), wpg_ref))
    x2 = x1 + mm(p_ref[...].astype(BF16), wple_ref) * gate
    if final:
        x2 = _rms(x2, fin_ref[...])
    o_ref[...] = x2


MISC_WIDTH = SSM_INNER + DIL_GROUP_WIDTH + 3 * D_MODEL
MISC_GB_LANE_BLOCK = SSM_INNER // LANES
MISC_U_BLOCK = (SSM_INNER + DIL_GROUP_WIDTH) // D_MODEL


def _tail(x2d, ya, yb, yc, misc, p2d, nw, woa, wob, woc, wout, pg, wpg, wple, fin, final, tm=256):
    m = x2d.shape[0]

    def rows(width, off=0):
        return pl.BlockSpec((tm, width), lambda i: (i, off))

    def whole(arr):
        return pl.BlockSpec(arr.shape, lambda i: (0, 0), pipeline_mode=pl.Buffered(1))

    ub = MISC_U_BLOCK
    return pl.pallas_call(
        functools.partial(_tail_kernel, final=final),
        grid=(m // tm,),
        in_specs=[rows(D_MODEL), rows(NA_WIDTH), rows(DIL_GROUP_WIDTH), rows(SSM_INNER),
                  rows(SSM_INNER, 0), rows(D_MODEL, ub), rows(D_MODEL, ub + 1), rows(D_MODEL, ub + 2),
                  rows(PLE_DIM),
                  whole(nw), whole(woa), whole(wob), whole(woc), whole(wout), whole(pg), whole(wpg),
                  whole(wple), whole(fin)],
        out_specs=rows(D_MODEL),
        out_shape=jax.ShapeDtypeStruct((m, D_MODEL), F32),
        compiler_params=_params("parallel"),
        name="tail",
    )(x2d, ya, yb, yc, misc, misc, misc, misc, p2d, nw, woa, wob, woc, wout, pg, wpg, wple, fin)


def _prep_weights(w_in, conv_w, conv_b, a_log, dt_bias, d_skip):
    depth = w_in.shape[0]
    offs = np.concatenate([[0], np.cumsum(IN_SPLITS)])
    (qa, ka, va, ga, qb, kb, vb, gb, xbc, z, dtr, ua, ub, uc) = [
        w_in[:, :, int(offs[i]):int(offs[i + 1])] for i in range(len(IN_SPLITS))]
    w_a = jnp.concatenate([qa, ka, va, ga], axis=2).astype(BF16)
    gw = DIL_GROUP_WIDTH
    w_b = [jnp.concatenate([t[:, :, g * gw:(g + 1) * gw] for t in (qb, kb, vb)], axis=2).astype(BF16)
           for g in range(len(DIL_PAIRS))]
    w_misc = jnp.concatenate([z, gb, ua, ub, uc], axis=2).astype(BF16)

    hpg = SSM_HEADS_PER_GROUP

    def per_group(t):
        lead = t.shape[:-2]
        t = t.reshape(lead + (2, SSM_GROUPS, hpg))
        t = jnp.moveaxis(t, -2, -3).reshape(lead + (SSM_GROUPS, 2 * hpg))
        pad = [(0, 0)] * (t.ndim - 1) + [(0, LANES - 2 * hpg)]
        return jnp.pad(t, pad).reshape(lead + (SSM_GROUPS * LANES,))

    w_dt = per_group(dtr.reshape(depth, D_MODEL, 2, SSM_HEADS))
    w_xbc = jnp.concatenate([xbc, w_dt], axis=2).astype(BF16)
    dtb = per_group(dt_bias.astype(F32)).reshape(depth, SSM_GROUPS, 1, LANES)
    a_neg = per_group(-jnp.exp(a_log.astype(F32))).reshape(depth, SSM_GROUPS, 1, LANES)
    dsk = jnp.repeat(d_skip.astype(F32), HEAD_DIM, axis=1).reshape(depth, SSM_GROUPS, 1, SSM_GROUP_WIDTH)
    cw = conv_w.astype(F32).reshape(depth, 1, SSM_CONV, SSM_CONV_DIM)
    cbias = conv_b.astype(F32).reshape(depth, 1, 1, SSM_CONV_DIM)
    return w_a, w_b, w_misc, w_xbc, dtb, a_neg, dsk, cw, cbias


def kernel(x, p, norm_w, w_in, na_rpb, conv_w, conv_b, a_log, dt_bias, d_skip, ssm_norm_w,
           w_oa, w_ob, w_oc, w_out, ple_norm_w, w_ple, w_ple_gate, final_norm_w):
    b, s, dm = x.shape
    depth = w_in.shape[0]
    m = b * s
    w_a, w_b, w_misc, w_xbc, dtb, a_neg, dsk, cw, cbias = _prep_weights(
        w_in, conv_w, conv_b, a_log, dt_bias, d_skip)
    tabs = [_rotary_tables(s, d) for _, d in DIL_PAIRS]
    row = lambda v: v.astype(F32).reshape(1, -1)
    fin = row(final_norm_w)

    x2d = x.reshape(m, dm)
    for i in range(depth):
        g = row(norm_w[i])
        x3d = x2d.reshape(b, s, dm)
        qkvg = _proj(x2d, g, w_a[i], 1024, BF16).reshape(b, s, 4 * NA_WIDTH)
        misc = _proj(x2d, g, w_misc[i], 1024, F32)
        xbc = _proj(x2d, g, w_xbc[i], 1024, F32).reshape(b, s, SSM_XBC_WIDTH)
        qkvs = [_proj_dil(x3d, g, w_b[gi][i], tabs[gi], d) for gi, (_, d) in enumerate(DIL_PAIRS)]

        ya = _na_attention(qkvg, _na_bias_table(na_rpb[i])).reshape(m, NA_WIDTH)
        yb = _dil_attention(qkvs, misc.reshape(b, s, MISC_WIDTH), MISC_GB_LANE_BLOCK)
        yb = yb.reshape(m, DIL_GROUP_WIDTH)
        yc = _ssd(xbc, cw[i], cbias[i], dtb[i], a_neg[i], dsk[i]).reshape(m, SSM_INNER)

        x2d = _tail(x2d, ya, yb, yc, misc, p[i].reshape(m, PLE_DIM), row(ssm_norm_w[i]),
                    w_oa[i].astype(BF16), w_ob[i].astype(BF16), w_oc[i].astype(BF16),
                    w_out[i].astype(BF16), row(ple_norm_w[i]), w_ple_gate[i].astype(BF16),
                    w_ple[i].astype(BF16), fin, final=(i == depth - 1))
    return x2d.reshape(b, s, dm)
```

```python
import functools
import math

import numpy as np
import jax
import jax.numpy as jnp
from jax import lax
from jax.experimental import pallas as pl
from jax.experimental.pallas import tpu as pltpu

F32 = jnp.float32
BF16 = jnp.bfloat16

LANES = 128
VMEM_LIMIT_BYTES = 56 * 1024 * 1024

D_MODEL = 1024
GRID_W = 64
HEAD_DIM = 64
EPS = 1e-6
PLE_DIM = 256

NA_HEADS = 16
NA_WIDTH = NA_HEADS * HEAD_DIM
NA_WIN_ROWS = 8
NA_WIN_COLS = 16

DIL_PAIRS = ((128, 1), (512, 4), (2048, 16))
DIL_HEADS_PER_GROUP = 8
DIL_GROUP_WIDTH = DIL_HEADS_PER_GROUP * HEAD_DIM
DIL_WIDTH = DIL_GROUP_WIDTH * len(DIL_PAIRS)
DIL_BLK = 64
ROPE_THETA = 500000.0
ROPE_DIM = HEAD_DIM // 4
ROPE_HALF = ROPE_DIM // 2

SSM_INNER = 1536
SSM_HEADS = 24
SSM_GROUPS = 4
SSM_HEADS_PER_GROUP = SSM_HEADS // SSM_GROUPS
SSM_GROUP_WIDTH = SSM_HEADS_PER_GROUP * HEAD_DIM
SSM_STATE = 128
SSM_CONV = 5
SSM_CHUNK = 128
SSM_CONV_DIM = SSM_INNER + 2 * SSM_GROUPS * SSM_STATE

IN_SPLITS = (NA_WIDTH, NA_WIDTH, NA_WIDTH, NA_WIDTH,
             DIL_WIDTH, DIL_WIDTH, DIL_WIDTH, DIL_GROUP_WIDTH,
             SSM_CONV_DIM, SSM_INNER, 2 * SSM_HEADS,
             D_MODEL, D_MODEL, D_MODEL)

HEAD_SCALE = HEAD_DIM ** -0.5


def _params(*semantics):
    return pltpu.CompilerParams(dimension_semantics=semantics,
                                vmem_limit_bytes=VMEM_LIMIT_BYTES)


def _rms(x, g):
    return x * lax.rsqrt(jnp.mean(x * x, axis=-1, keepdims=True) + EPS) * g


def _silu(x):
    x = x.astype(F32)
    return x * jax.nn.sigmoid(x)


def _sigmoid(x):
    return jax.nn.sigmoid(x.astype(F32))


def _proj_kernel(x_ref, g_ref, w_ref, o_ref, h_ref):
    @pl.when(pl.program_id(1) == 0)
    def _():
        h_ref[...] = _rms(x_ref[...], g_ref[...]).astype(BF16)

    o_ref[...] = jnp.dot(h_ref[...], w_ref[...],
                         preferred_element_type=F32).astype(o_ref.dtype)


def _proj(x2d, g, w, tn, out_dtype, tm=1024):
    m, k = x2d.shape
    n = w.shape[1]
    return pl.pallas_call(
        _proj_kernel,
        grid=(m // tm, n // tn),
        in_specs=[pl.BlockSpec((tm, k), lambda i, j: (i, 0)),
                  pl.BlockSpec((1, k), lambda i, j: (0, 0)),
                  pl.BlockSpec((k, tn), lambda i, j: (0, j))],
        out_specs=pl.BlockSpec((tm, tn), lambda i, j: (i, j)),
        out_shape=jax.ShapeDtypeStruct((m, n), out_dtype),
        scratch_shapes=[pltpu.VMEM((tm, k), BF16)],
        compiler_params=_params("parallel", "arbitrary"),
        name="proj",
    )(x2d, g, w)


def _proj_dil_kernel(x_ref, g_ref, w_ref, c_ref, s1_ref, s2_ref, o_ref, hn_ref, hp_ref, *, d):
    tm = x_ref.shape[1]
    rows = tm // d

    @pl.when(pl.program_id(2) == 0)
    def _():
        if d == 1:
            hp_ref[...] = _rms(x_ref[0], g_ref[...]).astype(BF16)
        else:
            h = _rms(x_ref[0], g_ref[...])
            for cb in range(h.shape[1] // LANES):
                hn_ref[cb] = h[:, cb * LANES:(cb + 1) * LANES]
            for rho in range(d):
                for cb in range(h.shape[1] // LANES):
                    hp_ref[rho * rows:(rho + 1) * rows, cb * LANES:(cb + 1) * LANES] = (
                        hn_ref[cb, pl.ds(rho, rows, stride=d), :].astype(BF16))

    acc = jnp.dot(hp_ref[...], w_ref[...], preferred_element_type=F32)
    tn = acc.shape[1]

    @pl.when(pl.program_id(2) < 2)
    def _():
        c = c_ref[0].reshape(tm, LANES)
        s1 = s1_ref[0].reshape(tm, LANES)
        s2 = s2_ref[0].reshape(tm, LANES)
        for cb in range(tn // LANES):
            blk = acc[:, cb * LANES:(cb + 1) * LANES]
            rot = (blk * c + pltpu.roll(blk, LANES - ROPE_HALF, 1) * s1
                   + pltpu.roll(blk, ROPE_HALF, 1) * s2)
            o_ref[0, :, :, cb * LANES:(cb + 1) * LANES] = (
                rot.astype(o_ref.dtype).reshape(d, rows, LANES))

    @pl.when(pl.program_id(2) >= 2)
    def _():
        o_ref[0] = acc.astype(o_ref.dtype).reshape(d, rows, tn)


def _proj_dil(x3d, g, w, tabs, d, tm=1024):
    b, s, k = x3d.shape
    n = w.shape[1]
    tn = DIL_GROUP_WIDTH
    l = s // d
    rows = tm // d
    c, s1, s2 = tabs
    tab_spec = pl.BlockSpec((1, d, rows, LANES), lambda bi, i, j: (0, 0, i, 0))
    return pl.pallas_call(
        functools.partial(_proj_dil_kernel, d=d),
        grid=(b, s // tm, n // tn),
        in_specs=[pl.BlockSpec((1, tm, k), lambda bi, i, j: (bi, i, 0)),
                  pl.BlockSpec((1, k), lambda bi, i, j: (0, 0)),
                  pl.BlockSpec((k, tn), lambda bi, i, j: (0, j)),
                  tab_spec, tab_spec, tab_spec],
        out_specs=pl.BlockSpec((1, d, rows, tn), lambda bi, i, j: (bi, 0, i, j)),
        out_shape=jax.ShapeDtypeStruct((b, d, l, n), BF16),
        scratch_shapes=[pltpu.VMEM((k // LANES, tm, LANES), F32), pltpu.VMEM((tm, k), BF16)],
        compiler_params=_params("parallel", "parallel", "arbitrary"),
        name=f"proj_dil{d}",
    )(x3d, g, w, c, s1, s2)


def _rotary_tables(s, d):
    inv = ROPE_THETA ** (-jnp.arange(0, ROPE_DIM, 2, dtype=F32) / ROPE_DIM)
    ang = jnp.arange(s).astype(F32)[:, None] * inv[None, :]
    cos, sin = jnp.cos(ang), jnp.sin(ang)
    rest = HEAD_DIM - ROPE_DIM
    zeros8 = jnp.zeros_like(sin)
    c = jnp.concatenate([cos, cos, jnp.ones((s, rest), F32)], axis=1)
    s1 = jnp.concatenate([-sin, zeros8, jnp.zeros((s, rest), F32)], axis=1)
    s2 = jnp.concatenate([zeros8, sin, jnp.zeros((s, rest), F32)], axis=1)

    def arrange(t):
        t = jnp.tile(t, (1, LANES // HEAD_DIM))
        return t.reshape(s // d, d, LANES).transpose(1, 0, 2)[None]

    return arrange(c), arrange(s1), arrange(s2)


NA_UNROLL = 8


def _na_kernel(q_ref, k_ref, v_ref, g_ref, tb_ref, o_ref):
    rows = q_ref.shape[1] // GRID_W
    nkeys = NA_WIN_ROWS * GRID_W
    lane = lax.broadcasted_iota(jnp.int32, (GRID_W, LANES), 1)
    first = lane < HEAD_DIM

    def body(step, carry):
        units = []
        for i in range(NA_UNROLL):
            r = step * NA_UNROLL + i
            r0 = jnp.clip(r - NA_WIN_ROWS // 2, 0, rows - NA_WIN_ROWS)
            st = r0 - r + NA_WIN_ROWS - 1
            qs = pl.ds(pl.multiple_of(r * GRID_W, GRID_W), GRID_W)
            ks = pl.ds(pl.multiple_of(r0 * GRID_W, GRID_W), nkeys)
            q = q_ref[0, qs, :] * HEAD_SCALE
            kw = k_ref[0, ks, :]
            zero = jnp.zeros_like(q)
            for h in range(2):
                qh = jnp.where(first, q, zero) if h == 0 else jnp.where(first, zero, q)
                sc = lax.dot_general(qh, kw, (((1,), (1,)), ((), ())),
                                     preferred_element_type=F32)
                units.append((qs, ks, h, sc + tb_ref[0, h, st]))
        probs = []
        for qs, ks, h, sc in units:
            mx = jnp.max(sc, axis=-1, keepdims=True)
            e = jnp.exp(sc - mx)
            den = jnp.sum(e, axis=-1, keepdims=True)
            probs.append((e / den).astype(BF16))
        for i in range(NA_UNROLL):
            qs, ks = units[2 * i][0], units[2 * i][1]
            vw = v_ref[0, ks, :]
            o0 = jnp.dot(probs[2 * i], vw, preferred_element_type=F32)
            o1 = jnp.dot(probs[2 * i + 1], vw, preferred_element_type=F32)
            o = jnp.where(first, o0, o1)
            o_ref[0, qs, :] = (o * _silu(g_ref[0, qs, :])).astype(o_ref.dtype)
        return carry

    lax.fori_loop(0, rows // NA_UNROLL, body, 0)


def _na_bias_table(rpb):
    h = rpb.shape[0]
    cq = np.arange(GRID_W)[:, None]
    ck = np.arange(GRID_W)[None, :]
    ws = np.clip(cq - NA_WIN_COLS // 2, 0, GRID_W - NA_WIN_COLS)
    in_win = (ck >= ws) & (ck < ws + NA_WIN_COLS)
    dcol = np.clip(ck - cq + NA_WIN_COLS - 1, 0, 2 * NA_WIN_COLS - 2)
    colb = jnp.where(in_win, rpb.astype(F32)[:, :, dcol], -jnp.inf)
    st = np.arange(NA_WIN_ROWS)[:, None] + np.arange(NA_WIN_ROWS)[None, :]
    tb = colb[:, st]
    tb = tb.transpose(0, 1, 3, 2, 4).reshape(h // 2, 2, NA_WIN_ROWS, GRID_W, NA_WIN_ROWS * GRID_W)
    return tb


def _na_attention(qkvg, tb):
    b, s, _ = qkvg.shape
    npair = NA_WIDTH // LANES

    def col(off):
        return pl.BlockSpec((1, s, LANES), lambda bi, hp: (bi, 0, off + hp))

    return pl.pallas_call(
        _na_kernel,
        grid=(b, npair),
        in_specs=[col(0), col(npair), col(2 * npair), col(3 * npair),
                  pl.BlockSpec((1,) + tb.shape[1:], lambda bi, hp: (hp, 0, 0, 0, 0))],
        out_specs=pl.BlockSpec((1, s, LANES), lambda bi, hp: (bi, 0, hp)),
        out_shape=jax.ShapeDtypeStruct((b, s, NA_WIDTH), BF16),
        compiler_params=_params("parallel", "parallel"),
        name="na_attention",
    )(qkvg, qkvg, qkvg, qkvg, tb)


DIL_QB = 2 * DIL_BLK
DIL_KB = 4 * DIL_BLK
DIL_UNROLL = 4


def _dil_kernel(q0, k0, v0, q1, k1, v1, q2, k2, v2, g_ref, o_ref, acc_ref, m_ref, w_ref):
    s = o_ref.shape[1]
    lane = lax.broadcasted_iota(jnp.int32, (DIL_QB, LANES), 1)
    first = lane < HEAD_DIM
    qi = lax.broadcasted_iota(jnp.int32, (DIL_QB, DIL_KB), 0)
    ki = lax.broadcasted_iota(jnp.int32, (DIL_QB, DIL_KB), 1)

    for g, (q_ref, k_ref, v_ref) in enumerate(((q0, k0, v0), (q1, k1, v1), (q2, k2, v2))):
        d = DIL_PAIRS[g][1]
        l = s // d
        nu = l // DIL_QB

        def trip(step, carry, g=g, d=d, l=l, nu=nu, q_ref=q_ref, k_ref=k_ref, v_ref=v_ref):
            units = []
            for i in range(DIL_UNROLL):
                idx = step * DIL_UNROLL + i
                rho = idx // nu
                u = idx % nu
                q_start = pl.multiple_of(u * DIL_QB, DIL_QB)
                k_start = pl.multiple_of(jnp.clip(u * DIL_QB - DIL_BLK, 0, l - DIL_KB), DIL_BLK)
                q = q_ref[0, rho, pl.ds(q_start, DIL_QB), :] * HEAD_SCALE
                kw = k_ref[0, rho, pl.ds(k_start, DIL_KB), :]
                valid = jnp.abs((ki + k_start) - (qi + q_start)) <= DIL_BLK
                zero = jnp.zeros_like(q)
                scs = []
                for h in range(2):
                    qh = jnp.where(first, q, zero) if h == 0 else jnp.where(first, zero, q)
                    sc = lax.dot_general(qh, kw, (((1,), (1,)), ((), ())),
                                         preferred_element_type=F32)
                    scs.append(jnp.where(valid, sc, -jnp.inf))
                units.append((rho, q_start, k_start, scs))
            soft = []
            for rho, q_start, k_start, scs in units:
                ps, lses = [], []
                for sc in scs:
                    mx = jnp.max(sc, axis=-1, keepdims=True)
                    e = jnp.exp(sc - mx)
                    den = jnp.sum(e, axis=-1, keepdims=True)
                    ps.append((e / den).astype(BF16))
                    lses.append(mx + jnp.log(den))
                soft.append((ps, jnp.where(first, lses[0], lses[1])))
            for (rho, q_start, k_start, _), (ps, lse) in zip(units, soft):
                vw = v_ref[0, rho, pl.ds(k_start, DIL_KB), :]
                o = jnp.where(first, jnp.dot(ps[0], vw, preferred_element_type=F32),
                              jnp.dot(ps[1], vw, preferred_element_type=F32))
                if d == 1:
                    rows = pl.ds(q_start, DIL_QB)
                else:
                    rows = pl.ds(rho + d * q_start, DIL_QB, stride=d)
                if g == 0:
                    acc_ref[rows, :] = o
                    m_ref[rows, :] = lse
                    w_ref[rows, :] = jnp.ones_like(lse)
                else:
                    m_old = m_ref[rows, :]
                    m_new = jnp.maximum(m_old, lse)
                    a = jnp.exp(m_old - m_new)
                    bw = jnp.exp(lse - m_new)
                    acc_ref[rows, :] = acc_ref[rows, :] * a + o * bw
                    w_ref[rows, :] = w_ref[rows, :] * a + bw
                    m_ref[rows, :] = m_new
            return carry

        lax.fori_loop(0, d * nu // DIL_UNROLL, trip, 0)

    def finish(i, carry):
        rows = pl.ds(pl.multiple_of(i * 256, 256), 256)
        y = acc_ref[rows, :] / w_ref[rows, :]
        o_ref[0, rows, :] = (y * _silu(g_ref[0, rows, :])).astype(o_ref.dtype)
        return carry

    lax.fori_loop(0, s // 256, finish, 0)


def _dil_attention(qkvs, gate, gate_off):
    b = gate.shape[0]
    s = gate.shape[1]
    npair = DIL_GROUP_WIDTH // LANES
    in_specs = []
    args = []
    for arr in qkvs:
        d, l = arr.shape[1], arr.shape[2]
        for part in range(3):
            in_specs.append(pl.BlockSpec((1, d, l, LANES),
                                         lambda bi, sp, part=part: (bi, 0, 0, part * npair + sp)))
            args.append(arr)
    in_specs.append(pl.BlockSpec((1, s, LANES), lambda bi, sp: (bi, 0, gate_off + sp)))
    args.append(gate)
    return pl.pallas_call(
        _dil_kernel,
        grid=(b, npair),
        in_specs=in_specs,
        out_specs=pl.BlockSpec((1, s, LANES), lambda bi, sp: (bi, 0, sp)),
        out_shape=jax.ShapeDtypeStruct((b, s, DIL_GROUP_WIDTH), BF16),
        scratch_shapes=[pltpu.VMEM((s, LANES), F32)] * 3,
        compiler_params=_params("parallel", "parallel"),
        name="dil_attention",
    )(*args)


def _split3(x):
    hi = x.astype(BF16)
    r1 = x - hi.astype(F32)
    mid = r1.astype(BF16)
    lo = (r1 - mid.astype(F32)).astype(BF16)
    return hi, mid, lo


SSM_HALO = 16
SSM_DT_COPIES = 2
SSM_UNROLL = 2


def _ssd_kernel(xs_ref, bm_ref, cm_ref, dt_ref, wx_ref, wb_ref, wc_ref, bx_ref, bb_ref, bc_ref,
                dtb_ref, a_ref, dsk_ref, y_ref, xs_s, bt_s, c_s, cb_s, acum_s, pt_s, st_s):
    s = xs_ref.shape[1]
    cl = SSM_CHUNK
    nchunk = s // cl
    hpg = SSM_HEADS_PER_GROUP
    npair = SSM_GROUP_WIDTH // LANES
    ncol = 2 * hpg

    def conv_chunk(c, carry):
        base = pl.multiple_of(c * cl, cl)
        prev_start = pl.multiple_of(jnp.maximum(base - SSM_HALO, 0), SSM_HALO)
        next_start = pl.multiple_of(jnp.minimum(base + cl, s - SSM_HALO), SSM_HALO)
        has_prev = jnp.where(c > 0, 1.0, 0.0)
        has_next = jnp.where(c < nchunk - 1, 1.0, 0.0)

        def conv(src, w_ref, bias_ref):
            ext = jnp.concatenate([src[0, pl.ds(prev_start, SSM_HALO), :].astype(F32) * has_prev,
                                   src[0, pl.ds(base, cl), :].astype(F32),
                                   src[0, pl.ds(next_start, SSM_HALO), :].astype(F32) * has_next], axis=0)
            acc = bias_ref[0]
            off = SSM_HALO - SSM_CONV // 2
            for tap in range(SSM_CONV):
                acc = acc + ext[off + tap:off + tap + cl, :] * w_ref[0, tap:tap + 1, :]
            return _silu(acc)

        xs_s[pl.ds(base, cl), :] = conv(xs_ref, wx_ref, bx_ref)
        bmat = conv(bm_ref, wb_ref, bb_ref).astype(BF16)
        cmat = conv(cm_ref, wc_ref, bc_ref).astype(BF16)
        bt_s[c] = bmat.astype(F32).T.astype(BF16)
        c_s[pl.ds(base, cl), :] = cmat
        cb_s[c] = lax.dot_general(cmat, bmat, (((1,), (1,)), ((), ())),
                                  preferred_element_type=F32)

        dt = jax.nn.softplus(dt_ref[0, pl.ds(base, cl), :] + dt_bias)
        pieces = _split3(dt * a_neg)
        acum_f = sum(jnp.dot(tril, p, preferred_element_type=F32) for p in pieces)
        acum_b = sum(jnp.dot(triu, p, preferred_element_type=F32) for p in pieces)
        acum = jnp.where(fwd_lane, acum_f, acum_b)
        a_end = jnp.where(fwd_lane[0:1], acum[cl - 1:cl, :], acum[0:1, :])
        w_state = dt * jnp.exp(a_end - acum)
        packed_t = jnp.where(lane < ncol, acum - jnp.log(dt), w_state).T
        acum_s[c] = acum
        pt_s[c] = packed_t[0:SSM_DT_COPIES * ncol, :]
        return carry

    ri = lax.broadcasted_iota(jnp.int32, (cl, cl), 0)
    ci = lax.broadcasted_iota(jnp.int32, (cl, cl), 1)
    lane = lax.broadcasted_iota(jnp.int32, (cl, LANES), 1)
    first = lane < HEAD_DIM
    fwd_lane = (lane % ncol) < hpg
    tril = jnp.where(ci <= ri, 1.0, 0.0).astype(BF16)
    triu = jnp.where(ci >= ri, 1.0, 0.0).astype(BF16)
    dt_bias = dtb_ref[0]
    a_neg = a_ref[0]

    lax.fori_loop(0, nchunk, conv_chunk, 0)

    for direction in range(2):
        if direction == 0:
            keep = ci <= ri
            end_row = cl - 1
        else:
            keep = ci >= ri
            end_row = 0

        st_s[...] = jnp.zeros_like(st_s)

        def local_part(c, direction=direction, keep=keep, end_row=end_row):
            rows = pl.ds(pl.multiple_of(c * cl, cl), cl)
            acum = acum_s[c]
            packed_t = pt_s[c]

            xs = xs_s[rows, :]
            bt = bt_s[c].astype(F32)
            cb = cb_s[c]
            yd_blocks, new_blocks, scale_blocks = [], [], []
            for pr in range(npair):
                xblk = xs[:, pr * LANES:(pr + 1) * LANES].astype(BF16)
                lhs, ea = [], []
                for j in (direction * hpg + 2 * pr, direction * hpg + 2 * pr + 1):
                    col = jnp.broadcast_to(acum[:, j:j + 1], (cl, cl))
                    row = packed_t[j:j + 1, :]
                    ws_row = packed_t[ncol + j:ncol + j + 1, :]
                    seg_dt = jnp.exp(jnp.where(keep, col - row, -jnp.inf))
                    lhs.append((cb * seg_dt).astype(BF16))
                    lhs.append((bt * ws_row).astype(BF16))
                    ea.append(jnp.exp(col))
                prod = jnp.dot(jnp.concatenate(lhs, axis=0), xblk, preferred_element_type=F32)
                yd_blocks.append(jnp.where(first, prod[0:cl], prod[2 * cl:3 * cl]))
                new_blocks.append(jnp.where(first, prod[cl:2 * cl], prod[3 * cl:4 * cl]))
                scale_blocks.append(jnp.where(first, ea[0], ea[1]))
            return rows, xs, yd_blocks, new_blocks, scale_blocks

        def trip(step, carry, direction=direction, end_row=end_row):
            chunks = [step * SSM_UNROLL + i for i in range(SSM_UNROLL)]
            if direction == 1:
                chunks = [nchunk - 1 - c for c in chunks]
            parts = [local_part(c) for c in chunks]
            state = [st_s[:, pr * LANES:(pr + 1) * LANES] for pr in range(npair)]
            for rows, xs, yd_blocks, new_blocks, scale_blocks in parts:
                cmat = c_s[rows, :]
                y_blocks = []
                for pr in range(npair):
                    y_off = jnp.dot(cmat, state[pr].astype(BF16), preferred_element_type=F32)
                    y_blocks.append(yd_blocks[pr] + y_off * scale_blocks[pr])
                    state[pr] = (state[pr] * scale_blocks[pr][end_row:end_row + 1, :]
                                 + new_blocks[pr])
                y = jnp.concatenate(y_blocks, axis=1)
                if direction == 0:
                    y_ref[0, rows, :] = y + dsk_ref[0] * xs
                else:
                    y_ref[0, rows, :] = y_ref[0, rows, :] + y
            for pr in range(npair):
                st_s[:, pr * LANES:(pr + 1) * LANES] = state[pr]
            return carry

        lax.fori_loop(0, nchunk // SSM_UNROLL, trip, 0)


def _ssd(xbc, dtp, conv_w, conv_b, dt_bias, a_neg, d_skip):
    b, s, _ = xbc.shape
    gw = SSM_GROUP_WIDTH
    b_off = SSM_INNER // LANES
    c_off = b_off + SSM_GROUPS
    once = pl.Buffered(1)
    nchunk = s // SSM_CHUNK

    def seq(width, off):
        return pl.BlockSpec((1, s, width), lambda bi, g: (bi, 0, off + g), pipeline_mode=once)

    def par(rows, width, off):
        return pl.BlockSpec((1, rows, width), lambda bi, g: (0, 0, off + g))

    return pl.pallas_call(
        _ssd_kernel,
        grid=(b, SSM_GROUPS),
        in_specs=[seq(gw, 0), seq(LANES, b_off), seq(LANES, c_off), seq(LANES, 0),
                  par(SSM_CONV, gw, 0), par(SSM_CONV, LANES, b_off), par(SSM_CONV, LANES, c_off),
                  par(1, gw, 0), par(1, LANES, b_off), par(1, LANES, c_off),
                  pl.BlockSpec((1, 1, LANES), lambda bi, g: (g, 0, 0)),
                  pl.BlockSpec((1, 1, LANES), lambda bi, g: (g, 0, 0)),
                  pl.BlockSpec((1, 1, gw), lambda bi, g: (g, 0, 0))],
        out_specs=pl.BlockSpec((1, s, gw), lambda bi, g: (bi, 0, g)),
        out_shape=jax.ShapeDtypeStruct((b, s, SSM_INNER), F32),
        scratch_shapes=[pltpu.VMEM((s, gw), F32),
                        pltpu.VMEM((nchunk, SSM_STATE, SSM_CHUNK), BF16),
                        pltpu.VMEM((s, LANES), BF16),
                        pltpu.VMEM((nchunk, SSM_CHUNK, SSM_CHUNK), F32),
                        pltpu.VMEM((nchunk, SSM_CHUNK, LANES), F32),
                        pltpu.VMEM((nchunk, SSM_DT_COPIES * 2 * SSM_HEADS_PER_GROUP, SSM_CHUNK), F32),
                        pltpu.VMEM((SSM_STATE, gw), F32)],
        compiler_params=_params("parallel", "parallel"),
        name="ssd",
    )(xbc, xbc, xbc, dtp, conv_w, conv_w, conv_w, conv_b, conv_b, conv_b, dt_bias, a_neg, d_skip)


def _tail_kernel(x_ref, ya_ref, yb_ref, yc_ref, z_ref, ua_ref, ub_ref, uc_ref, p_ref,
                 nw_ref, woa_ref, wob_ref, woc_ref, wout_ref, pg_ref, wpg_ref, wple_ref, fin_ref,
                 o_ref, *, final):
    def mm(a, w_ref):
        return jnp.dot(a, w_ref[...], preferred_element_type=F32)

    ya = mm(ya_ref[...], woa_ref)
    yb = mm(yb_ref[...], wob_ref)
    yc_in = _rms(yc_ref[...] * _silu(z_ref[...]), nw_ref[...]).astype(BF16)
    yc = mm(yc_in, woc_ref)
    merged = (_sigmoid(ua_ref[...]) * ya + _sigmoid(ub_ref[...]) * yb
              + _sigmoid(uc_ref[...]) * yc)
    x1 = x_ref[...] + mm(merged.astype(BF16), wout_ref)
    gate = jax.nn.sigmoid(mm(_rms(x1, pg_ref[...]).astype(BF16), wpg_ref))
    x2 = x1 + mm(p_ref[...].astype(BF16), wple_ref) * gate
    if final:
        x2 = _rms(x2, fin_ref[...])
    o_ref[...] = x2


MISC_WIDTH = SSM_INNER + DIL_GROUP_WIDTH + 3 * D_MODEL
MISC_GB_LANE_BLOCK = SSM_INNER // LANES
MISC_U_BLOCK = (SSM_INNER + DIL_GROUP_WIDTH) // D_MODEL


def _tail(x2d, ya, yb, yc, misc, p2d, nw, woa, wob, woc, wout, pg, wpg, wple, fin, final, tm=256):
    m = x2d.shape[0]

    def rows(width, off=0):
        return pl.BlockSpec((tm, width), lambda i: (i, off))

    def whole(arr):
        return pl.BlockSpec(arr.shape, lambda i: (0, 0), pipeline_mode=pl.Buffered(1))

    ub = MISC_U_BLOCK
    return pl.pallas_call(
        functools.partial(_tail_kernel, final=final),
        grid=(m // tm,),
        in_specs=[rows(D_MODEL), rows(NA_WIDTH), rows(DIL_GROUP_WIDTH), rows(SSM_INNER),
                  rows(SSM_INNER, 0), rows(D_MODEL, ub), rows(D_MODEL, ub + 1), rows(D_MODEL, ub + 2),
                  rows(PLE_DIM),
                  whole(nw), whole(woa), whole(wob), whole(woc), whole(wout), whole(pg), whole(wpg),
                  whole(wple), whole(fin)],
        out_specs=rows(D_MODEL),
        out_shape=jax.ShapeDtypeStruct((m, D_MODEL), F32),
        compiler_params=_params("parallel"),
        name="tail",
    )(x2d, ya, yb, yc, misc, misc, misc, misc, p2d, nw, woa, wob, woc, wout, pg, wpg, wple, fin)


def _prep_weights(w_in, conv_w, conv_b, a_log, dt_bias, d_skip):
    depth = w_in.shape[0]
    offs = np.concatenate([[0], np.cumsum(IN_SPLITS)])
    (qa, ka, va, ga, qb, kb, vb, gb, xbc, z, dtr, ua, ub, uc) = [
        w_in[:, :, int(offs[i]):int(offs[i + 1])] for i in range(len(IN_SPLITS))]
    w_a = jnp.concatenate([qa, ka, va, ga], axis=2).astype(BF16)
    gw = DIL_GROUP_WIDTH
    w_b = [jnp.concatenate([t[:, :, g * gw:(g + 1) * gw] for t in (qb, kb, vb)], axis=2).astype(BF16)
           for g in range(len(DIL_PAIRS))]
    w_misc = jnp.concatenate([z, gb, ua, ub, uc], axis=2).astype(BF16)

    hpg = SSM_HEADS_PER_GROUP

    def per_group(t):
        lead = t.shape[:-2]
        t = t.reshape(lead + (2, SSM_GROUPS, hpg))
        t = jnp.moveaxis(t, -2, -3).reshape(lead + (SSM_GROUPS, 2 * hpg))
        t = jnp.tile(t, (1,) * (t.ndim - 1) + (SSM_DT_COPIES,))
        pad = [(0, 0)] * (t.ndim - 1) + [(0, LANES - 2 * hpg * SSM_DT_COPIES)]
        return jnp.pad(t, pad).reshape(lead + (SSM_GROUPS * LANES,))

    w_dt = per_group(dtr.reshape(depth, D_MODEL, 2, SSM_HEADS)).astype(BF16)
    w_xbc = xbc.astype(BF16)
    dtb = per_group(dt_bias.astype(F32)).reshape(depth, SSM_GROUPS, 1, LANES)
    a_neg = per_group(-jnp.exp(a_log.astype(F32))).reshape(depth, SSM_GROUPS, 1, LANES)
    dsk = jnp.repeat(d_skip.astype(F32), HEAD_DIM, axis=1).reshape(depth, SSM_GROUPS, 1, SSM_GROUP_WIDTH)
    cw = conv_w.astype(F32).reshape(depth, 1, SSM_CONV, SSM_CONV_DIM)
    cbias = conv_b.astype(F32).reshape(depth, 1, 1, SSM_CONV_DIM)
    return w_a, w_b, w_misc, w_xbc, w_dt, dtb, a_neg, dsk, cw, cbias


def kernel(x, p, norm_w, w_in, na_rpb, conv_w, conv_b, a_log, dt_bias, d_skip, ssm_norm_w,
           w_oa, w_ob, w_oc, w_out, ple_norm_w, w_ple, w_ple_gate, final_norm_w):
    b, s, dm = x.shape
    depth = w_in.shape[0]
    m = b * s
    w_a, w_b, w_misc, w_xbc, w_dt, dtb, a_neg, dsk, cw, cbias = _prep_weights(
        w_in, conv_w, conv_b, a_log, dt_bias, d_skip)
    tabs = [_rotary_tables(s, d) for _, d in DIL_PAIRS]
    row = lambda v: v.astype(F32).reshape(1, -1)
    fin = row(final_norm_w)

    x2d = x.reshape(m, dm)
    for i in range(depth):
        g = row(norm_w[i])
        x3d = x2d.reshape(b, s, dm)
        qkvg = _proj(x2d, g, w_a[i], 1024, BF16).reshape(b, s, 4 * NA_WIDTH)
        misc = _proj(x2d, g, w_misc[i], 1024, BF16)
        xbc = _proj(x2d, g, w_xbc[i], SSM_CONV_DIM // 2, BF16).reshape(b, s, SSM_CONV_DIM)
        dtp = _proj(x2d, g, w_dt[i], SSM_GROUPS * LANES, F32).reshape(b, s, SSM_GROUPS * LANES)
        qkvs = [_proj_dil(x3d, g, w_b[gi][i], tabs[gi], d) for gi, (_, d) in enumerate(DIL_PAIRS)]

        ya = _na_attention(qkvg, _na_bias_table(na_rpb[i])).reshape(m, NA_WIDTH)
        yb = _dil_attention(qkvs, misc.reshape(b, s, MISC_WIDTH), MISC_GB_LANE_BLOCK)
        yb = yb.reshape(m, DIL_GROUP_WIDTH)
        yc = _ssd(xbc, dtp, cw[i], cbias[i], dtb[i], a_neg[i], dsk[i]).reshape(m, SSM_INNER)

        x2d = _tail(x2d, ya, yb, yc, misc, p[i].reshape(m, PLE_DIM), row(ssm_norm_w[i]),
                    w_oa[i].astype(BF16), w_ob[i].astype(BF16), w_oc[i].astype(BF16),
                    w_out[i].astype(BF16), row(ple_norm_w[i]), w_ple_gate[i].astype(BF16),
                    w_ple[i].astype(BF16), fin, final=(i == depth - 1))
    return x2d.reshape(b, s, dm)
```

```python
import functools
import math

import numpy as np
import jax
import jax.numpy as jnp
from jax import lax
from jax.experimental import pallas as pl
from jax.experimental.pallas import tpu as pltpu

F32 = jnp.float32
BF16 = jnp.bfloat16

LANES = 128
VMEM_LIMIT_BYTES = 56 * 1024 * 1024

D_MODEL = 1024
GRID_W = 64
HEAD_DIM = 64
EPS = 1e-6
PLE_DIM = 256

NA_HEADS = 16
NA_WIDTH = NA_HEADS * HEAD_DIM
NA_WIN_ROWS = 8
NA_WIN_COLS = 16

DIL_PAIRS = ((128, 1), (512, 4), (2048, 16))
DIL_HEADS_PER_GROUP = 8
DIL_GROUP_WIDTH = DIL_HEADS_PER_GROUP * HEAD_DIM
DIL_WIDTH = DIL_GROUP_WIDTH * len(DIL_PAIRS)
DIL_BLK = 64
ROPE_THETA = 500000.0
ROPE_DIM = HEAD_DIM // 4
ROPE_HALF = ROPE_DIM // 2

SSM_INNER = 1536
SSM_HEADS = 24
SSM_GROUPS = 4
SSM_HEADS_PER_GROUP = SSM_HEADS // SSM_GROUPS
SSM_GROUP_WIDTH = SSM_HEADS_PER_GROUP * HEAD_DIM
SSM_STATE = 128
SSM_CONV = 5
SSM_CHUNK = 128
SSM_CONV_DIM = SSM_INNER + 2 * SSM_GROUPS * SSM_STATE

IN_SPLITS = (NA_WIDTH, NA_WIDTH, NA_WIDTH, NA_WIDTH,
             DIL_WIDTH, DIL_WIDTH, DIL_WIDTH, DIL_GROUP_WIDTH,
             SSM_CONV_DIM, SSM_INNER, 2 * SSM_HEADS,
             D_MODEL, D_MODEL, D_MODEL)

HEAD_SCALE = HEAD_DIM ** -0.5


def _params(*semantics):
    return pltpu.CompilerParams(dimension_semantics=semantics,
                                vmem_limit_bytes=VMEM_LIMIT_BYTES)


def _rms(x, g):
    return x * lax.rsqrt(jnp.mean(x * x, axis=-1, keepdims=True) + EPS) * g


def _silu(x):
    x = x.astype(F32)
    return x * jax.nn.sigmoid(x)


def _sigmoid(x):
    return jax.nn.sigmoid(x.astype(F32))


def _norm_kernel(x_ref, g_ref, h_ref):
    h_ref[...] = _rms(x_ref[...], g_ref[...]).astype(h_ref.dtype)


def _norm(x2d, g, tm=1024):
    m, k = x2d.shape
    return pl.pallas_call(
        _norm_kernel,
        grid=(m // tm,),
        in_specs=[pl.BlockSpec((tm, k), lambda i: (i, 0)), pl.BlockSpec((1, k), lambda i: (0, 0))],
        out_specs=pl.BlockSpec((tm, k), lambda i: (i, 0)),
        out_shape=jax.ShapeDtypeStruct((m, k), BF16),
        compiler_params=_params("parallel"),
        name="norm",
    )(x2d, g)


def _proj_kernel(h_ref, w_ref, o_ref):
    o_ref[...] = jnp.dot(h_ref[...], w_ref[...],
                         preferred_element_type=F32).astype(o_ref.dtype)


def _proj(h2d, w, tn, out_dtype, tm=1024):
    m, k = h2d.shape
    n = w.shape[1]
    return pl.pallas_call(
        _proj_kernel,
        grid=(m // tm, n // tn),
        in_specs=[pl.BlockSpec((tm, k), lambda i, j: (i, 0)),
                  pl.BlockSpec((k, tn), lambda i, j: (0, j))],
        out_specs=pl.BlockSpec((tm, tn), lambda i, j: (i, j)),
        out_shape=jax.ShapeDtypeStruct((m, n), out_dtype),
        compiler_params=_params("parallel", "arbitrary"),
        name="proj",
    )(h2d, w)


MXU_N = 256


def _proj_dil_kernel(h_ref, w_ref, c_ref, s_ref, o_ref):
    d, rows, k = h_ref.shape[1:]
    tm = d * rows
    tn = w_ref.shape[1]
    hp = h_ref[0].reshape(tm, k)

    @pl.when(pl.program_id(2) < 2)
    def _():
        c = c_ref[0].reshape(tm, LANES)
        sn = s_ref[0].reshape(tm, LANES)
        for nb in range(tn // MXU_N):
            acc = jnp.dot(hp, w_ref[:, nb * MXU_N:(nb + 1) * MXU_N], preferred_element_type=F32)
            for cb in range(MXU_N // LANES):
                blk = acc[:, cb * LANES:(cb + 1) * LANES]
                rot = blk * c + pltpu.roll(blk, LANES // 2, 1) * sn
                lo = nb * MXU_N + cb * LANES
                o_ref[0, :, :, lo:lo + LANES] = rot.astype(o_ref.dtype).reshape(d, rows, LANES)

    @pl.when(pl.program_id(2) >= 2)
    def _():
        acc = jnp.dot(hp, w_ref[...], preferred_element_type=F32)
        o_ref[0] = acc.astype(o_ref.dtype).reshape(d, rows, tn)


def _proj_dil(hp, w, tabs, tm=1024):
    b, d, l, k = hp.shape
    n = w.shape[1]
    tn = DIL_GROUP_WIDTH
    rows = tm // d
    c, sn = tabs
    tab_spec = pl.BlockSpec((1, d, rows, LANES), lambda bi, i, j: (0, 0, i, 0))
    return pl.pallas_call(
        _proj_dil_kernel,
        grid=(b, l // rows, n // tn),
        in_specs=[pl.BlockSpec((1, d, rows, k), lambda bi, i, j: (bi, 0, i, 0)),
                  pl.BlockSpec((k, tn), lambda bi, i, j: (0, j)),
                  tab_spec, tab_spec],
        out_specs=pl.BlockSpec((1, d, rows, tn), lambda bi, i, j: (bi, 0, i, j)),
        out_shape=jax.ShapeDtypeStruct((b, d, l, n), BF16),
        compiler_params=_params("parallel", "parallel", "arbitrary"),
        name=f"proj_dil{d}",
    )(hp, w, c, sn)


def _rotary_lane_sources():
    half = LANES // 2
    head = np.zeros(LANES, np.int64)
    dim = np.zeros(LANES, np.int64)
    for lane in range(LANES):
        off = lane % half
        is_b = lane >= half
        if off < 2 * ROPE_HALF:
            head[lane] = off // ROPE_HALF
            dim[lane] = off % ROPE_HALF + (ROPE_HALF if is_b else 0)
        else:
            head[lane] = 1 if is_b else 0
            dim[lane] = off
    return head, dim


def _rotary_tables(s, d):
    inv = ROPE_THETA ** (-jnp.arange(0, ROPE_DIM, 2, dtype=F32) / ROPE_DIM)
    ang = jnp.arange(s).astype(F32)[:, None] * inv[None, :]
    cos, sin = jnp.cos(ang), jnp.sin(ang)
    _, dim = _rotary_lane_sources()
    lanes = np.arange(LANES)
    rotary = (lanes % (LANES // 2)) < 2 * ROPE_HALF
    is_b = lanes >= LANES // 2
    idx = dim % ROPE_HALF
    c = jnp.where(rotary[None, :], cos[:, idx], 1.0)
    sn = jnp.where(rotary[None, :], jnp.where(is_b[None, :], sin[:, idx], -sin[:, idx]), 0.0)

    def arrange(t):
        return t.reshape(s // d, d, LANES).transpose(1, 0, 2)[None]

    return arrange(c), arrange(sn)


def _rotary_column_order():
    head, dim = _rotary_lane_sources()
    blocks = [(2 * p + head) * HEAD_DIM + dim for p in range(DIL_GROUP_WIDTH // LANES)]
    return np.concatenate(blocks)


NA_UNROLL = 8
NA_FILL_ROWS = 256


def _na_kernel(q_ref, k_ref, v_ref, g_ref, tb_ref, o_ref, va_ref):
    rows = q_ref.shape[1] // GRID_W
    nkeys = NA_WIN_ROWS * GRID_W
    lane = lax.broadcasted_iota(jnp.int32, (GRID_W, LANES), 1)
    first = lane < HEAD_DIM

    def body(step, carry):
        units = []
        for i in range(NA_UNROLL):
            r = step * NA_UNROLL + i
            r0 = jnp.clip(r - NA_WIN_ROWS // 2, 0, rows - NA_WIN_ROWS)
            st = r0 - r + NA_WIN_ROWS - 1
            qs = pl.ds(pl.multiple_of(r * GRID_W, GRID_W), GRID_W)
            ks = pl.ds(pl.multiple_of(r0 * GRID_W, GRID_W), nkeys)
            q = q_ref[0, qs, :] * HEAD_SCALE
            kw = k_ref[0, ks, :]
            zero = jnp.zeros_like(q)
            for h in range(2):
                qh = jnp.where(first, q, zero) if h == 0 else jnp.where(first, zero, q)
                sc = lax.dot_general(qh, kw, (((1,), (1,)), ((), ())),
                                     preferred_element_type=F32)
                units.append((qs, ks, h, sc + tb_ref[0, h, st]))
        probs = []
        for qs, ks, h, sc in units:
            mx = jnp.max(sc, axis=-1, keepdims=True)
            probs.append(jnp.exp(sc - mx).astype(BF16))
        for i in range(NA_UNROLL):
            qs, ks = units[2 * i][0], units[2 * i][1]
            o0 = jnp.dot(probs[2 * i], va_ref[0, ks, :], preferred_element_type=F32)
            o1 = jnp.dot(probs[2 * i + 1], va_ref[1, ks, :], preferred_element_type=F32)
            num = jnp.where(first, o0, o1)
            den = pltpu.roll(jnp.where(first, o1, o0), HEAD_DIM, 1)
            o_ref[0, qs, :] = (num / den * _silu(g_ref[0, qs, :])).astype(o_ref.dtype)
        return carry

    def fill(i, carry):
        sl = pl.ds(pl.multiple_of(i * NA_FILL_ROWS, NA_FILL_ROWS), NA_FILL_ROWS)
        v = v_ref[0, sl, :]
        one = jnp.ones_like(v)
        keep0 = lax.broadcasted_iota(jnp.int32, v.shape, 1) < HEAD_DIM
        va_ref[0, sl, :] = jnp.where(keep0, v, one)
        va_ref[1, sl, :] = jnp.where(keep0, one, v)
        return carry

    lax.fori_loop(0, q_ref.shape[1] // NA_FILL_ROWS, fill, 0)
    lax.fori_loop(0, rows // NA_UNROLL, body, 0)


def _na_bias_table(rpb):
    h = rpb.shape[0]
    cq = np.arange(GRID_W)[:, None]
    ck = np.arange(GRID_W)[None, :]
    ws = np.clip(cq - NA_WIN_COLS // 2, 0, GRID_W - NA_WIN_COLS)
    in_win = (ck >= ws) & (ck < ws + NA_WIN_COLS)
    dcol = np.clip(ck - cq + NA_WIN_COLS - 1, 0, 2 * NA_WIN_COLS - 2)
    colb = jnp.where(in_win, rpb.astype(F32)[:, :, dcol], -jnp.inf)
    st = np.arange(NA_WIN_ROWS)[:, None] + np.arange(NA_WIN_ROWS)[None, :]
    tb = colb[:, st]
    tb = tb.transpose(0, 1, 3, 2, 4).reshape(h // 2, 2, NA_WIN_ROWS, GRID_W, NA_WIN_ROWS * GRID_W)
    return tb


def _na_attention(qkvg, tb):
    b, s, _ = qkvg.shape
    npair = NA_WIDTH // LANES

    def col(off):
        return pl.BlockSpec((1, s, LANES), lambda bi, hp: (bi, 0, off + hp))

    return pl.pallas_call(
        _na_kernel,
        grid=(b, npair),
        in_specs=[col(0), col(npair), col(2 * npair), col(3 * npair),
                  pl.BlockSpec((1,) + tb.shape[1:], lambda bi, hp: (hp, 0, 0, 0, 0))],
        out_specs=pl.BlockSpec((1, s, LANES), lambda bi, hp: (bi, 0, hp)),
        out_shape=jax.ShapeDtypeStruct((b, s, NA_WIDTH), BF16),
        scratch_shapes=[pltpu.VMEM((2, s, LANES), BF16)],
        compiler_params=_params("parallel", "parallel"),
        name="na_attention",
    )(qkvg, qkvg, qkvg, qkvg, tb)


DIL_QB = 2 * DIL_BLK
DIL_KB = 4 * DIL_BLK
DIL_UNROLL = 4


def _dil_kernel(q0, k0, v0, q1, k1, v1, q2, k2, v2, g_ref, o_ref, acc_ref, m_ref, w_ref):
    s = o_ref.shape[1]
    lane = lax.broadcasted_iota(jnp.int32, (DIL_QB, LANES), 1)
    first = lane < HEAD_DIM
    half_off = lane % (LANES // 2)
    q_first = (half_off < ROPE_HALF) | ((half_off >= 2 * ROPE_HALF) & (lane < LANES // 2))
    v_first = lax.broadcasted_iota(jnp.int32, (DIL_KB, LANES), 1) < HEAD_DIM
    qi = lax.broadcasted_iota(jnp.int32, (DIL_QB, DIL_KB), 0)
    ki = lax.broadcasted_iota(jnp.int32, (DIL_QB, DIL_KB), 1)

    for g, (q_ref, k_ref, v_ref) in enumerate(((q0, k0, v0), (q1, k1, v1), (q2, k2, v2))):
        d = DIL_PAIRS[g][1]
        l = s // d
        nu = l // DIL_QB

        def trip(step, carry, g=g, d=d, l=l, nu=nu, q_ref=q_ref, k_ref=k_ref, v_ref=v_ref):
            units = []
            for i in range(DIL_UNROLL):
                idx = step * DIL_UNROLL + i
                rho = idx // nu
                u = idx % nu
                q_start = pl.multiple_of(u * DIL_QB, DIL_QB)
                k_start = pl.multiple_of(jnp.clip(u * DIL_QB - DIL_BLK, 0, l - DIL_KB), DIL_BLK)
                q = q_ref[0, rho, pl.ds(q_start, DIL_QB), :] * HEAD_SCALE
                kw = k_ref[0, rho, pl.ds(k_start, DIL_KB), :]
                valid = jnp.abs((ki + k_start) - (qi + q_start)) <= DIL_BLK
                zero = jnp.zeros_like(q)
                scs = []
                for h in range(2):
                    qh = jnp.where(q_first, q, zero) if h == 0 else jnp.where(q_first, zero, q)
                    sc = lax.dot_general(qh, kw, (((1,), (1,)), ((), ())),
                                         preferred_element_type=F32)
                    scs.append(jnp.where(valid, sc, -jnp.inf))
                units.append((rho, q_start, k_start, scs))
            soft = []
            for rho, q_start, k_start, scs in units:
                es, mxs = [], []
                for sc in scs:
                    mx = jnp.max(sc, axis=-1, keepdims=True)
                    es.append(jnp.exp(sc - mx).astype(BF16))
                    mxs.append(mx)
                soft.append((es, jnp.where(first, mxs[0], mxs[1])))
            for (rho, q_start, k_start, _), (es, mx) in zip(units, soft):
                vw = v_ref[0, rho, pl.ds(k_start, DIL_KB), :]
                one = jnp.ones_like(vw)
                o0 = jnp.dot(es[0], jnp.where(v_first, vw, one), preferred_element_type=F32)
                o1 = jnp.dot(es[1], jnp.where(v_first, one, vw), preferred_element_type=F32)
                num = jnp.where(first, o0, o1)
                den = pltpu.roll(jnp.where(first, o1, o0), HEAD_DIM, 1)
                if d == 1:
                    rows = pl.ds(q_start, DIL_QB)
                else:
                    rows = pl.ds(rho + d * q_start, DIL_QB, stride=d)
                if g == 0:
                    acc_ref[rows, :] = num
                    m_ref[rows, :] = mx
                    w_ref[rows, :] = den
                else:
                    m_old = m_ref[rows, :]
                    m_new = jnp.maximum(m_old, mx)
                    a = jnp.exp(m_old - m_new)
                    bw = jnp.exp(mx - m_new)
                    acc_ref[rows, :] = acc_ref[rows, :] * a + num * bw
                    w_ref[rows, :] = w_ref[rows, :] * a + den * bw
                    m_ref[rows, :] = m_new
            return carry

        lax.fori_loop(0, d * nu // DIL_UNROLL, trip, 0)

    def finish(i, carry):
        rows = pl.ds(pl.multiple_of(i * 256, 256), 256)
        y = acc_ref[rows, :] / w_ref[rows, :]
        o_ref[0, rows, :] = (y * _silu(g_ref[0, rows, :])).astype(o_ref.dtype)
        return carry

    lax.fori_loop(0, s // 256, finish, 0)


def _dil_attention(qkvs, gate, gate_off):
    b = gate.shape[0]
    s = gate.shape[1]
    npair = DIL_GROUP_WIDTH // LANES
    in_specs = []
    args = []
    for arr in qkvs:
        d, l = arr.shape[1], arr.shape[2]
        for part in range(3):
            in_specs.append(pl.BlockSpec((1, d, l, LANES),
                                         lambda bi, sp, part=part: (bi, 0, 0, part * npair + sp)))
            args.append(arr)
    in_specs.append(pl.BlockSpec((1, s, LANES), lambda bi, sp: (bi, 0, gate_off + sp)))
    args.append(gate)
    return pl.pallas_call(
        _dil_kernel,
        grid=(b, npair),
        in_specs=in_specs,
        out_specs=pl.BlockSpec((1, s, LANES), lambda bi, sp: (bi, 0, sp)),
        out_shape=jax.ShapeDtypeStruct((b, s, DIL_GROUP_WIDTH), BF16),
        scratch_shapes=[pltpu.VMEM((s, LANES), F32)] * 3,
        compiler_params=_params("parallel", "parallel"),
        name="dil_attention",
    )(*args)


def _split3(x):
    hi = x.astype(BF16)
    r1 = x - hi.astype(F32)
    mid = r1.astype(BF16)
    lo = (r1 - mid.astype(F32)).astype(BF16)
    return hi, mid, lo


SSM_HALO = 16
SSM_DT_COPIES = 2
SSM_UNROLL = 2


def _ssd_kernel(xs_ref, bm_ref, cm_ref, dt_ref, wx_ref, wb_ref, wc_ref, bx_ref, bb_ref, bc_ref,
                dtb_ref, a_ref, dsk_ref, y_ref, xs_s, bt_s, c_s, cb_s, acum_s, pt_s, st_s):
    s = xs_ref.shape[1]
    cl = SSM_CHUNK
    nchunk = s // cl
    hpg = SSM_HEADS_PER_GROUP
    npair = SSM_GROUP_WIDTH // LANES
    ncol = 2 * hpg

    def conv_chunk(c, carry):
        base = pl.multiple_of(c * cl, cl)
        prev_start = pl.multiple_of(jnp.maximum(base - SSM_HALO, 0), SSM_HALO)
        next_start = pl.multiple_of(jnp.minimum(base + cl, s - SSM_HALO), SSM_HALO)
        has_prev = jnp.where(c > 0, 1.0, 0.0)
        has_next = jnp.where(c < nchunk - 1, 1.0, 0.0)

        def conv(src, w_ref, bias_ref):
            ext = jnp.concatenate([src[0, pl.ds(prev_start, SSM_HALO), :].astype(F32) * has_prev,
                                   src[0, pl.ds(base, cl), :].astype(F32),
                                   src[0, pl.ds(next_start, SSM_HALO), :].astype(F32) * has_next], axis=0)
            acc = bias_ref[0]
            off = SSM_HALO - SSM_CONV // 2
            for tap in range(SSM_CONV):
                acc = acc + ext[off + tap:off + tap + cl, :] * w_ref[0, tap:tap + 1, :]
            return _silu(acc)

        xs_s[pl.ds(base, cl), :] = conv(xs_ref, wx_ref, bx_ref)
        bmat = conv(bm_ref, wb_ref, bb_ref).astype(BF16)
        cmat = conv(cm_ref, wc_ref, bc_ref).astype(BF16)
        bt_s[c] = bmat.astype(F32).T.astype(BF16)
        c_s[pl.ds(base, cl), :] = cmat
        cb_s[c] = lax.dot_general(cmat, bmat, (((1,), (1,)), ((), ())),
                                  preferred_element_type=F32)

        dt = jax.nn.softplus(dt_ref[0, pl.ds(base, cl), :] + dt_bias)
        pieces = _split3(dt * a_neg)
        acum_f = sum(jnp.dot(tril, p, preferred_element_type=F32) for p in pieces)
        acum_b = sum(jnp.dot(triu, p, preferred_element_type=F32) for p in pieces)
        acum = jnp.where(fwd_lane, acum_f, acum_b)
        a_end = jnp.where(fwd_lane[0:1], acum[cl - 1:cl, :], acum[0:1, :])
        w_state = dt * jnp.exp(a_end - acum)
        packed_t = jnp.where(lane < ncol, acum - jnp.log(dt), w_state).T
        acum_s[c] = acum
        pt_s[c] = packed_t[0:SSM_DT_COPIES * ncol, :]
        return carry

    ri = lax.broadcasted_iota(jnp.int32, (cl, cl), 0)
    ci = lax.broadcasted_iota(jnp.int32, (cl, cl), 1)
    lane = lax.broadcasted_iota(jnp.int32, (cl, LANES), 1)
    first = lane < HEAD_DIM
    fwd_lane = (lane % ncol) < hpg
    tril = jnp.where(ci <= ri, 1.0, 0.0).astype(BF16)
    triu = jnp.where(ci >= ri, 1.0, 0.0).astype(BF16)
    dt_bias = dtb_ref[0]
    a_neg = a_ref[0]

    lax.fori_loop(0, nchunk, conv_chunk, 0)

    for direction in range(2):
        if direction == 0:
            keep = ci <= ri
            end_row = cl - 1
        else:
            keep = ci >= ri
            end_row = 0

        st_s[...] = jnp.zeros_like(st_s)

        def local_part(c, direction=direction, keep=keep, end_row=end_row):
            rows = pl.ds(pl.multiple_of(c * cl, cl), cl)
            acum = acum_s[c]
            packed_t = pt_s[c]

            xs = xs_s[rows, :]
            bt = bt_s[c].astype(F32)
            cb = cb_s[c]
            yd_blocks, new_blocks, scale_blocks = [], [], []
            for pr in range(npair):
                xblk = xs[:, pr * LANES:(pr + 1) * LANES].astype(BF16)
                lhs, ea = [], []
                for j in (direction * hpg + 2 * pr, direction * hpg + 2 * pr + 1):
                    col = jnp.broadcast_to(acum[:, j:j + 1], (cl, cl))
                    row = packed_t[j:j + 1, :]
                    ws_row = packed_t[ncol + j:ncol + j + 1, :]
                    seg_dt = jnp.exp(jnp.where(keep, col - row, -jnp.inf))
                    lhs.append((cb * seg_dt).astype(BF16))
                    lhs.append((bt * ws_row).astype(BF16))
                    ea.append(jnp.exp(col))
                prod = jnp.dot(jnp.concatenate(lhs, axis=0), xblk, preferred_element_type=F32)
                yd_blocks.append(jnp.where(first, prod[0:cl], prod[2 * cl:3 * cl]))
                new_blocks.append(jnp.where(first, prod[cl:2 * cl], prod[3 * cl:4 * cl]))
                scale_blocks.append(jnp.where(first, ea[0], ea[1]))
            return rows, xs, yd_blocks, new_blocks, scale_blocks

        def trip(step, carry, direction=direction, end_row=end_row):
            chunks = [step * SSM_UNROLL + i for i in range(SSM_UNROLL)]
            if direction == 1:
                chunks = [nchunk - 1 - c for c in chunks]
            parts = [local_part(c) for c in chunks]
            state = [st_s[:, pr * LANES:(pr + 1) * LANES] for pr in range(npair)]
            for rows, xs, yd_blocks, new_blocks, scale_blocks in parts:
                cmat = c_s[rows, :]
                y_blocks = []
                for pr in range(npair):
                    y_off = jnp.dot(cmat, state[pr].astype(BF16), preferred_element_type=F32)
                    y_blocks.append(yd_blocks[pr] + y_off * scale_blocks[pr])
                    state[pr] = (state[pr] * scale_blocks[pr][end_row:end_row + 1, :]
                                 + new_blocks[pr])
                y = jnp.concatenate(y_blocks, axis=1)
                if direction == 0:
                    y_ref[0, rows, :] = y + dsk_ref[0] * xs
                else:
                    y_ref[0, rows, :] = y_ref[0, rows, :] + y
            for pr in range(npair):
                st_s[:, pr * LANES:(pr + 1) * LANES] = state[pr]
            return carry

        lax.fori_loop(0, nchunk // SSM_UNROLL, trip, 0)


def _ssd(xbc, dtp, conv_w, conv_b, dt_bias, a_neg, d_skip):
    b, s, _ = xbc.shape
    gw = SSM_GROUP_WIDTH
    b_off = SSM_INNER // LANES
    c_off = b_off + SSM_GROUPS
    once = pl.Buffered(1)
    nchunk = s // SSM_CHUNK

    def seq(width, off):
        return pl.BlockSpec((1, s, width), lambda bi, g: (bi, 0, off + g), pipeline_mode=once)

    def par(rows, width, off):
        return pl.BlockSpec((1, rows, width), lambda bi, g: (0, 0, off + g))

    return pl.pallas_call(
        _ssd_kernel,
        grid=(b, SSM_GROUPS),
        in_specs=[seq(gw, 0), seq(LANES, b_off), seq(LANES, c_off), seq(LANES, 0),
                  par(SSM_CONV, gw, 0), par(SSM_CONV, LANES, b_off), par(SSM_CONV, LANES, c_off),
                  par(1, gw, 0), par(1, LANES, b_off), par(1, LANES, c_off),
                  pl.BlockSpec((1, 1, LANES), lambda bi, g: (g, 0, 0)),
                  pl.BlockSpec((1, 1, LANES), lambda bi, g: (g, 0, 0)),
                  pl.BlockSpec((1, 1, gw), lambda bi, g: (g, 0, 0))],
        out_specs=pl.BlockSpec((1, s, gw), lambda bi, g: (bi, 0, g)),
        out_shape=jax.ShapeDtypeStruct((b, s, SSM_INNER), F32),
        scratch_shapes=[pltpu.VMEM((s, gw), F32),
                        pltpu.VMEM((nchunk, SSM_STATE, SSM_CHUNK), BF16),
                        pltpu.VMEM((s, LANES), BF16),
                        pltpu.VMEM((nchunk, SSM_CHUNK, SSM_CHUNK), F32),
                        pltpu.VMEM((nchunk, SSM_CHUNK, LANES), F32),
                        pltpu.VMEM((nchunk, SSM_DT_COPIES * 2 * SSM_HEADS_PER_GROUP, SSM_CHUNK), F32),
                        pltpu.VMEM((SSM_STATE, gw), F32)],
        compiler_params=_params("parallel", "parallel"),
        name="ssd",
    )(xbc, xbc, xbc, dtp, conv_w, conv_w, conv_w, conv_b, conv_b, conv_b, dt_bias, a_neg, d_skip)


def _tail_kernel(x_ref, ya_ref, yb_ref, yc_ref, z_ref, ua_ref, ub_ref, uc_ref, p_ref,
                 nw_ref, woa_ref, wob_ref, woc_ref, wout_ref, pg_ref, wpg_ref, wple_ref, gn_ref,
                 o_ref, *h_ref, final):
    def mm(a, w_ref):
        return jnp.dot(a, w_ref[...], preferred_element_type=F32)

    ya = mm(ya_ref[...], woa_ref)
    yb = mm(yb_ref[...], wob_ref)
    yc_in = _rms(yc_ref[...] * _silu(z_ref[...]), nw_ref[...]).astype(BF16)
    yc = mm(yc_in, woc_ref)
    merged = (_sigmoid(ua_ref[...]) * ya + _sigmoid(ub_ref[...]) * yb
              + _sigmoid(uc_ref[...]) * yc)
    x1 = x_ref[...] + mm(merged.astype(BF16), wout_ref)
    gate = jax.nn.sigmoid(mm(_rms(x1, pg_ref[...]).astype(BF16), wpg_ref))
    x2 = x1 + mm(p_ref[...].astype(BF16), wple_ref) * gate
    if final:
        o_ref[...] = _rms(x2, gn_ref[...])
    else:
        o_ref[...] = x2
        h_ref[0][...] = _rms(x2, gn_ref[...]).astype(BF16)


MISC_WIDTH = SSM_INNER + DIL_GROUP_WIDTH + 3 * D_MODEL
MISC_GB_LANE_BLOCK = SSM_INNER // LANES
MISC_U_BLOCK = (SSM_INNER + DIL_GROUP_WIDTH) // D_MODEL


def _tail(x2d, ya, yb, yc, misc, p2d, nw, woa, wob, woc, wout, pg, wpg, wple, gn, final, tm=512):
    m = x2d.shape[0]
    out_specs = pl.BlockSpec((tm, D_MODEL), lambda i: (i, 0))
    out_shape = jax.ShapeDtypeStruct((m, D_MODEL), F32)
    if not final:
        out_specs = [out_specs, pl.BlockSpec((tm, D_MODEL), lambda i: (i, 0))]
        out_shape = [out_shape, jax.ShapeDtypeStruct((m, D_MODEL), BF16)]

    def rows(width, off=0):
        return pl.BlockSpec((tm, width), lambda i: (i, off))

    def whole(arr):
        return pl.BlockSpec(arr.shape, lambda i: (0, 0), pipeline_mode=pl.Buffered(1))

    ub = MISC_U_BLOCK
    return pl.pallas_call(
        functools.partial(_tail_kernel, final=final),
        grid=(m // tm,),
        in_specs=[rows(D_MODEL), rows(NA_WIDTH), rows(DIL_GROUP_WIDTH), rows(SSM_INNER),
                  rows(SSM_INNER, 0), rows(D_MODEL, ub), rows(D_MODEL, ub + 1), rows(D_MODEL, ub + 2),
                  rows(PLE_DIM),
                  whole(nw), whole(woa), whole(wob), whole(woc), whole(wout), whole(pg), whole(wpg),
                  whole(wple), whole(gn)],
        out_specs=out_specs,
        out_shape=out_shape,
        compiler_params=_params("parallel"),
        name="tail",
    )(x2d, ya, yb, yc, misc, misc, misc, misc, p2d, nw, woa, wob, woc, wout, pg, wpg, wple, gn)


def _prep_weights(w_in, conv_w, conv_b, a_log, dt_bias, d_skip):
    depth = w_in.shape[0]
    offs = np.concatenate([[0], np.cumsum(IN_SPLITS)])
    (qa, ka, va, ga, qb, kb, vb, gb, xbc, z, dtr, ua, ub, uc) = [
        w_in[:, :, int(offs[i]):int(offs[i + 1])] for i in range(len(IN_SPLITS))]
    w_a = jnp.concatenate([qa, ka, va, ga], axis=2).astype(BF16)
    gw = DIL_GROUP_WIDTH
    order = _rotary_column_order()
    w_b = [jnp.concatenate([qb[:, :, g * gw:(g + 1) * gw][:, :, order],
                            kb[:, :, g * gw:(g + 1) * gw][:, :, order],
                            vb[:, :, g * gw:(g + 1) * gw]], axis=2).astype(BF16)
           for g in range(len(DIL_PAIRS))]
    w_misc = jnp.concatenate([z, gb, ua, ub, uc], axis=2).astype(BF16)

    hpg = SSM_HEADS_PER_GROUP

    def per_group(t):
        lead = t.shape[:-2]
        t = t.reshape(lead + (2, SSM_GROUPS, hpg))
        t = jnp.moveaxis(t, -2, -3).reshape(lead + (SSM_GROUPS, 2 * hpg))
        t = jnp.tile(t, (1,) * (t.ndim - 1) + (SSM_DT_COPIES,))
        pad = [(0, 0)] * (t.ndim - 1) + [(0, LANES - 2 * hpg * SSM_DT_COPIES)]
        return jnp.pad(t, pad).reshape(lead + (SSM_GROUPS * LANES,))

    w_dt = per_group(dtr.reshape(depth, D_MODEL, 2, SSM_HEADS)).astype(BF16)
    w_xbc = xbc.astype(BF16)
    dtb = per_group(dt_bias.astype(F32)).reshape(depth, SSM_GROUPS, 1, LANES)
    a_neg = per_group(-jnp.exp(a_log.astype(F32))).reshape(depth, SSM_GROUPS, 1, LANES)
    dsk = jnp.repeat(d_skip.astype(F32), HEAD_DIM, axis=1).reshape(depth, SSM_GROUPS, 1, SSM_GROUP_WIDTH)
    cw = conv_w.astype(F32).reshape(depth, 1, SSM_CONV, SSM_CONV_DIM)
    cbias = conv_b.astype(F32).reshape(depth, 1, 1, SSM_CONV_DIM)
    return w_a, w_b, w_misc, w_xbc, w_dt, dtb, a_neg, dsk, cw, cbias


def kernel(x, p, norm_w, w_in, na_rpb, conv_w, conv_b, a_log, dt_bias, d_skip, ssm_norm_w,
           w_oa, w_ob, w_oc, w_out, ple_norm_w, w_ple, w_ple_gate, final_norm_w):
    b, s, dm = x.shape
    depth = w_in.shape[0]
    m = b * s
    w_a, w_b, w_misc, w_xbc, w_dt, dtb, a_neg, dsk, cw, cbias = _prep_weights(
        w_in, conv_w, conv_b, a_log, dt_bias, d_skip)
    tabs = [_rotary_tables(s, d) for _, d in DIL_PAIRS]
    row = lambda v: v.astype(F32).reshape(1, -1)

    x2d = x.reshape(m, dm)
    h = _norm(x2d, row(norm_w[0]))
    for i in range(depth):
        final = i == depth - 1
        qkvg = _proj(h, w_a[i], 1024, BF16).reshape(b, s, 4 * NA_WIDTH)
        misc = _proj(h, w_misc[i], 1024, BF16)
        xbc = _proj(h, w_xbc[i], SSM_CONV_DIM // 2, BF16).reshape(b, s, SSM_CONV_DIM)
        dtp = _proj(h, w_dt[i], SSM_GROUPS * LANES, F32).reshape(b, s, SSM_GROUPS * LANES)
        qkvs = [_proj_dil(h.reshape(b, s // d, d, dm).transpose(0, 2, 1, 3), w_b[gi][i], tabs[gi])
                for gi, (_, d) in enumerate(DIL_PAIRS)]

        ya = _na_attention(qkvg, _na_bias_table(na_rpb[i])).reshape(m, NA_WIDTH)
        yb = _dil_attention(qkvs, misc.reshape(b, s, MISC_WIDTH), MISC_GB_LANE_BLOCK)
        yb = yb.reshape(m, DIL_GROUP_WIDTH)
        yc = _ssd(xbc, dtp, cw[i], cbias[i], dtb[i], a_neg[i], dsk[i]).reshape(m, SSM_INNER)

        gn = row(final_norm_w) if final else row(norm_w[i + 1])
        out = _tail(x2d, ya, yb, yc, misc, p[i].reshape(m, PLE_DIM), row(ssm_norm_w[i]),
                    w_oa[i].astype(BF16), w_ob[i].astype(BF16), w_oc[i].astype(BF16),
                    w_out[i].astype(BF16), row(ple_norm_w[i]), w_ple_gate[i].astype(BF16),
                    w_ple[i].astype(BF16), gn, final=final)
        if final:
            x2d = out
        else:
            x2d, h = out
    return x2d.reshape(b, s, dm)
```

```python
import functools
import math

import numpy as np
import jax
import jax.numpy as jnp
from jax import lax
from jax.experimental import pallas as pl
from jax.experimental.pallas import tpu as pltpu

F32 = jnp.float32
BF16 = jnp.bfloat16

LANES = 128
VMEM_LIMIT_BYTES = 56 * 1024 * 1024

D_MODEL = 1024
GRID_W = 64
HEAD_DIM = 64
EPS = 1e-6
PLE_DIM = 256

NA_HEADS = 16
NA_WIDTH = NA_HEADS * HEAD_DIM
NA_WIN_ROWS = 8
NA_WIN_COLS = 16

DIL_PAIRS = ((128, 1), (512, 4), (2048, 16))
DIL_HEADS_PER_GROUP = 8
DIL_GROUP_WIDTH = DIL_HEADS_PER_GROUP * HEAD_DIM
DIL_WIDTH = DIL_GROUP_WIDTH * len(DIL_PAIRS)
DIL_BLK = 64
ROPE_THETA = 500000.0
ROPE_DIM = HEAD_DIM // 4
ROPE_HALF = ROPE_DIM // 2

SSM_INNER = 1536
SSM_HEADS = 24
SSM_GROUPS = 4
SSM_HEADS_PER_GROUP = SSM_HEADS // SSM_GROUPS
SSM_GROUP_WIDTH = SSM_HEADS_PER_GROUP * HEAD_DIM
SSM_STATE = 128
SSM_CONV = 5
SSM_CHUNK = 128
SSM_CONV_DIM = SSM_INNER + 2 * SSM_GROUPS * SSM_STATE

IN_SPLITS = (NA_WIDTH, NA_WIDTH, NA_WIDTH, NA_WIDTH,
             DIL_WIDTH, DIL_WIDTH, DIL_WIDTH, DIL_GROUP_WIDTH,
             SSM_CONV_DIM, SSM_INNER, 2 * SSM_HEADS,
             D_MODEL, D_MODEL, D_MODEL)

HEAD_SCALE = HEAD_DIM ** -0.5


def _params(*semantics):
    return pltpu.CompilerParams(dimension_semantics=semantics,
                                vmem_limit_bytes=VMEM_LIMIT_BYTES)


def _rms(x, g):
    return x * lax.rsqrt(jnp.mean(x * x, axis=-1, keepdims=True) + EPS) * g


def _silu(x):
    x = x.astype(F32)
    return x * jax.nn.sigmoid(x)


def _sigmoid(x):
    return jax.nn.sigmoid(x.astype(F32))


def _norm_kernel(x_ref, g_ref, h_ref):
    h_ref[...] = _rms(x_ref[...], g_ref[...]).astype(h_ref.dtype)


def _norm(x2d, g, tm=1024):
    m, k = x2d.shape
    return pl.pallas_call(
        _norm_kernel,
        grid=(m // tm,),
        in_specs=[pl.BlockSpec((tm, k), lambda i: (i, 0)), pl.BlockSpec((1, k), lambda i: (0, 0))],
        out_specs=pl.BlockSpec((tm, k), lambda i: (i, 0)),
        out_shape=jax.ShapeDtypeStruct((m, k), BF16),
        compiler_params=_params("parallel"),
        name="norm",
    )(x2d, g)


def _proj_kernel(h_ref, w_ref, o_ref):
    o_ref[...] = jnp.dot(h_ref[...], w_ref[...],
                         preferred_element_type=F32).astype(o_ref.dtype)


def _proj(h2d, w, tn, out_dtype, tm=1024):
    m, k = h2d.shape
    n = w.shape[1]
    return pl.pallas_call(
        _proj_kernel,
        grid=(m // tm, n // tn),
        in_specs=[pl.BlockSpec((tm, k), lambda i, j: (i, 0)),
                  pl.BlockSpec((k, tn), lambda i, j: (0, j))],
        out_specs=pl.BlockSpec((tm, tn), lambda i, j: (i, j)),
        out_shape=jax.ShapeDtypeStruct((m, n), out_dtype),
        compiler_params=_params("parallel", "arbitrary"),
        name="proj",
    )(h2d, w)


MXU_N = 256


def _proj_dil_kernel(h_ref, w_ref, c_ref, s_ref, o_ref):
    d, rows, k = h_ref.shape[1:]
    tm = d * rows
    tn = w_ref.shape[1]
    hp = h_ref[0].reshape(tm, k)

    @pl.when(pl.program_id(2) < 2)
    def _():
        c = c_ref[0].reshape(tm, LANES)
        sn = s_ref[0].reshape(tm, LANES)
        for nb in range(tn // MXU_N):
            acc = jnp.dot(hp, w_ref[:, nb * MXU_N:(nb + 1) * MXU_N], preferred_element_type=F32)
            for cb in range(MXU_N // LANES):
                blk = acc[:, cb * LANES:(cb + 1) * LANES]
                rot = blk * c + pltpu.roll(blk, LANES // 2, 1) * sn
                lo = nb * MXU_N + cb * LANES
                o_ref[0, :, :, lo:lo + LANES] = rot.astype(o_ref.dtype).reshape(d, rows, LANES)

    @pl.when(pl.program_id(2) >= 2)
    def _():
        acc = jnp.dot(hp, w_ref[...], preferred_element_type=F32)
        o_ref[0] = acc.astype(o_ref.dtype).reshape(d, rows, tn)


def _proj_dil(hp, w, tabs, tm=1024):
    b, d, l, k = hp.shape
    n = w.shape[1]
    tn = DIL_GROUP_WIDTH
    rows = tm // d
    c, sn = tabs
    tab_spec = pl.BlockSpec((1, d, rows, LANES), lambda bi, i, j: (0, 0, i, 0))
    return pl.pallas_call(
        _proj_dil_kernel,
        grid=(b, l // rows, n // tn),
        in_specs=[pl.BlockSpec((1, d, rows, k), lambda bi, i, j: (bi, 0, i, 0)),
                  pl.BlockSpec((k, tn), lambda bi, i, j: (0, j)),
                  tab_spec, tab_spec],
        out_specs=pl.BlockSpec((1, d, rows, tn), lambda bi, i, j: (bi, 0, i, j)),
        out_shape=jax.ShapeDtypeStruct((b, d, l, n), BF16),
        compiler_params=_params("parallel", "parallel", "arbitrary"),
        name=f"proj_dil{d}",
    )(hp, w, c, sn)


def _rotary_lane_sources():
    half = LANES // 2
    head = np.zeros(LANES, np.int64)
    dim = np.zeros(LANES, np.int64)
    for lane in range(LANES):
        off = lane % half
        is_b = lane >= half
        if off < 2 * ROPE_HALF:
            head[lane] = off // ROPE_HALF
            dim[lane] = off % ROPE_HALF + (ROPE_HALF if is_b else 0)
        else:
            head[lane] = 1 if is_b else 0
            dim[lane] = off
    return head, dim


def _rotary_tables(s, d):
    inv = ROPE_THETA ** (-jnp.arange(0, ROPE_DIM, 2, dtype=F32) / ROPE_DIM)
    ang = jnp.arange(s).astype(F32)[:, None] * inv[None, :]
    cos, sin = jnp.cos(ang), jnp.sin(ang)
    _, dim = _rotary_lane_sources()
    lanes = np.arange(LANES)
    rotary = (lanes % (LANES // 2)) < 2 * ROPE_HALF
    is_b = lanes >= LANES // 2
    idx = dim % ROPE_HALF
    c = jnp.where(rotary[None, :], cos[:, idx], 1.0)
    sn = jnp.where(rotary[None, :], jnp.where(is_b[None, :], sin[:, idx], -sin[:, idx]), 0.0)

    def arrange(t):
        return t.reshape(s // d, d, LANES).transpose(1, 0, 2)[None]

    return arrange(c), arrange(sn)


def _rotary_layout(t):
    lead = t.shape[:-1]
    t = t.reshape(lead + (t.shape[-1] // LANES, 2, HEAD_DIM))
    npair = t.shape[-3]
    a = t[..., :, :, 0:ROPE_HALF].reshape(lead + (npair, 2 * ROPE_HALF))
    b = t[..., :, :, ROPE_HALF:ROPE_DIM].reshape(lead + (npair, 2 * ROPE_HALF))
    rest0 = t[..., :, 0, ROPE_DIM:]
    rest1 = t[..., :, 1, ROPE_DIM:]
    return jnp.concatenate([a, rest0, b, rest1], axis=-1).reshape(lead + (npair * LANES,))


NA_UNROLL = 8


def _na_kernel(q_ref, k_ref, v_ref, g_ref, tb_ref, o_ref):
    rows = q_ref.shape[1] // GRID_W
    nkeys = NA_WIN_ROWS * GRID_W
    lane = lax.broadcasted_iota(jnp.int32, (GRID_W, LANES), 1)
    first = lane < HEAD_DIM

    def body(step, carry):
        units = []
        for i in range(NA_UNROLL):
            r = step * NA_UNROLL + i
            r0 = jnp.clip(r - NA_WIN_ROWS // 2, 0, rows - NA_WIN_ROWS)
            st = r0 - r + NA_WIN_ROWS - 1
            qs = pl.ds(pl.multiple_of(r * GRID_W, GRID_W), GRID_W)
            ks = pl.ds(pl.multiple_of(r0 * GRID_W, GRID_W), nkeys)
            q = q_ref[0, qs, :] * HEAD_SCALE
            kw = k_ref[0, ks, :]
            zero = jnp.zeros_like(q)
            q2 = jnp.concatenate([jnp.where(first, q, zero), jnp.where(first, zero, q)], axis=0)
            sc = lax.dot_general(q2, kw, (((1,), (1,)), ((), ())),
                                 preferred_element_type=F32)
            units.append((qs, ks, sc + tb_ref[0, st]))
        probs = []
        for qs, ks, sc in units:
            mx = jnp.max(sc, axis=-1, keepdims=True)
            e = jnp.exp(sc - mx)
            den = jnp.sum(e, axis=-1, keepdims=True)
            probs.append((e / den).astype(BF16))
        for (qs, ks, _), p in zip(units, probs):
            o2 = jnp.dot(p, v_ref[0, ks, :], preferred_element_type=F32)
            o = jnp.where(first, o2[0:GRID_W], o2[GRID_W:])
            o_ref[0, qs, :] = (o * _silu(g_ref[0, qs, :])).astype(o_ref.dtype)
        return carry

    lax.fori_loop(0, rows // NA_UNROLL, body, 0)


def _na_bias_table(rpb):
    h = rpb.shape[0]
    cq = np.arange(GRID_W)[:, None]
    ck = np.arange(GRID_W)[None, :]
    ws = np.clip(cq - NA_WIN_COLS // 2, 0, GRID_W - NA_WIN_COLS)
    in_win = (ck >= ws) & (ck < ws + NA_WIN_COLS)
    dcol = np.clip(ck - cq + NA_WIN_COLS - 1, 0, 2 * NA_WIN_COLS - 2)
    colb = jnp.where(in_win, rpb.astype(F32)[:, :, dcol], -jnp.inf)
    st = np.arange(NA_WIN_ROWS)[:, None] + np.arange(NA_WIN_ROWS)[None, :]
    tb = colb[:, st]
    tb = tb.reshape(h // 2, 2, NA_WIN_ROWS, NA_WIN_ROWS, GRID_W, GRID_W)
    tb = tb.transpose(0, 2, 1, 4, 3, 5).reshape(h // 2, NA_WIN_ROWS, 2 * GRID_W, NA_WIN_ROWS * GRID_W)
    return tb


def _na_attention(qkvg, tb):
    b, s, _ = qkvg.shape
    npair = NA_WIDTH // LANES

    def col(off):
        return pl.BlockSpec((1, s, LANES), lambda bi, hp: (bi, 0, off + hp))

    return pl.pallas_call(
        _na_kernel,
        grid=(b, npair),
        in_specs=[col(0), col(npair), col(2 * npair), col(3 * npair),
                  pl.BlockSpec((1,) + tb.shape[1:], lambda bi, hp: (hp, 0, 0, 0))],
        out_specs=pl.BlockSpec((1, s, LANES), lambda bi, hp: (bi, 0, hp)),
        out_shape=jax.ShapeDtypeStruct((b, s, NA_WIDTH), BF16),
        compiler_params=_params("parallel", "parallel"),
        name="na_attention",
    )(qkvg, qkvg, qkvg, qkvg, tb)


DIL_QB = 2 * DIL_BLK
DIL_KB = 4 * DIL_BLK
DIL_UNROLL = 4


def _dil_kernel(q0, k0, v0, q1, k1, v1, q2, k2, v2, g_ref, o_ref, acc_ref, m_ref, w_ref):
    s = o_ref.shape[1]
    lane = lax.broadcasted_iota(jnp.int32, (DIL_QB, LANES), 1)
    first = lane < HEAD_DIM
    half_off = lane % (LANES // 2)
    q_first = (half_off < ROPE_HALF) | ((half_off >= 2 * ROPE_HALF) & (lane < LANES // 2))
    qi = lax.broadcasted_iota(jnp.int32, (2 * DIL_QB, DIL_KB), 0) % DIL_QB
    ki = lax.broadcasted_iota(jnp.int32, (2 * DIL_QB, DIL_KB), 1)

    for g, (q_ref, k_ref, v_ref) in enumerate(((q0, k0, v0), (q1, k1, v1), (q2, k2, v2))):
        d = DIL_PAIRS[g][1]
        l = s // d
        nu = l // DIL_QB

        def trip(step, carry, g=g, d=d, l=l, nu=nu, q_ref=q_ref, k_ref=k_ref, v_ref=v_ref):
            units = []
            for i in range(DIL_UNROLL):
                idx = step * DIL_UNROLL + i
                rho = idx // nu
                u = idx % nu
                q_start = pl.multiple_of(u * DIL_QB, DIL_QB)
                k_start = pl.multiple_of(jnp.clip(u * DIL_QB - DIL_BLK, 0, l - DIL_KB), DIL_BLK)
                q = q_ref[0, rho, pl.ds(q_start, DIL_QB), :] * HEAD_SCALE
                kw = k_ref[0, rho, pl.ds(k_start, DIL_KB), :]
                valid2 = jnp.abs((ki + k_start) - (qi + q_start)) <= DIL_BLK
                zero = jnp.zeros_like(q)
                q2 = jnp.concatenate([jnp.where(q_first, q, zero), jnp.where(q_first, zero, q)], axis=0)
                sc = lax.dot_general(q2, kw, (((1,), (1,)), ((), ())),
                                     preferred_element_type=F32)
                units.append((rho, q_start, k_start, jnp.where(valid2, sc, -jnp.inf)))
            soft = []
            for rho, q_start, k_start, sc in units:
                mx = jnp.max(sc, axis=-1, keepdims=True)
                e = jnp.exp(sc - mx)
                den = jnp.sum(e, axis=-1, keepdims=True)
                soft.append((e.astype(BF16), mx, den))
            for (rho, q_start, k_start, _), (e, mx2, den2) in zip(units, soft):
                o2 = jnp.dot(e, v_ref[0, rho, pl.ds(k_start, DIL_KB), :],
                             preferred_element_type=F32)
                num = jnp.where(first, o2[0:DIL_QB], o2[DIL_QB:])
                mx = jnp.where(first, mx2[0:DIL_QB], mx2[DIL_QB:])
                den = jnp.where(first, den2[0:DIL_QB], den2[DIL_QB:])
                if d == 1:
                    rows = pl.ds(q_start, DIL_QB)
                else:
                    rows = pl.ds(rho + d * q_start, DIL_QB, stride=d)
                if g == 0:
                    acc_ref[rows, :] = num
                    m_ref[rows, :] = mx
                    w_ref[rows, :] = den
                else:
                    m_old = m_ref[rows, :]
                    m_new = jnp.maximum(m_old, mx)
                    a = jnp.exp(m_old - m_new)
                    bw = jnp.exp(mx - m_new)
                    acc_ref[rows, :] = acc_ref[rows, :] * a + num * bw
                    w_ref[rows, :] = w_ref[rows, :] * a + den * bw
                    m_ref[rows, :] = m_new
            return carry

        lax.fori_loop(0, d * nu // DIL_UNROLL, trip, 0)

    def finish(i, carry):
        rows = pl.ds(pl.multiple_of(i * 256, 256), 256)
        y = acc_ref[rows, :] / w_ref[rows, :]
        o_ref[0, rows, :] = (y * _silu(g_ref[0, rows, :])).astype(o_ref.dtype)
        return carry

    lax.fori_loop(0, s // 256, finish, 0)


def _dil_attention(qkvs, gate, gate_off):
    b = gate.shape[0]
    s = gate.shape[1]
    npair = DIL_GROUP_WIDTH // LANES
    in_specs = []
    args = []
    for arr in qkvs:
        d, l = arr.shape[1], arr.shape[2]
        for part in range(3):
            in_specs.append(pl.BlockSpec((1, d, l, LANES),
                                         lambda bi, sp, part=part: (bi, 0, 0, part * npair + sp)))
            args.append(arr)
    in_specs.append(pl.BlockSpec((1, s, LANES), lambda bi, sp: (bi, 0, gate_off + sp)))
    args.append(gate)
    return pl.pallas_call(
        _dil_kernel,
        grid=(b, npair),
        in_specs=in_specs,
        out_specs=pl.BlockSpec((1, s, LANES), lambda bi, sp: (bi, 0, sp)),
        out_shape=jax.ShapeDtypeStruct((b, s, DIL_GROUP_WIDTH), BF16),
        scratch_shapes=[pltpu.VMEM((s, LANES), F32)] * 3,
        compiler_params=_params("parallel", "parallel"),
        name="dil_attention",
    )(*args)


def _split3(x):
    hi = x.astype(BF16)
    r1 = x - hi.astype(F32)
    mid = r1.astype(BF16)
    lo = (r1 - mid.astype(F32)).astype(BF16)
    return hi, mid, lo


SSM_HALO = 16
SSM_DT_COPIES = 2
SSM_UNROLL = 2


def _ssd_kernel(xs_ref, bm_ref, cm_ref, dt_ref, wx_ref, wb_ref, wc_ref, bx_ref, bb_ref, bc_ref,
                dtb_ref, a_ref, dsk_ref, y_ref, xs_s, bt_s, c_s, cb_s, acum_s, pt_s, st_s):
    s = xs_ref.shape[1]
    cl = SSM_CHUNK
    nchunk = s // cl
    hpg = SSM_HEADS_PER_GROUP
    npair = SSM_GROUP_WIDTH // LANES
    ncol = 2 * hpg

    def conv_chunk(c, carry):
        base = pl.multiple_of(c * cl, cl)
        prev_start = pl.multiple_of(jnp.maximum(base - SSM_HALO, 0), SSM_HALO)
        next_start = pl.multiple_of(jnp.minimum(base + cl, s - SSM_HALO), SSM_HALO)
        has_prev = jnp.where(c > 0, 1.0, 0.0)
        has_next = jnp.where(c < nchunk - 1, 1.0, 0.0)

        def conv(src, w_ref, bias_ref):
            ext = jnp.concatenate([src[0, pl.ds(prev_start, SSM_HALO), :].astype(F32) * has_prev,
                                   src[0, pl.ds(base, cl), :].astype(F32),
                                   src[0, pl.ds(next_start, SSM_HALO), :].astype(F32) * has_next], axis=0)
            acc = bias_ref[0]
            off = SSM_HALO - SSM_CONV // 2
            for tap in range(SSM_CONV):
                acc = acc + ext[off + tap:off + tap + cl, :] * w_ref[0, tap:tap + 1, :]
            return _silu(acc)

        xs_s[pl.ds(base, cl), :] = conv(xs_ref, wx_ref, bx_ref)
        bmat = conv(bm_ref, wb_ref, bb_ref).astype(BF16)
        cmat = conv(cm_ref, wc_ref, bc_ref).astype(BF16)
        bt_s[c] = bmat.astype(F32).T.astype(BF16)
        c_s[pl.ds(base, cl), :] = cmat
        cb_s[c] = lax.dot_general(cmat, bmat, (((1,), (1,)), ((), ())),
                                  preferred_element_type=F32)

        dt = jax.nn.softplus(dt_ref[0, pl.ds(base, cl), :] + dt_bias)
        pieces = _split3(dt * a_neg)
        acum_f = sum(jnp.dot(tril, p, preferred_element_type=F32) for p in pieces)
        acum_b = sum(jnp.dot(triu, p, preferred_element_type=F32) for p in pieces)
        acum = jnp.where(fwd_lane, acum_f, acum_b)
        a_end = jnp.where(fwd_lane[0:1], acum[cl - 1:cl, :], acum[0:1, :])
        w_state = dt * jnp.exp(a_end - acum)
        packed_t = jnp.where(lane < ncol, acum - jnp.log(dt), w_state).T
        acum_s[c] = acum
        pt_s[c] = packed_t[0:SSM_DT_COPIES * ncol, :]
        return carry

    ri = lax.broadcasted_iota(jnp.int32, (cl, cl), 0)
    ci = lax.broadcasted_iota(jnp.int32, (cl, cl), 1)
    lane = lax.broadcasted_iota(jnp.int32, (cl, LANES), 1)
    first = lane < HEAD_DIM
    fwd_lane = (lane % ncol) < hpg
    tril = jnp.where(ci <= ri, 1.0, 0.0).astype(BF16)
    triu = jnp.where(ci >= ri, 1.0, 0.0).astype(BF16)
    dt_bias = dtb_ref[0]
    a_neg = a_ref[0]

    lax.fori_loop(0, nchunk, conv_chunk, 0)

    for direction in range(2):
        if direction == 0:
            keep = ci <= ri
            end_row = cl - 1
        else:
            keep = ci >= ri
            end_row = 0

        st_s[...] = jnp.zeros_like(st_s)

        def local_part(c, direction=direction, keep=keep, end_row=end_row):
            rows = pl.ds(pl.multiple_of(c * cl, cl), cl)
            acum = acum_s[c]
            packed_t = pt_s[c]

            xs = xs_s[rows, :]
            bt = bt_s[c].astype(F32)
            cb = cb_s[c]
            yd_blocks, new_blocks, scale_blocks = [], [], []
            for pr in range(npair):
                xblk = xs[:, pr * LANES:(pr + 1) * LANES].astype(BF16)
                lhs, ea = [], []
                for j in (direction * hpg + 2 * pr, direction * hpg + 2 * pr + 1):
                    col = jnp.broadcast_to(acum[:, j:j + 1], (cl, cl))
                    row = packed_t[j:j + 1, :]
                    ws_row = packed_t[ncol + j:ncol + j + 1, :]
                    seg_dt = jnp.exp(jnp.where(keep, col - row, -jnp.inf))
                    lhs.append((cb * seg_dt).astype(BF16))
                    lhs.append((bt * ws_row).astype(BF16))
                    ea.append(jnp.exp(col))
                prod = jnp.dot(jnp.concatenate(lhs, axis=0), xblk, preferred_element_type=F32)
                yd_blocks.append(jnp.where(first, prod[0:cl], prod[2 * cl:3 * cl]))
                new_blocks.append(jnp.where(first, prod[cl:2 * cl], prod[3 * cl:4 * cl]))
                scale_blocks.append(jnp.where(first, ea[0], ea[1]))
            return rows, xs, yd_blocks, new_blocks, scale_blocks

        def trip(step, carry, direction=direction, end_row=end_row):
            chunks = [step * SSM_UNROLL + i for i in range(SSM_UNROLL)]
            if direction == 1:
                chunks = [nchunk - 1 - c for c in chunks]
            parts = [local_part(c) for c in chunks]
            state = [st_s[:, pr * LANES:(pr + 1) * LANES] for pr in range(npair)]
            for rows, xs, yd_blocks, new_blocks, scale_blocks in parts:
                cmat = c_s[rows, :]
                y_blocks = []
                for pr in range(npair):
                    y_off = jnp.dot(cmat, state[pr].astype(BF16), preferred_element_type=F32)
                    y_blocks.append(yd_blocks[pr] + y_off * scale_blocks[pr])
                    state[pr] = (state[pr] * scale_blocks[pr][end_row:end_row + 1, :]
                                 + new_blocks[pr])
                y = jnp.concatenate(y_blocks, axis=1)
                if direction == 0:
                    y_ref[0, rows, :] = y + dsk_ref[0] * xs
                else:
                    y_ref[0, rows, :] = y_ref[0, rows, :] + y
            for pr in range(npair):
                st_s[:, pr * LANES:(pr + 1) * LANES] = state[pr]
            return carry

        lax.fori_loop(0, nchunk // SSM_UNROLL, trip, 0)


def _ssd(xbc, dtp, conv_w, conv_b, dt_bias, a_neg, d_skip):
    b, s, _ = xbc.shape
    gw = SSM_GROUP_WIDTH
    b_off = SSM_INNER // LANES
    c_off = b_off + SSM_GROUPS
    nchunk = s // SSM_CHUNK

    def seq(width, off):
        return pl.BlockSpec((1, s, width), lambda bi, g: (bi, 0, off + g))

    def par(rows, width, off):
        return pl.BlockSpec((1, rows, width), lambda bi, g: (0, 0, off + g))

    return pl.pallas_call(
        _ssd_kernel,
        grid=(b, SSM_GROUPS),
        in_specs=[seq(gw, 0), seq(LANES, b_off), seq(LANES, c_off), seq(LANES, 0),
                  par(SSM_CONV, gw, 0), par(SSM_CONV, LANES, b_off), par(SSM_CONV, LANES, c_off),
                  par(1, gw, 0), par(1, LANES, b_off), par(1, LANES, c_off),
                  pl.BlockSpec((1, 1, LANES), lambda bi, g: (g, 0, 0)),
                  pl.BlockSpec((1, 1, LANES), lambda bi, g: (g, 0, 0)),
                  pl.BlockSpec((1, 1, gw), lambda bi, g: (g, 0, 0))],
        out_specs=pl.BlockSpec((1, s, gw), lambda bi, g: (bi, 0, g)),
        out_shape=jax.ShapeDtypeStruct((b, s, SSM_INNER), F32),
        scratch_shapes=[pltpu.VMEM((s, gw), F32),
                        pltpu.VMEM((nchunk, SSM_STATE, SSM_CHUNK), BF16),
                        pltpu.VMEM((s, LANES), BF16),
                        pltpu.VMEM((nchunk, SSM_CHUNK, SSM_CHUNK), F32),
                        pltpu.VMEM((nchunk, SSM_CHUNK, LANES), F32),
                        pltpu.VMEM((nchunk, SSM_DT_COPIES * 2 * SSM_HEADS_PER_GROUP, SSM_CHUNK), F32),
                        pltpu.VMEM((SSM_STATE, gw), F32)],
        compiler_params=_params("parallel", "parallel"),
        name="ssd",
    )(xbc, xbc, xbc, dtp, conv_w, conv_w, conv_w, conv_b, conv_b, conv_b, dt_bias, a_neg, d_skip)


def _tail_kernel(x_ref, ya_ref, yb_ref, yc_ref, z_ref, ua_ref, ub_ref, uc_ref, p_ref,
                 nw_ref, woa_ref, wob_ref, woc_ref, wout_ref, pg_ref, wpg_ref, wple_ref, gn_ref,
                 o_ref, *h_ref, final):
    def mm(a, w_ref):
        return jnp.dot(a, w_ref[...], preferred_element_type=F32)

    ya = mm(ya_ref[...], woa_ref)
    yb = mm(yb_ref[...], wob_ref)
    yc_in = _rms(yc_ref[...] * _silu(z_ref[...]), nw_ref[...]).astype(BF16)
    yc = mm(yc_in, woc_ref)
    merged = (_sigmoid(ua_ref[...]) * ya + _sigmoid(ub_ref[...]) * yb
              + _sigmoid(uc_ref[...]) * yc)
    x1 = x_ref[...] + mm(merged.astype(BF16), wout_ref)
    gate = jax.nn.sigmoid(mm(_rms(x1, pg_ref[...]).astype(BF16), wpg_ref))
    x2 = x1 + mm(p_ref[...].astype(BF16), wple_ref) * gate
    if final:
        o_ref[...] = _rms(x2, gn_ref[...])
    else:
        o_ref[...] = x2
        h_ref[0][...] = _rms(x2, gn_ref[...]).astype(BF16)


MISC_WIDTH = SSM_INNER + DIL_GROUP_WIDTH + 3 * D_MODEL
MISC_GB_LANE_BLOCK = SSM_INNER // LANES
MISC_U_BLOCK = (SSM_INNER + DIL_GROUP_WIDTH) // D_MODEL


def _tail(x2d, ya, yb, yc, misc, p2d, nw, woa, wob, woc, wout, pg, wpg, wple, gn, final, tm=512):
    m = x2d.shape[0]
    out_specs = pl.BlockSpec((tm, D_MODEL), lambda i: (i, 0))
    out_shape = jax.ShapeDtypeStruct((m, D_MODEL), F32)
    if not final:
        out_specs = [out_specs, pl.BlockSpec((tm, D_MODEL), lambda i: (i, 0))]
        out_shape = [out_shape, jax.ShapeDtypeStruct((m, D_MODEL), BF16)]

    def rows(width, off=0):
        return pl.BlockSpec((tm, width), lambda i: (i, off))

    def whole(arr):
        return pl.BlockSpec(arr.shape, lambda i: (0, 0), pipeline_mode=pl.Buffered(1))

    ub = MISC_U_BLOCK
    return pl.pallas_call(
        functools.partial(_tail_kernel, final=final),
        grid=(m // tm,),
        in_specs=[rows(D_MODEL), rows(NA_WIDTH), rows(DIL_GROUP_WIDTH), rows(SSM_INNER),
                  rows(SSM_INNER, 0), rows(D_MODEL, ub), rows(D_MODEL, ub + 1), rows(D_MODEL, ub + 2),
                  rows(PLE_DIM),
                  whole(nw), whole(woa), whole(wob), whole(woc), whole(wout), whole(pg), whole(wpg),
                  whole(wple), whole(gn)],
        out_specs=out_specs,
        out_shape=out_shape,
        compiler_params=_params("parallel"),
        name="tail",
    )(x2d, ya, yb, yc, misc, misc, misc, misc, p2d, nw, woa, wob, woc, wout, pg, wpg, wple, gn)


def _prep_weights(w_in, conv_w, conv_b, a_log, dt_bias, d_skip):
    depth = w_in.shape[0]
    offs = np.concatenate([[0], np.cumsum(IN_SPLITS)])
    (qa, ka, va, ga, qb, kb, vb, gb, xbc, z, dtr, ua, ub, uc) = [
        w_in[:, :, int(offs[i]):int(offs[i + 1])] for i in range(len(IN_SPLITS))]
    w_a = jnp.concatenate([qa, ka, va, ga], axis=2).astype(BF16)
    gw = DIL_GROUP_WIDTH
    w_b = [jnp.concatenate([_rotary_layout(qb[:, :, g * gw:(g + 1) * gw]),
                            _rotary_layout(kb[:, :, g * gw:(g + 1) * gw]),
                            vb[:, :, g * gw:(g + 1) * gw]], axis=2).astype(BF16)
           for g in range(len(DIL_PAIRS))]
    w_misc = jnp.concatenate([z, gb, ua, ub, uc], axis=2).astype(BF16)

    hpg = SSM_HEADS_PER_GROUP

    def per_group(t):
        lead = t.shape[:-2]
        t = t.reshape(lead + (2, SSM_GROUPS, hpg))
        t = jnp.moveaxis(t, -2, -3).reshape(lead + (SSM_GROUPS, 2 * hpg))
        t = jnp.tile(t, (1,) * (t.ndim - 1) + (SSM_DT_COPIES,))
        pad = [(0, 0)] * (t.ndim - 1) + [(0, LANES - 2 * hpg * SSM_DT_COPIES)]
        return jnp.pad(t, pad).reshape(lead + (SSM_GROUPS * LANES,))

    w_dt = per_group(dtr.reshape(depth, D_MODEL, 2, SSM_HEADS)).astype(BF16)
    w_xbc = xbc.astype(BF16)
    dtb = per_group(dt_bias.astype(F32)).reshape(depth, SSM_GROUPS, 1, LANES)
    a_neg = per_group(-jnp.exp(a_log.astype(F32))).reshape(depth, SSM_GROUPS, 1, LANES)
    dsk = jnp.repeat(d_skip.astype(F32), HEAD_DIM, axis=1).reshape(depth, SSM_GROUPS, 1, SSM_GROUP_WIDTH)
    cw = conv_w.astype(F32).reshape(depth, 1, SSM_CONV, SSM_CONV_DIM)
    cbias = conv_b.astype(F32).reshape(depth, 1, 1, SSM_CONV_DIM)
    return w_a, w_b, w_misc, w_xbc, w_dt, dtb, a_neg, dsk, cw, cbias


def kernel(x, p, norm_w, w_in, na_rpb, conv_w, conv_b, a_log, dt_bias, d_skip, ssm_norm_w,
           w_oa, w_ob, w_oc, w_out, ple_norm_w, w_ple, w_ple_gate, final_norm_w):
    b, s, dm = x.shape
    depth = w_in.shape[0]
    m = b * s
    w_a, w_b, w_misc, w_xbc, w_dt, dtb, a_neg, dsk, cw, cbias = _prep_weights(
        w_in, conv_w, conv_b, a_log, dt_bias, d_skip)
    tabs = [_rotary_tables(s, d) for _, d in DIL_PAIRS]
    row = lambda v: v.astype(F32).reshape(1, -1)

    x2d = x.reshape(m, dm)
    h = _norm(x2d, row(norm_w[0]))
    for i in range(depth):
        final = i == depth - 1
        qkvg = _proj(h, w_a[i], 1024, BF16).reshape(b, s, 4 * NA_WIDTH)
        misc = _proj(h, w_misc[i], 1024, BF16)
        xbc = _proj(h, w_xbc[i], SSM_CONV_DIM // 2, BF16).reshape(b, s, SSM_CONV_DIM)
        dtp = _proj(h, w_dt[i], SSM_GROUPS * LANES, F32).reshape(b, s, SSM_GROUPS * LANES)
        h3 = h.reshape(b, s, dm)
        qkvs = [_proj_dil(jnp.stack([h3[:, rho::d, :] for rho in range(d)], axis=1), w_b[gi][i], tabs[gi])
                for gi, (_, d) in enumerate(DIL_PAIRS)]

        ya = _na_attention(qkvg, _na_bias_table(na_rpb[i])).reshape(m, NA_WIDTH)
        yb = _dil_attention(qkvs, misc.reshape(b, s, MISC_WIDTH), MISC_GB_LANE_BLOCK)
        yb = yb.reshape(m, DIL_GROUP_WIDTH)
        yc = _ssd(xbc, dtp, cw[i], cbias[i], dtb[i], a_neg[i], dsk[i]).reshape(m, SSM_INNER)

        gn = row(final_norm_w) if final else row(norm_w[i + 1])
        out = _tail(x2d, ya, yb, yc, misc, p[i].reshape(m, PLE_DIM), row(ssm_norm_w[i]),
                    w_oa[i].astype(BF16), w_ob[i].astype(BF16), w_oc[i].astype(BF16),
                    w_out[i].astype(BF16), row(ple_norm_w[i]), w_ple_gate[i].astype(BF16),
                    w_ple[i].astype(BF16), gn, final=final)
        if final:
            x2d = out
        else:
            x2d, h = out
    return x2d.reshape(b, s, dm)
```

```python
import functools
import math

import numpy as np
import jax
import jax.numpy as jnp
from jax import lax
from jax.experimental import pallas as pl
from jax.experimental.pallas import tpu as pltpu

F32 = jnp.float32
BF16 = jnp.bfloat16

LANES = 128
VMEM_LIMIT_BYTES = 56 * 1024 * 1024

D_MODEL = 1024
GRID_W = 64
HEAD_DIM = 64
EPS = 1e-6
PLE_DIM = 256

NA_HEADS = 16
NA_WIDTH = NA_HEADS * HEAD_DIM
NA_WIN_ROWS = 8
NA_WIN_COLS = 16

DIL_PAIRS = ((128, 1), (512, 4), (2048, 16))
DIL_HEADS_PER_GROUP = 8
DIL_GROUP_WIDTH = DIL_HEADS_PER_GROUP * HEAD_DIM
DIL_WIDTH = DIL_GROUP_WIDTH * len(DIL_PAIRS)
DIL_BLK = 64
ROPE_THETA = 500000.0
ROPE_DIM = HEAD_DIM // 4
ROPE_HALF = ROPE_DIM // 2

SSM_INNER = 1536
SSM_HEADS = 24
SSM_GROUPS = 4
SSM_HEADS_PER_GROUP = SSM_HEADS // SSM_GROUPS
SSM_GROUP_WIDTH = SSM_HEADS_PER_GROUP * HEAD_DIM
SSM_STATE = 128
SSM_CONV = 5
SSM_CHUNK = 128
SSM_CONV_DIM = SSM_INNER + 2 * SSM_GROUPS * SSM_STATE

IN_SPLITS = (NA_WIDTH, NA_WIDTH, NA_WIDTH, NA_WIDTH,
             DIL_WIDTH, DIL_WIDTH, DIL_WIDTH, DIL_GROUP_WIDTH,
             SSM_CONV_DIM, SSM_INNER, 2 * SSM_HEADS,
             D_MODEL, D_MODEL, D_MODEL)

LOG2E = math.log2(math.e)
SCORE_SCALE = HEAD_DIM ** -0.5 * LOG2E


def _params(*semantics):
    return pltpu.CompilerParams(dimension_semantics=semantics,
                                vmem_limit_bytes=VMEM_LIMIT_BYTES)


def _rms(x, g):
    return x * lax.rsqrt(jnp.mean(x * x, axis=-1, keepdims=True) + EPS) * g


def _silu(x):
    x = x.astype(F32)
    return x * jax.nn.sigmoid(x)


def _sigmoid(x):
    return jax.nn.sigmoid(x.astype(F32))


def _norm_kernel(x_ref, g_ref, h_ref):
    h_ref[...] = _rms(x_ref[...], g_ref[...]).astype(h_ref.dtype)


def _norm(x2d, g, tm=1024):
    m, k = x2d.shape
    return pl.pallas_call(
        _norm_kernel,
        grid=(m // tm,),
        in_specs=[pl.BlockSpec((tm, k), lambda i: (i, 0)), pl.BlockSpec((1, k), lambda i: (0, 0))],
        out_specs=pl.BlockSpec((tm, k), lambda i: (i, 0)),
        out_shape=jax.ShapeDtypeStruct((m, k), BF16),
        compiler_params=_params("parallel"),
        name="norm",
    )(x2d, g)


def _proj_kernel(h_ref, w_ref, o_ref):
    o_ref[...] = jnp.dot(h_ref[...], w_ref[...],
                         preferred_element_type=F32).astype(o_ref.dtype)


def _proj(h2d, w, tn, out_dtype, tm=1024):
    m, k = h2d.shape
    n = w.shape[1]
    return pl.pallas_call(
        _proj_kernel,
        grid=(m // tm, n // tn),
        in_specs=[pl.BlockSpec((tm, k), lambda i, j: (i, 0)),
                  pl.BlockSpec((k, tn), lambda i, j: (0, j))],
        out_specs=pl.BlockSpec((tm, tn), lambda i, j: (i, j)),
        out_shape=jax.ShapeDtypeStruct((m, n), out_dtype),
        compiler_params=_params("parallel", "arbitrary"),
        name="proj",
    )(h2d, w)


MXU_N = 256


def _proj_dil_kernel(h_ref, w_ref, c_ref, s_ref, o_ref):
    d, rows, k = h_ref.shape[1:]
    tm = d * rows
    tn = w_ref.shape[1]
    hp = h_ref[0].reshape(tm, k)

    @pl.when(pl.program_id(2) < 2)
    def _():
        c = c_ref[0].reshape(tm, LANES)
        sn = s_ref[0].reshape(tm, LANES)
        for nb in range(tn // MXU_N):
            acc = jnp.dot(hp, w_ref[:, nb * MXU_N:(nb + 1) * MXU_N], preferred_element_type=F32)
            for cb in range(MXU_N // LANES):
                blk = acc[:, cb * LANES:(cb + 1) * LANES]
                rot = blk * c + pltpu.roll(blk, LANES // 2, 1) * sn
                lo = nb * MXU_N + cb * LANES
                o_ref[0, :, :, lo:lo + LANES] = rot.astype(o_ref.dtype).reshape(d, rows, LANES)

    @pl.when(pl.program_id(2) >= 2)
    def _():
        acc = jnp.dot(hp, w_ref[...], preferred_element_type=F32)
        o_ref[0] = acc.astype(o_ref.dtype).reshape(d, rows, tn)


def _proj_dil(hp, w, tabs, tm=1024):
    b, d, l, k = hp.shape
    n = w.shape[1]
    tn = DIL_GROUP_WIDTH
    rows = tm // d
    c, sn = tabs
    tab_spec = pl.BlockSpec((1, d, rows, LANES), lambda bi, i, j: (0, 0, i, 0))
    return pl.pallas_call(
        _proj_dil_kernel,
        grid=(b, l // rows, n // tn),
        in_specs=[pl.BlockSpec((1, d, rows, k), lambda bi, i, j: (bi, 0, i, 0)),
                  pl.BlockSpec((k, tn), lambda bi, i, j: (0, j)),
                  tab_spec, tab_spec],
        out_specs=pl.BlockSpec((1, d, rows, tn), lambda bi, i, j: (bi, 0, i, j)),
        out_shape=jax.ShapeDtypeStruct((b, d, l, n), BF16),
        compiler_params=_params("parallel", "parallel", "arbitrary"),
        name=f"proj_dil{d}",
    )(hp, w, c, sn)


def _rotary_lane_sources():
    half = LANES // 2
    head = np.zeros(LANES, np.int64)
    dim = np.zeros(LANES, np.int64)
    for lane in range(LANES):
        off = lane % half
        is_b = lane >= half
        if off < 2 * ROPE_HALF:
            head[lane] = off // ROPE_HALF
            dim[lane] = off % ROPE_HALF + (ROPE_HALF if is_b else 0)
        else:
            head[lane] = 1 if is_b else 0
            dim[lane] = off
    return head, dim


def _rotary_tables(s, d):
    inv = ROPE_THETA ** (-jnp.arange(0, ROPE_DIM, 2, dtype=F32) / ROPE_DIM)
    ang = jnp.arange(s).astype(F32)[:, None] * inv[None, :]
    cos, sin = jnp.cos(ang), jnp.sin(ang)
    _, dim = _rotary_lane_sources()
    lanes = np.arange(LANES)
    rotary = (lanes % (LANES // 2)) < 2 * ROPE_HALF
    is_b = lanes >= LANES // 2
    idx = dim % ROPE_HALF
    c = jnp.where(rotary[None, :], cos[:, idx], 1.0)
    sn = jnp.where(rotary[None, :], jnp.where(is_b[None, :], sin[:, idx], -sin[:, idx]), 0.0)

    def arrange(t):
        return t.reshape(s // d, d, LANES).transpose(1, 0, 2)[None]

    return arrange(c), arrange(sn)


def _rotary_layout(t):
    lead = t.shape[:-1]
    t = t.reshape(lead + (t.shape[-1] // LANES, 2, HEAD_DIM))
    npair = t.shape[-3]
    a = t[..., :, :, 0:ROPE_HALF].reshape(lead + (npair, 2 * ROPE_HALF))
    b = t[..., :, :, ROPE_HALF:ROPE_DIM].reshape(lead + (npair, 2 * ROPE_HALF))
    rest0 = t[..., :, 0, ROPE_DIM:]
    rest1 = t[..., :, 1, ROPE_DIM:]
    return jnp.concatenate([a, rest0, b, rest1], axis=-1).reshape(lead + (npair * LANES,))


NA_UNROLL = 8


def _na_kernel(q_ref, k_ref, v_ref, g_ref, tb_ref, o_ref):
    rows = q_ref.shape[1] // GRID_W
    nkeys = NA_WIN_ROWS * GRID_W
    lane = lax.broadcasted_iota(jnp.int32, (GRID_W, LANES), 1)
    first = lane < HEAD_DIM

    def body(step, carry):
        units = []
        for i in range(NA_UNROLL):
            r = step * NA_UNROLL + i
            r0 = jnp.clip(r - NA_WIN_ROWS // 2, 0, rows - NA_WIN_ROWS)
            st = r0 - r + NA_WIN_ROWS - 1
            qs = pl.ds(pl.multiple_of(r * GRID_W, GRID_W), GRID_W)
            ks = pl.ds(pl.multiple_of(r0 * GRID_W, GRID_W), nkeys)
            q = q_ref[0, qs, :]
            kw = k_ref[0, ks, :]
            zero = jnp.zeros_like(q)
            q2 = jnp.concatenate([jnp.where(first, q, zero), jnp.where(first, zero, q)], axis=0)
            sc = lax.dot_general(q2, kw, (((1,), (1,)), ((), ())),
                                 preferred_element_type=F32)
            units.append((qs, ks, sc + tb_ref[0, st]))
        probs = []
        for qs, ks, sc in units:
            mx = jnp.max(sc, axis=-1, keepdims=True)
            e = jnp.exp2(sc - mx)
            probs.append((e.astype(BF16), jnp.sum(e, axis=-1, keepdims=True)))
        for (qs, ks, _), (p, den) in zip(units, probs):
            o2 = jnp.dot(p, v_ref[0, ks, :], preferred_element_type=F32) / den
            o = jnp.where(first, o2[0:GRID_W], o2[GRID_W:])
            o_ref[0, qs, :] = (o * _silu(g_ref[0, qs, :])).astype(o_ref.dtype)
        return carry

    lax.fori_loop(0, rows // NA_UNROLL, body, 0)


def _na_bias_table(rpb):
    h = rpb.shape[0]
    cq = np.arange(GRID_W)[:, None]
    ck = np.arange(GRID_W)[None, :]
    ws = np.clip(cq - NA_WIN_COLS // 2, 0, GRID_W - NA_WIN_COLS)
    in_win = (ck >= ws) & (ck < ws + NA_WIN_COLS)
    dcol = np.clip(ck - cq + NA_WIN_COLS - 1, 0, 2 * NA_WIN_COLS - 2)
    colb = jnp.where(in_win, rpb.astype(F32)[:, :, dcol] * LOG2E, -jnp.inf)
    st = np.arange(NA_WIN_ROWS)[:, None] + np.arange(NA_WIN_ROWS)[None, :]
    tb = colb[:, st]
    tb = tb.reshape(h // 2, 2, NA_WIN_ROWS, NA_WIN_ROWS, GRID_W, GRID_W)
    tb = tb.transpose(0, 2, 1, 4, 3, 5).reshape(h // 2, NA_WIN_ROWS, 2 * GRID_W, NA_WIN_ROWS * GRID_W)
    return tb


def _na_attention(qkvg, tb):
    b, s, _ = qkvg.shape
    npair = NA_WIDTH // LANES

    def col(off):
        return pl.BlockSpec((1, s, LANES), lambda bi, hp: (bi, 0, off + hp))

    return pl.pallas_call(
        _na_kernel,
        grid=(b, npair),
        in_specs=[col(0), col(npair), col(2 * npair), col(3 * npair),
                  pl.BlockSpec((1,) + tb.shape[1:], lambda bi, hp: (hp, 0, 0, 0))],
        out_specs=pl.BlockSpec((1, s, LANES), lambda bi, hp: (bi, 0, hp)),
        out_shape=jax.ShapeDtypeStruct((b, s, NA_WIDTH), BF16),
        compiler_params=_params("parallel", "parallel"),
        name="na_attention",
    )(qkvg, qkvg, qkvg, qkvg, tb)


DIL_QB = 2 * DIL_BLK
DIL_KB = 4 * DIL_BLK
DIL_UNROLL = 4


def _dil_kernel(q0, k0, v0, q1, k1, v1, q2, k2, v2, g_ref, o_ref, acc_ref, m_ref, w_ref):
    s = o_ref.shape[1]
    lane = lax.broadcasted_iota(jnp.int32, (DIL_QB, LANES), 1)
    first = lane < HEAD_DIM
    half_off = lane % (LANES // 2)
    q_first = (half_off < ROPE_HALF) | ((half_off >= 2 * ROPE_HALF) & (lane < LANES // 2))
    v_first = lax.broadcasted_iota(jnp.int32, (DIL_KB, LANES), 1) < HEAD_DIM
    qi = lax.broadcasted_iota(jnp.int32, (DIL_QB, DIL_KB), 0)
    ki = lax.broadcasted_iota(jnp.int32, (DIL_QB, DIL_KB), 1)

    for g, (q_ref, k_ref, v_ref) in enumerate(((q0, k0, v0), (q1, k1, v1), (q2, k2, v2))):
        d = DIL_PAIRS[g][1]
        l = s // d
        nu = l // DIL_QB

        def trip(step, carry, g=g, d=d, l=l, nu=nu, q_ref=q_ref, k_ref=k_ref, v_ref=v_ref):
            units = []
            for i in range(DIL_UNROLL):
                idx = step * DIL_UNROLL + i
                rho = idx // nu
                u = idx % nu
                q_start = pl.multiple_of(u * DIL_QB, DIL_QB)
                k_start = pl.multiple_of(jnp.clip(u * DIL_QB - DIL_BLK, 0, l - DIL_KB), DIL_BLK)
                q = q_ref[0, rho, pl.ds(q_start, DIL_QB), :]
                kw = k_ref[0, rho, pl.ds(k_start, DIL_KB), :]
                valid = jnp.abs((ki + k_start) - (qi + q_start)) <= DIL_BLK
                zero = jnp.zeros_like(q)
                scs = []
                for h in range(2):
                    qh = jnp.where(q_first, q, zero) if h == 0 else jnp.where(q_first, zero, q)
                    sc = lax.dot_general(qh, kw, (((1,), (1,)), ((), ())),
                                         preferred_element_type=F32)
                    scs.append(jnp.where(valid, sc, -jnp.inf))
                units.append((rho, q_start, k_start, scs))
            soft = []
            for rho, q_start, k_start, scs in units:
                es, mxs = [], []
                for sc in scs:
                    mx = jnp.max(sc, axis=-1, keepdims=True)
                    es.append(jnp.exp2(sc - mx).astype(BF16))
                    mxs.append(mx)
                soft.append((es, jnp.where(first, mxs[0], mxs[1])))
            for (rho, q_start, k_start, _), (es, mx) in zip(units, soft):
                vw = v_ref[0, rho, pl.ds(k_start, DIL_KB), :]
                one = jnp.ones_like(vw)
                o0 = jnp.dot(es[0], jnp.where(v_first, vw, one), preferred_element_type=F32)
                o1 = jnp.dot(es[1], jnp.where(v_first, one, vw), preferred_element_type=F32)
                num = jnp.where(first, o0, o1)
                den = pltpu.roll(jnp.where(first, o1, o0), HEAD_DIM, 1)
                if d == 1:
                    rows = pl.ds(q_start, DIL_QB)
                else:
                    rows = pl.ds(rho + d * q_start, DIL_QB, stride=d)
                if g == 0:
                    acc_ref[rows, :] = num
                    m_ref[rows, :] = mx
                    w_ref[rows, :] = den
                else:
                    m_old = m_ref[rows, :]
                    m_new = jnp.maximum(m_old, mx)
                    a = jnp.exp2(m_old - m_new)
                    bw = jnp.exp2(mx - m_new)
                    acc_ref[rows, :] = acc_ref[rows, :] * a + num * bw
                    w_ref[rows, :] = w_ref[rows, :] * a + den * bw
                    m_ref[rows, :] = m_new
            return carry

        lax.fori_loop(0, d * nu // DIL_UNROLL, trip, 0)

    def finish(i, carry):
        rows = pl.ds(pl.multiple_of(i * 256, 256), 256)
        y = acc_ref[rows, :] / w_ref[rows, :]
        o_ref[0, rows, :] = (y * _silu(g_ref[0, rows, :])).astype(o_ref.dtype)
        return carry

    lax.fori_loop(0, s // 256, finish, 0)


def _dil_attention(qkvs, gate, gate_off):
    b = gate.shape[0]
    s = gate.shape[1]
    npair = DIL_GROUP_WIDTH // LANES
    in_specs = []
    args = []
    for arr in qkvs:
        d, l = arr.shape[1], arr.shape[2]
        for part in range(3):
            in_specs.append(pl.BlockSpec((1, d, l, LANES),
                                         lambda bi, sp, part=part: (bi, 0, 0, part * npair + sp)))
            args.append(arr)
    in_specs.append(pl.BlockSpec((1, s, LANES), lambda bi, sp: (bi, 0, gate_off + sp)))
    args.append(gate)
    return pl.pallas_call(
        _dil_kernel,
        grid=(b, npair),
        in_specs=in_specs,
        out_specs=pl.BlockSpec((1, s, LANES), lambda bi, sp: (bi, 0, sp)),
        out_shape=jax.ShapeDtypeStruct((b, s, DIL_GROUP_WIDTH), BF16),
        scratch_shapes=[pltpu.VMEM((s, LANES), F32)] * 3,
        compiler_params=_params("parallel", "parallel"),
        name="dil_attention",
    )(*args)


def _split3(x):
    hi = x.astype(BF16)
    r1 = x - hi.astype(F32)
    mid = r1.astype(BF16)
    lo = (r1 - mid.astype(F32)).astype(BF16)
    return hi, mid, lo


SSM_HALO = 16
SSM_DT_COPIES = 2
SSM_UNROLL = 2


def _ssd_kernel(xs_ref, bm_ref, cm_ref, dt_ref, wx_ref, wb_ref, wc_ref, bx_ref, bb_ref, bc_ref,
                dtb_ref, a_ref, dsk_ref, y_ref, xs_s, bt_s, c_s, cb_s, acum_s, pt_s, st_s):
    s = xs_ref.shape[1]
    cl = SSM_CHUNK
    nchunk = s // cl
    hpg = SSM_HEADS_PER_GROUP
    npair = SSM_GROUP_WIDTH // LANES
    ncol = 2 * hpg

    def conv_chunk(c, carry):
        base = pl.multiple_of(c * cl, cl)
        prev_start = pl.multiple_of(jnp.maximum(base - SSM_HALO, 0), SSM_HALO)
        next_start = pl.multiple_of(jnp.minimum(base + cl, s - SSM_HALO), SSM_HALO)
        has_prev = jnp.where(c > 0, 1.0, 0.0)
        has_next = jnp.where(c < nchunk - 1, 1.0, 0.0)

        def conv(src, w_ref, bias_ref):
            ext = jnp.concatenate([src[0, pl.ds(prev_start, SSM_HALO), :].astype(F32) * has_prev,
                                   src[0, pl.ds(base, cl), :].astype(F32),
                                   src[0, pl.ds(next_start, SSM_HALO), :].astype(F32) * has_next], axis=0)
            acc = bias_ref[0]
            off = SSM_HALO - SSM_CONV // 2
            for tap in range(SSM_CONV):
                acc = acc + ext[off + tap:off + tap + cl, :] * w_ref[0, tap:tap + 1, :]
            return _silu(acc)

        xs_s[pl.ds(base, cl), :] = conv(xs_ref, wx_ref, bx_ref)
        bmat = conv(bm_ref, wb_ref, bb_ref).astype(BF16)
        cmat = conv(cm_ref, wc_ref, bc_ref).astype(BF16)
        bt_s[c] = bmat.astype(F32).T.astype(BF16)
        c_s[pl.ds(base, cl), :] = cmat
        cb_s[c] = lax.dot_general(cmat, bmat, (((1,), (1,)), ((), ())),
                                  preferred_element_type=F32)

        dt = jax.nn.softplus(dt_ref[0, pl.ds(base, cl), :] + dt_bias)
        pieces = _split3(dt * a_neg)
        acum_f = sum(jnp.dot(tril, p, preferred_element_type=F32) for p in pieces)
        acum_b = sum(jnp.dot(triu, p, preferred_element_type=F32) for p in pieces)
        acum = jnp.where(fwd_lane, acum_f, acum_b)
        a_end = jnp.where(fwd_lane[0:1], acum[cl - 1:cl, :], acum[0:1, :])
        w_state = dt * jnp.exp(a_end - acum)
        packed_t = jnp.where(lane < ncol, acum - jnp.log(dt), w_state).T
        acum_s[c] = acum
        pt_s[c] = packed_t[0:SSM_DT_COPIES * ncol, :]
        return carry

    ri = lax.broadcasted_iota(jnp.int32, (cl, cl), 0)
    ci = lax.broadcasted_iota(jnp.int32, (cl, cl), 1)
    lane = lax.broadcasted_iota(jnp.int32, (cl, LANES), 1)
    first = lane < HEAD_DIM
    fwd_lane = (lane % ncol) < hpg
    tril = jnp.where(ci <= ri, 1.0, 0.0).astype(BF16)
    triu = jnp.where(ci >= ri, 1.0, 0.0).astype(BF16)
    dt_bias = dtb_ref[0]
    a_neg = a_ref[0]

    lax.fori_loop(0, nchunk, conv_chunk, 0)

    for direction in range(2):
        if direction == 0:
            keep = ci <= ri
            end_row = cl - 1
        else:
            keep = ci >= ri
            end_row = 0

        st_s[...] = jnp.zeros_like(st_s)

        def local_part(c, direction=direction, keep=keep, end_row=end_row):
            rows = pl.ds(pl.multiple_of(c * cl, cl), cl)
            acum = acum_s[c]
            packed_t = pt_s[c]

            xs = xs_s[rows, :]
            bt = bt_s[c].astype(F32)
            cb = cb_s[c]
            yd_blocks, new_blocks, scale_blocks = [], [], []
            for pr in range(npair):
                xblk = xs[:, pr * LANES:(pr + 1) * LANES].astype(BF16)
                lhs, ea = [], []
                for j in (direction * hpg + 2 * pr, direction * hpg + 2 * pr + 1):
                    col = jnp.broadcast_to(acum[:, j:j + 1], (cl, cl))
                    row = packed_t[j:j + 1, :]
                    ws_row = packed_t[ncol + j:ncol + j + 1, :]
                    seg_dt = jnp.exp(jnp.where(keep, col - row, -jnp.inf))
                    lhs.append((cb * seg_dt).astype(BF16))
                    lhs.append((bt * ws_row).astype(BF16))
                    ea.append(jnp.exp(col))
                prod = jnp.dot(jnp.concatenate(lhs, axis=0), xblk, preferred_element_type=F32)
                yd_blocks.append(jnp.where(first, prod[0:cl], prod[2 * cl:3 * cl]))
                new_blocks.append(jnp.where(first, prod[cl:2 * cl], prod[3 * cl:4 * cl]))
                scale_blocks.append(jnp.where(first, ea[0], ea[1]))
            return rows, xs, yd_blocks, new_blocks, scale_blocks

        def trip(step, carry, direction=direction, end_row=end_row):
            chunks = [step * SSM_UNROLL + i for i in range(SSM_UNROLL)]
            if direction == 1:
                chunks = [nchunk - 1 - c for c in chunks]
            parts = [local_part(c) for c in chunks]
            state = [st_s[:, pr * LANES:(pr + 1) * LANES] for pr in range(npair)]
            for rows, xs, yd_blocks, new_blocks, scale_blocks in parts:
                cmat = c_s[rows, :]
                y_blocks = []
                for pr in range(npair):
                    y_off = jnp.dot(cmat, state[pr].astype(BF16), preferred_element_type=F32)
                    y_blocks.append(yd_blocks[pr] + y_off * scale_blocks[pr])
                    state[pr] = (state[pr] * scale_blocks[pr][end_row:end_row + 1, :]
                                 + new_blocks[pr])
                y = jnp.concatenate(y_blocks, axis=1)
                if direction == 0:
                    y_ref[0, rows, :] = y + dsk_ref[0] * xs
                else:
                    y_ref[0, rows, :] = y_ref[0, rows, :] + y
            for pr in range(npair):
                st_s[:, pr * LANES:(pr + 1) * LANES] = state[pr]
            return carry

        lax.fori_loop(0, nchunk // SSM_UNROLL, trip, 0)


def _ssd(xbc, dtp, conv_w, conv_b, dt_bias, a_neg, d_skip):
    b, s, _ = xbc.shape
    gw = SSM_GROUP_WIDTH
    b_off = SSM_INNER // LANES
    c_off = b_off + SSM_GROUPS
    nchunk = s // SSM_CHUNK

    def seq(width, off):
        return pl.BlockSpec((1, s, width), lambda bi, g: (bi, 0, off + g))

    def par(rows, width, off):
        return pl.BlockSpec((1, rows, width), lambda bi, g: (0, 0, off + g))

    return pl.pallas_call(
        _ssd_kernel,
        grid=(b, SSM_GROUPS),
        in_specs=[seq(gw, 0), seq(LANES, b_off), seq(LANES, c_off), seq(LANES, 0),
                  par(SSM_CONV, gw, 0), par(SSM_CONV, LANES, b_off), par(SSM_CONV, LANES, c_off),
                  par(1, gw, 0), par(1, LANES, b_off), par(1, LANES, c_off),
                  pl.BlockSpec((1, 1, LANES), lambda bi, g: (g, 0, 0)),
                  pl.BlockSpec((1, 1, LANES), lambda bi, g: (g, 0, 0)),
                  pl.BlockSpec((1, 1, gw), lambda bi, g: (g, 0, 0))],
        out_specs=pl.BlockSpec((1, s, gw), lambda bi, g: (bi, 0, g)),
        out_shape=jax.ShapeDtypeStruct((b, s, SSM_INNER), F32),
        scratch_shapes=[pltpu.VMEM((s, gw), F32),
                        pltpu.VMEM((nchunk, SSM_STATE, SSM_CHUNK), BF16),
                        pltpu.VMEM((s, LANES), BF16),
                        pltpu.VMEM((nchunk, SSM_CHUNK, SSM_CHUNK), F32),
                        pltpu.VMEM((nchunk, SSM_CHUNK, LANES), F32),
                        pltpu.VMEM((nchunk, SSM_DT_COPIES * 2 * SSM_HEADS_PER_GROUP, SSM_CHUNK), F32),
                        pltpu.VMEM((SSM_STATE, gw), F32)],
        compiler_params=_params("parallel", "parallel"),
        name="ssd",
    )(xbc, xbc, xbc, dtp, conv_w, conv_w, conv_w, conv_b, conv_b, conv_b, dt_bias, a_neg, d_skip)


def _tail_kernel(x_ref, ya_ref, yb_ref, yc_ref, z_ref, ua_ref, ub_ref, uc_ref, p_ref,
                 nw_ref, woa_ref, wob_ref, woc_ref, wout_ref, pg_ref, wpg_ref, wple_ref, gn_ref,
                 o_ref, *h_ref, final):
    def mm(a, w_ref):
        return jnp.dot(a, w_ref[...], preferred_element_type=F32)

    ya = mm(ya_ref[...], woa_ref)
    yb = mm(yb_ref[...], wob_ref)
    yc_in = _rms(yc_ref[...] * _silu(z_ref[...]), nw_ref[...]).astype(BF16)
    yc = mm(yc_in, woc_ref)
    merged = (_sigmoid(ua_ref[...]) * ya + _sigmoid(ub_ref[...]) * yb
              + _sigmoid(uc_ref[...]) * yc)
    x1 = x_ref[...] + mm(merged.astype(BF16), wout_ref)
    gate = jax.nn.sigmoid(mm(_rms(x1, pg_ref[...]).astype(BF16), wpg_ref))
    x2 = x1 + mm(p_ref[...].astype(BF16), wple_ref) * gate
    if final:
        o_ref[...] = _rms(x2, gn_ref[...])
    else:
        o_ref[...] = x2
        h_ref[0][...] = _rms(x2, gn_ref[...]).astype(BF16)


MISC_WIDTH = SSM_INNER + DIL_GROUP_WIDTH + 3 * D_MODEL
MISC_GB_LANE_BLOCK = SSM_INNER // LANES
MISC_U_BLOCK = (SSM_INNER + DIL_GROUP_WIDTH) // D_MODEL


def _tail(x2d, ya, yb, yc, misc, p2d, nw, woa, wob, woc, wout, pg, wpg, wple, gn, final, tm=512):
    m = x2d.shape[0]
    out_specs = pl.BlockSpec((tm, D_MODEL), lambda i: (i, 0))
    out_shape = jax.ShapeDtypeStruct((m, D_MODEL), F32)
    if not final:
        out_specs = [out_specs, pl.BlockSpec((tm, D_MODEL), lambda i: (i, 0))]
        out_shape = [out_shape, jax.ShapeDtypeStruct((m, D_MODEL), BF16)]

    def rows(width, off=0):
        return pl.BlockSpec((tm, width), lambda i: (i, off))

    def whole(arr):
        return pl.BlockSpec(arr.shape, lambda i: (0, 0), pipeline_mode=pl.Buffered(1))

    ub = MISC_U_BLOCK
    return pl.pallas_call(
        functools.partial(_tail_kernel, final=final),
        grid=(m // tm,),
        in_specs=[rows(D_MODEL), rows(NA_WIDTH), rows(DIL_GROUP_WIDTH), rows(SSM_INNER),
                  rows(SSM_INNER, 0), rows(D_MODEL, ub), rows(D_MODEL, ub + 1), rows(D_MODEL, ub + 2),
                  rows(PLE_DIM),
                  whole(nw), whole(woa), whole(wob), whole(woc), whole(wout), whole(pg), whole(wpg),
                  whole(wple), whole(gn)],
        out_specs=out_specs,
        out_shape=out_shape,
        compiler_params=_params("parallel"),
        name="tail",
    )(x2d, ya, yb, yc, misc, misc, misc, misc, p2d, nw, woa, wob, woc, wout, pg, wpg, wple, gn)


def _prep_weights(w_in, conv_w, conv_b, a_log, dt_bias, d_skip):
    depth = w_in.shape[0]
    offs = np.concatenate([[0], np.cumsum(IN_SPLITS)])
    (qa, ka, va, ga, qb, kb, vb, gb, xbc, z, dtr, ua, ub, uc) = [
        w_in[:, :, int(offs[i]):int(offs[i + 1])] for i in range(len(IN_SPLITS))]
    w_a = jnp.concatenate([qa * SCORE_SCALE, ka, va, ga], axis=2).astype(BF16)
    gw = DIL_GROUP_WIDTH
    w_b = [jnp.concatenate([_rotary_layout(qb[:, :, g * gw:(g + 1) * gw] * SCORE_SCALE),
                            _rotary_layout(kb[:, :, g * gw:(g + 1) * gw]),
                            vb[:, :, g * gw:(g + 1) * gw]], axis=2).astype(BF16)
           for g in range(len(DIL_PAIRS))]
    w_misc = jnp.concatenate([z, gb, ua, ub, uc], axis=2).astype(BF16)

    hpg = SSM_HEADS_PER_GROUP

    def per_group(t):
        lead = t.shape[:-2]
        t = t.reshape(lead + (2, SSM_GROUPS, hpg))
        t = jnp.moveaxis(t, -2, -3).reshape(lead + (SSM_GROUPS, 2 * hpg))
        t = jnp.tile(t, (1,) * (t.ndim - 1) + (SSM_DT_COPIES,))
        pad = [(0, 0)] * (t.ndim - 1) + [(0, LANES - 2 * hpg * SSM_DT_COPIES)]
        return jnp.pad(t, pad).reshape(lead + (SSM_GROUPS * LANES,))

    w_dt = per_group(dtr.reshape(depth, D_MODEL, 2, SSM_HEADS)).astype(BF16)
    w_xbc = xbc.astype(BF16)
    dtb = per_group(dt_bias.astype(F32)).reshape(depth, SSM_GROUPS, 1, LANES)
    a_neg = per_group(-jnp.exp(a_log.astype(F32))).reshape(depth, SSM_GROUPS, 1, LANES)
    dsk = jnp.repeat(d_skip.astype(F32), HEAD_DIM, axis=1).reshape(depth, SSM_GROUPS, 1, SSM_GROUP_WIDTH)
    cw = conv_w.astype(F32).reshape(depth, 1, SSM_CONV, SSM_CONV_DIM)
    cbias = conv_b.astype(F32).reshape(depth, 1, 1, SSM_CONV_DIM)
    return w_a, w_b, w_misc, w_xbc, w_dt, dtb, a_neg, dsk, cw, cbias


def kernel(x, p, norm_w, w_in, na_rpb, conv_w, conv_b, a_log, dt_bias, d_skip, ssm_norm_w,
           w_oa, w_ob, w_oc, w_out, ple_norm_w, w_ple, w_ple_gate, final_norm_w):
    b, s, dm = x.shape
    depth = w_in.shape[0]
    m = b * s
    w_a, w_b, w_misc, w_xbc, w_dt, dtb, a_neg, dsk, cw, cbias = _prep_weights(
        w_in, conv_w, conv_b, a_log, dt_bias, d_skip)
    tabs = [_rotary_tables(s, d) for _, d in DIL_PAIRS]
    row = lambda v: v.astype(F32).reshape(1, -1)

    x2d = x.reshape(m, dm)
    h = _norm(x2d, row(norm_w[0]))
    for i in range(depth):
        final = i == depth - 1
        qkvg = _proj(h, w_a[i], 1024, BF16).reshape(b, s, 4 * NA_WIDTH)
        misc = _proj(h, w_misc[i], 1024, BF16)
        xbc = _proj(h, w_xbc[i], SSM_CONV_DIM // 2, BF16).reshape(b, s, SSM_CONV_DIM)
        dtp = _proj(h, w_dt[i], SSM_GROUPS * LANES, F32).reshape(b, s, SSM_GROUPS * LANES)
        qkvs = [_proj_dil(h.reshape(b, s // d, d, dm).transpose(0, 2, 1, 3), w_b[gi][i], tabs[gi])
                for gi, (_, d) in enumerate(DIL_PAIRS)]

        ya = _na_attention(qkvg, _na_bias_table(na_rpb[i])).reshape(m, NA_WIDTH)
        yb = _dil_attention(qkvs, misc.reshape(b, s, MISC_WIDTH), MISC_GB_LANE_BLOCK)
        yb = yb.reshape(m, DIL_GROUP_WIDTH)
        yc = _ssd(xbc, dtp, cw[i], cbias[i], dtb[i], a_neg[i], dsk[i]).reshape(m, SSM_INNER)

        gn = row(final_norm_w) if final else row(norm_w[i + 1])
        out = _tail(x2d, ya, yb, yc, misc, p[i].reshape(m, PLE_DIM), row(ssm_norm_w[i]),
                    w_oa[i].astype(BF16), w_ob[i].astype(BF16), w_oc[i].astype(BF16),
                    w_out[i].astype(BF16), row(ple_norm_w[i]), w_ple_gate[i].astype(BF16),
                    w_ple[i].astype(BF16), gn, final=final)
        if final:
            x2d = out
        else:
            x2d, h = out
    return x2d.reshape(b, s, dm)
```

```python
import functools
import math

import numpy as np
import jax
import jax.numpy as jnp
from jax import lax
from jax.experimental import pallas as pl
from jax.experimental.pallas import tpu as pltpu

F32 = jnp.float32
BF16 = jnp.bfloat16

LANES = 128
VMEM_LIMIT_BYTES = 56 * 1024 * 1024

D_MODEL = 1024
GRID_W = 64
HEAD_DIM = 64
EPS = 1e-6
PLE_DIM = 256

NA_HEADS = 16
NA_WIDTH = NA_HEADS * HEAD_DIM
NA_WIN_ROWS = 8
NA_WIN_COLS = 16

DIL_PAIRS = ((128, 1), (512, 4), (2048, 16))
DIL_HEADS_PER_GROUP = 8
DIL_GROUP_WIDTH = DIL_HEADS_PER_GROUP * HEAD_DIM
DIL_WIDTH = DIL_GROUP_WIDTH * len(DIL_PAIRS)
DIL_BLK = 64
ROPE_THETA = 500000.0
ROPE_DIM = HEAD_DIM // 4
ROPE_HALF = ROPE_DIM // 2

SSM_INNER = 1536
SSM_HEADS = 24
SSM_GROUPS = 4
SSM_HEADS_PER_GROUP = SSM_HEADS // SSM_GROUPS
SSM_GROUP_WIDTH = SSM_HEADS_PER_GROUP * HEAD_DIM
SSM_STATE = 128
SSM_CONV = 5
SSM_CHUNK = 128
SSM_CONV_DIM = SSM_INNER + 2 * SSM_GROUPS * SSM_STATE

IN_SPLITS = (NA_WIDTH, NA_WIDTH, NA_WIDTH, NA_WIDTH,
             DIL_WIDTH, DIL_WIDTH, DIL_WIDTH, DIL_GROUP_WIDTH,
             SSM_CONV_DIM, SSM_INNER, 2 * SSM_HEADS,
             D_MODEL, D_MODEL, D_MODEL)

LOG2E = math.log2(math.e)
SCORE_SCALE = HEAD_DIM ** -0.5 * LOG2E


def _params(*semantics):
    return pltpu.CompilerParams(dimension_semantics=semantics,
                                vmem_limit_bytes=VMEM_LIMIT_BYTES)


def _rms(x, g):
    return x * lax.rsqrt(jnp.mean(x * x, axis=-1, keepdims=True) + EPS) * g


def _silu(x):
    x = x.astype(F32)
    return x * jax.nn.sigmoid(x)


def _sigmoid(x):
    return jax.nn.sigmoid(x.astype(F32))


def _norm_kernel(x_ref, g_ref, h_ref):
    h_ref[...] = _rms(x_ref[...], g_ref[...]).astype(h_ref.dtype)


def _norm(x2d, g, tm=1024):
    m, k = x2d.shape
    return pl.pallas_call(
        _norm_kernel,
        grid=(m // tm,),
        in_specs=[pl.BlockSpec((tm, k), lambda i: (i, 0)), pl.BlockSpec((1, k), lambda i: (0, 0))],
        out_specs=pl.BlockSpec((tm, k), lambda i: (i, 0)),
        out_shape=jax.ShapeDtypeStruct((m, k), BF16),
        compiler_params=_params("parallel"),
        name="norm",
    )(x2d, g)


def _proj_kernel(h_ref, w_ref, o_ref, *, first_tile_scale):
    acc = jnp.dot(h_ref[...], w_ref[...], preferred_element_type=F32)
    if first_tile_scale is None:
        o_ref[...] = acc.astype(o_ref.dtype)
    else:
        @pl.when(pl.program_id(1) == 0)
        def _():
            o_ref[...] = (acc * first_tile_scale).astype(o_ref.dtype)

        @pl.when(pl.program_id(1) != 0)
        def _():
            o_ref[...] = acc.astype(o_ref.dtype)


def _proj(h2d, w, tn, out_dtype, tm=1024, first_tile_scale=None):
    m, k = h2d.shape
    n = w.shape[1]
    return pl.pallas_call(
        functools.partial(_proj_kernel, first_tile_scale=first_tile_scale),
        grid=(m // tm, n // tn),
        in_specs=[pl.BlockSpec((tm, k), lambda i, j: (i, 0)),
                  pl.BlockSpec((k, tn), lambda i, j: (0, j))],
        out_specs=pl.BlockSpec((tm, tn), lambda i, j: (i, j)),
        out_shape=jax.ShapeDtypeStruct((m, n), out_dtype),
        compiler_params=_params("parallel", "arbitrary"),
        name="proj",
    )(h2d, w)


MXU_N = 256


def _proj_dil_kernel(h_ref, w_ref, c_ref, s_ref, o_ref):
    tn = w_ref.shape[1]
    hp = h_ref[0]

    @pl.when(pl.program_id(3) < 2)
    def _():
        c = c_ref[0, 0]
        sn = s_ref[0, 0]
        for nb in range(tn // MXU_N):
            acc = jnp.dot(hp, w_ref[:, nb * MXU_N:(nb + 1) * MXU_N], preferred_element_type=F32)
            for cb in range(MXU_N // LANES):
                blk = acc[:, cb * LANES:(cb + 1) * LANES]
                rot = blk * c + pltpu.roll(blk, LANES // 2, 1) * sn
                lo = nb * MXU_N + cb * LANES
                o_ref[0, 0, :, lo:lo + LANES] = rot.astype(o_ref.dtype)

    @pl.when(pl.program_id(3) >= 2)
    def _():
        o_ref[0, 0] = jnp.dot(hp, w_ref[...], preferred_element_type=F32).astype(o_ref.dtype)


def _proj_dil(hd, w, tabs, d, tm=1024):
    b, l, dk = hd.shape
    k = dk // d
    n = w.shape[1]
    tn = DIL_GROUP_WIDTH
    rows = min(l, tm)
    c, sn = tabs
    tab_spec = pl.BlockSpec((1, 1, rows, LANES), lambda bi, r, i, j: (jnp.minimum(j, 1), r, i, 0))
    return pl.pallas_call(
        _proj_dil_kernel,
        grid=(b, d, l // rows, n // tn),
        in_specs=[pl.BlockSpec((1, rows, k), lambda bi, r, i, j: (bi, i, r)),
                  pl.BlockSpec((k, tn), lambda bi, r, i, j: (0, j)),
                  tab_spec, tab_spec],
        out_specs=pl.BlockSpec((1, 1, rows, tn), lambda bi, r, i, j: (bi, r, i, j)),
        out_shape=jax.ShapeDtypeStruct((b, d, l, n), BF16),
        compiler_params=_params("parallel", "parallel", "parallel", "arbitrary"),
        name=f"proj_dil{d}",
    )(hd, w, c, sn)


def _rotary_lane_sources():
    half = LANES // 2
    head = np.zeros(LANES, np.int64)
    dim = np.zeros(LANES, np.int64)
    for lane in range(LANES):
        off = lane % half
        is_b = lane >= half
        if off < 2 * ROPE_HALF:
            head[lane] = off // ROPE_HALF
            dim[lane] = off % ROPE_HALF + (ROPE_HALF if is_b else 0)
        else:
            head[lane] = 1 if is_b else 0
            dim[lane] = off
    return head, dim


def _rotary_tables(s, d):
    inv = ROPE_THETA ** (-jnp.arange(0, ROPE_DIM, 2, dtype=F32) / ROPE_DIM)
    ang = jnp.arange(s).astype(F32)[:, None] * inv[None, :]
    cos, sin = jnp.cos(ang), jnp.sin(ang)
    _, dim = _rotary_lane_sources()
    lanes = np.arange(LANES)
    rotary = (lanes % (LANES // 2)) < 2 * ROPE_HALF
    is_b = lanes >= LANES // 2
    idx = dim % ROPE_HALF
    c = jnp.where(rotary[None, :], cos[:, idx], 1.0)
    sn = jnp.where(rotary[None, :], jnp.where(is_b[None, :], sin[:, idx], -sin[:, idx]), 0.0)

    def arrange(t):
        t = t.reshape(s // d, d, LANES).transpose(1, 0, 2)
        return jnp.stack([t * SCORE_SCALE, t])

    return arrange(c), arrange(sn)


def _rotary_layout(t):
    lead = t.shape[:-1]
    t = t.reshape(lead + (t.shape[-1] // LANES, 2, HEAD_DIM))
    npair = t.shape[-3]
    a = t[..., :, :, 0:ROPE_HALF].reshape(lead + (npair, 2 * ROPE_HALF))
    b = t[..., :, :, ROPE_HALF:ROPE_DIM].reshape(lead + (npair, 2 * ROPE_HALF))
    rest0 = t[..., :, 0, ROPE_DIM:]
    rest1 = t[..., :, 1, ROPE_DIM:]
    return jnp.concatenate([a, rest0, b, rest1], axis=-1).reshape(lead + (npair * LANES,))


NA_UNROLL = 8


def _na_kernel(q_ref, k_ref, v_ref, g_ref, tb_ref, o_ref):
    rows = q_ref.shape[1] // GRID_W
    nkeys = NA_WIN_ROWS * GRID_W
    lane = lax.broadcasted_iota(jnp.int32, (GRID_W, LANES), 1)
    first = lane < HEAD_DIM

    def body(step, carry):
        units = []
        for i in range(NA_UNROLL):
            r = step * NA_UNROLL + i
            r0 = jnp.clip(r - NA_WIN_ROWS // 2, 0, rows - NA_WIN_ROWS)
            st = r0 - r + NA_WIN_ROWS - 1
            qs = pl.ds(pl.multiple_of(r * GRID_W, GRID_W), GRID_W)
            ks = pl.ds(pl.multiple_of(r0 * GRID_W, GRID_W), nkeys)
            q = q_ref[0, qs, :]
            kw = k_ref[0, ks, :]
            zero = jnp.zeros_like(q)
            q2 = jnp.concatenate([jnp.where(first, q, zero), jnp.where(first, zero, q)], axis=0)
            sc = lax.dot_general(q2, kw, (((1,), (1,)), ((), ())),
                                 preferred_element_type=F32)
            units.append((qs, ks, sc + tb_ref[0, st]))
        probs = []
        for qs, ks, sc in units:
            mx = jnp.max(sc, axis=-1, keepdims=True)
            e = jnp.exp2(sc - mx)
            probs.append((e.astype(BF16), jnp.sum(e, axis=-1, keepdims=True)))
        for (qs, ks, _), (p, den) in zip(units, probs):
            o2 = jnp.dot(p, v_ref[0, ks, :], preferred_element_type=F32) / den
            o = jnp.where(first, o2[0:GRID_W], o2[GRID_W:])
            o_ref[0, qs, :] = (o * _silu(g_ref[0, qs, :])).astype(o_ref.dtype)
        return carry

    lax.fori_loop(0, rows // NA_UNROLL, body, 0)


def _na_bias_table(rpb):
    h = rpb.shape[0]
    cq = np.arange(GRID_W)[:, None]
    ck = np.arange(GRID_W)[None, :]
    ws = np.clip(cq - NA_WIN_COLS // 2, 0, GRID_W - NA_WIN_COLS)
    in_win = (ck >= ws) & (ck < ws + NA_WIN_COLS)
    line = 2 * GRID_W
    lo = GRID_W - NA_WIN_COLS
    ext = jnp.pad(rpb.astype(F32) * LOG2E, ((0, 0), (0, 0), (lo, line - lo - rpb.shape[-1])))
    skew = jnp.broadcast_to(ext[:, :, None, :], ext.shape[:2] + (GRID_W, line))
    skew = skew.reshape(ext.shape[:2] + (GRID_W * line,))[:, :, :GRID_W * (line - 1)]
    toep = skew.reshape(ext.shape[:2] + (GRID_W, line - 1))[:, :, :, GRID_W - 1:]
    colb = jnp.where(in_win, toep, -jnp.inf)
    tb = jnp.stack([colb[:, st:st + NA_WIN_ROWS] for st in range(NA_WIN_ROWS)], axis=1)
    tb = tb.reshape(h // 2, 2, NA_WIN_ROWS, NA_WIN_ROWS, GRID_W, GRID_W)
    tb = tb.transpose(0, 2, 1, 4, 3, 5).reshape(h // 2, NA_WIN_ROWS, 2 * GRID_W, NA_WIN_ROWS * GRID_W)
    return tb


def _na_attention(qkvg, tb):
    b, s, _ = qkvg.shape
    npair = NA_WIDTH // LANES

    def col(off):
        return pl.BlockSpec((1, s, LANES), lambda bi, hp: (bi, 0, off + hp))

    return pl.pallas_call(
        _na_kernel,
        grid=(b, npair),
        in_specs=[col(0), col(npair), col(2 * npair), col(3 * npair),
                  pl.BlockSpec((1,) + tb.shape[1:], lambda bi, hp: (hp, 0, 0, 0))],
        out_specs=pl.BlockSpec((1, s, LANES), lambda bi, hp: (bi, 0, hp)),
        out_shape=jax.ShapeDtypeStruct((b, s, NA_WIDTH), BF16),
        compiler_params=_params("parallel", "parallel"),
        name="na_attention",
    )(qkvg, qkvg, qkvg, qkvg, tb)


DIL_QB = 2 * DIL_BLK
DIL_KB = 4 * DIL_BLK
DIL_UNROLL = 4


def _dil_kernel(q0, k0, v0, q1, k1, v1, q2, k2, v2, g_ref, o_ref, acc_ref, m_ref, w_ref):
    s = o_ref.shape[1]
    lane = lax.broadcasted_iota(jnp.int32, (DIL_QB, LANES), 1)
    first = lane < HEAD_DIM
    half_off = lane % (LANES // 2)
    q_first = (half_off < ROPE_HALF) | ((half_off >= 2 * ROPE_HALF) & (lane < LANES // 2))
    v_first = lax.broadcasted_iota(jnp.int32, (DIL_KB, LANES), 1) < HEAD_DIM
    qi = lax.broadcasted_iota(jnp.int32, (DIL_QB, DIL_KB), 0)
    ki = lax.broadcasted_iota(jnp.int32, (DIL_QB, DIL_KB), 1)

    for g, (q_ref, k_ref, v_ref) in enumerate(((q0, k0, v0), (q1, k1, v1), (q2, k2, v2))):
        d = DIL_PAIRS[g][1]
        l = s // d
        nu = l // DIL_QB

        def trip(step, carry, g=g, d=d, l=l, nu=nu, q_ref=q_ref, k_ref=k_ref, v_ref=v_ref):
            units = []
            for i in range(DIL_UNROLL):
                idx = step * DIL_UNROLL + i
                rho = idx // nu
                u = idx % nu
                q_start = pl.multiple_of(u * DIL_QB, DIL_QB)
                k_start = pl.multiple_of(jnp.clip(u * DIL_QB - DIL_BLK, 0, l - DIL_KB), DIL_BLK)
                q = q_ref[0, rho, pl.ds(q_start, DIL_QB), :]
                kw = k_ref[0, rho, pl.ds(k_start, DIL_KB), :]
                valid = jnp.abs((ki + k_start) - (qi + q_start)) <= DIL_BLK
                zero = jnp.zeros_like(q)
                scs = []
                for h in range(2):
                    qh = jnp.where(q_first, q, zero) if h == 0 else jnp.where(q_first, zero, q)
                    sc = lax.dot_general(qh, kw, (((1,), (1,)), ((), ())),
                                         preferred_element_type=F32)
                    scs.append(jnp.where(valid, sc, -jnp.inf))
                units.append((rho, q_start, k_start, scs))
            soft = []
            for rho, q_start, k_start, scs in units:
                es, mxs = [], []
                for sc in scs:
                    mx = jnp.max(sc, axis=-1, keepdims=True)
                    es.append(jnp.exp2(sc - mx).astype(BF16))
                    mxs.append(mx)
                soft.append((es, jnp.where(first, mxs[0], mxs[1])))
            for (rho, q_start, k_start, _), (es, mx) in zip(units, soft):
                vw = v_ref[0, rho, pl.ds(k_start, DIL_KB), :]
                one = jnp.ones_like(vw)
                o0 = jnp.dot(es[0], jnp.where(v_first, vw, one), preferred_element_type=F32)
                o1 = jnp.dot(es[1], jnp.where(v_first, one, vw), preferred_element_type=F32)
                num = jnp.where(first, o0, o1)
                den = pltpu.roll(jnp.where(first, o1, o0), HEAD_DIM, 1)
                if d == 1:
                    rows = pl.ds(q_start, DIL_QB)
                else:
                    rows = pl.ds(rho + d * q_start, DIL_QB, stride=d)
                if g == 0:
                    acc_ref[rows, :] = num
                    m_ref[rows, :] = mx
                    w_ref[rows, :] = den
                else:
                    m_old = m_ref[rows, :]
                    m_new = jnp.maximum(m_old, mx)
                    a = jnp.exp2(m_old - m_new)
                    bw = jnp.exp2(mx - m_new)
                    acc_ref[rows, :] = acc_ref[rows, :] * a + num * bw
                    w_ref[rows, :] = w_ref[rows, :] * a + den * bw
                    m_ref[rows, :] = m_new
            return carry

        lax.fori_loop(0, d * nu // DIL_UNROLL, trip, 0)

    def finish(i, carry):
        rows = pl.ds(pl.multiple_of(i * 256, 256), 256)
        y = acc_ref[rows, :] / w_ref[rows, :]
        o_ref[0, rows, :] = (y * _silu(g_ref[0, rows, :])).astype(o_ref.dtype)
        return carry

    lax.fori_loop(0, s // 256, finish, 0)


def _dil_attention(qkvs, gate, gate_off):
    b = gate.shape[0]
    s = gate.shape[1]
    npair = DIL_GROUP_WIDTH // LANES
    in_specs = []
    args = []
    for arr in qkvs:
        d, l = arr.shape[1], arr.shape[2]
        for part in range(3):
            in_specs.append(pl.BlockSpec((1, d, l, LANES),
                                         lambda bi, sp, part=part: (bi, 0, 0, part * npair + sp)))
            args.append(arr)
    in_specs.append(pl.BlockSpec((1, s, LANES), lambda bi, sp: (bi, 0, gate_off + sp)))
    args.append(gate)
    return pl.pallas_call(
        _dil_kernel,
        grid=(b, npair),
        in_specs=in_specs,
        out_specs=pl.BlockSpec((1, s, LANES), lambda bi, sp: (bi, 0, sp)),
        out_shape=jax.ShapeDtypeStruct((b, s, DIL_GROUP_WIDTH), BF16),
        scratch_shapes=[pltpu.VMEM((s, LANES), F32)] * 3,
        compiler_params=_params("parallel", "parallel"),
        name="dil_attention",
    )(*args)


def _split3(x):
    hi = x.astype(BF16)
    r1 = x - hi.astype(F32)
    mid = r1.astype(BF16)
    lo = (r1 - mid.astype(F32)).astype(BF16)
    return hi, mid, lo


SSM_HALO = 16
SSM_DT_COPIES = 2
SSM_UNROLL = 2


def _ssd_kernel(xs_ref, bm_ref, cm_ref, dt_ref, wx_ref, wb_ref, wc_ref, bx_ref, bb_ref, bc_ref,
                dtb_ref, a_ref, dsk_ref, y_ref, xs_s, bt_s, c_s, cb_s, acum_s, pt_s, st_s):
    s = xs_ref.shape[1]
    cl = SSM_CHUNK
    nchunk = s // cl
    hpg = SSM_HEADS_PER_GROUP
    npair = SSM_GROUP_WIDTH // LANES
    ncol = 2 * hpg

    def conv_chunk(c, carry):
        base = pl.multiple_of(c * cl, cl)
        prev_start = pl.multiple_of(jnp.maximum(base - SSM_HALO, 0), SSM_HALO)
        next_start = pl.multiple_of(jnp.minimum(base + cl, s - SSM_HALO), SSM_HALO)
        has_prev = jnp.where(c > 0, 1.0, 0.0)
        has_next = jnp.where(c < nchunk - 1, 1.0, 0.0)

        def conv(src, w_ref, bias_ref):
            ext = jnp.concatenate([src[0, pl.ds(prev_start, SSM_HALO), :].astype(F32) * has_prev,
                                   src[0, pl.ds(base, cl), :].astype(F32),
                                   src[0, pl.ds(next_start, SSM_HALO), :].astype(F32) * has_next], axis=0)
            acc = bias_ref[0]
            off = SSM_HALO - SSM_CONV // 2
            for tap in range(SSM_CONV):
                acc = acc + ext[off + tap:off + tap + cl, :] * w_ref[0, tap:tap + 1, :]
            return _silu(acc)

        xs_s[pl.ds(base, cl), :] = conv(xs_ref, wx_ref, bx_ref)
        bmat = conv(bm_ref, wb_ref, bb_ref).astype(BF16)
        cmat = conv(cm_ref, wc_ref, bc_ref).astype(BF16)
        bt_s[c] = bmat.astype(F32).T.astype(BF16)
        c_s[pl.ds(base, cl), :] = cmat
        cb_s[c] = lax.dot_general(cmat, bmat, (((1,), (1,)), ((), ())),
                                  preferred_element_type=F32)

        dt = jax.nn.softplus(dt_ref[0, pl.ds(base, cl), :] + dt_bias)
        pieces = _split3(dt * a_neg)
        acum_f = sum(jnp.dot(tril, p, preferred_element_type=F32) for p in pieces)
        acum_b = sum(jnp.dot(triu, p, preferred_element_type=F32) for p in pieces)
        acum = jnp.where(fwd_lane, acum_f, acum_b)
        a_end = jnp.where(fwd_lane[0:1], acum[cl - 1:cl, :], acum[0:1, :])
        w_state = dt * jnp.exp(a_end - acum)
        packed_t = jnp.where(lane < ncol, acum - jnp.log(dt), w_state).T
        acum_s[c] = acum
        pt_s[c] = packed_t[0:SSM_DT_COPIES * ncol, :]
        return carry

    ri = lax.broadcasted_iota(jnp.int32, (cl, cl), 0)
    ci = lax.broadcasted_iota(jnp.int32, (cl, cl), 1)
    lane = lax.broadcasted_iota(jnp.int32, (cl, LANES), 1)
    first = lane < HEAD_DIM
    fwd_lane = (lane % ncol) < hpg
    tril = jnp.where(ci <= ri, 1.0, 0.0).astype(BF16)
    triu = jnp.where(ci >= ri, 1.0, 0.0).astype(BF16)
    dt_bias = dtb_ref[0]
    a_neg = a_ref[0]

    lax.fori_loop(0, nchunk, conv_chunk, 0)

    for direction in range(2):
        if direction == 0:
            keep = ci <= ri
            end_row = cl - 1
        else:
            keep = ci >= ri
            end_row = 0

        st_s[...] = jnp.zeros_like(st_s)

        def local_part(c, direction=direction, keep=keep, end_row=end_row):
            rows = pl.ds(pl.multiple_of(c * cl, cl), cl)
            acum = acum_s[c]
            packed_t = pt_s[c]

            xs = xs_s[rows, :]
            bt = bt_s[c].astype(F32)
            cb = cb_s[c]
            yd_blocks, new_blocks, scale_blocks = [], [], []
            for pr in range(npair):
                xblk = xs[:, pr * LANES:(pr + 1) * LANES].astype(BF16)
                lhs, cols = [], []
                for j in (direction * hpg + 2 * pr, direction * hpg + 2 * pr + 1):
                    col = jnp.broadcast_to(acum[:, j:j + 1], (cl, cl))
                    row = packed_t[j:j + 1, :]
                    ws_row = packed_t[ncol + j:ncol + j + 1, :]
                    seg_dt = jnp.exp(jnp.where(keep, col - row, -jnp.inf))
                    lhs.append((cb * seg_dt).astype(BF16))
                    lhs.append((bt * ws_row).astype(BF16))
                    cols.append(col)
                prod = jnp.dot(jnp.concatenate(lhs, axis=0), xblk, preferred_element_type=F32)
                yd_blocks.append(jnp.where(first, prod[0:cl], prod[2 * cl:3 * cl]))
                new_blocks.append(jnp.where(first, prod[cl:2 * cl], prod[3 * cl:4 * cl]))
                scale_blocks.append(jnp.exp(jnp.where(first, cols[0], cols[1])))
            return rows, xs, yd_blocks, new_blocks, scale_blocks

        def trip(step, carry, direction=direction, end_row=end_row):
            chunks = [step * SSM_UNROLL + i for i in range(SSM_UNROLL)]
            if direction == 1:
                chunks = [nchunk - 1 - c for c in chunks]
            parts = [local_part(c) for c in chunks]
            state = [st_s[:, pr * LANES:(pr + 1) * LANES] for pr in range(npair)]
            for rows, xs, yd_blocks, new_blocks, scale_blocks in parts:
                cmat = c_s[rows, :]
                y_blocks = []
                for pr in range(npair):
                    y_off = jnp.dot(cmat, state[pr].astype(BF16), preferred_element_type=F32)
                    y_blocks.append(yd_blocks[pr] + y_off * scale_blocks[pr])
                    state[pr] = (state[pr] * scale_blocks[pr][end_row:end_row + 1, :]
                                 + new_blocks[pr])
                y = jnp.concatenate(y_blocks, axis=1)
                if direction == 0:
                    y_ref[0, rows, :] = y + dsk_ref[0] * xs
                else:
                    y_ref[0, rows, :] = y_ref[0, rows, :] + y
            for pr in range(npair):
                st_s[:, pr * LANES:(pr + 1) * LANES] = state[pr]
            return carry

        lax.fori_loop(0, nchunk // SSM_UNROLL, trip, 0)


def _ssd(xbc, dtp, conv_w, conv_b, dt_bias, a_neg, d_skip):
    b, s, _ = xbc.shape
    gw = SSM_GROUP_WIDTH
    b_off = SSM_INNER // LANES
    c_off = b_off + SSM_GROUPS
    nchunk = s // SSM_CHUNK

    def seq(width, off):
        return pl.BlockSpec((1, s, width), lambda bi, g: (bi, 0, off + g))

    def par(rows, width, off):
        return pl.BlockSpec((1, rows, width), lambda bi, g: (0, 0, off + g))

    return pl.pallas_call(
        _ssd_kernel,
        grid=(b, SSM_GROUPS),
        in_specs=[seq(gw, 0), seq(LANES, b_off), seq(LANES, c_off), seq(LANES, 0),
                  par(SSM_CONV, gw, 0), par(SSM_CONV, LANES, b_off), par(SSM_CONV, LANES, c_off),
                  par(1, gw, 0), par(1, LANES, b_off), par(1, LANES, c_off),
                  pl.BlockSpec((1, 1, LANES), lambda bi, g: (g, 0, 0)),
                  pl.BlockSpec((1, 1, LANES), lambda bi, g: (g, 0, 0)),
                  pl.BlockSpec((1, 1, gw), lambda bi, g: (g, 0, 0))],
        out_specs=pl.BlockSpec((1, s, gw), lambda bi, g: (bi, 0, g)),
        out_shape=jax.ShapeDtypeStruct((b, s, SSM_INNER), F32),
        scratch_shapes=[pltpu.VMEM((s, gw), F32),
                        pltpu.VMEM((nchunk, SSM_STATE, SSM_CHUNK), BF16),
                        pltpu.VMEM((s, LANES), BF16),
                        pltpu.VMEM((nchunk, SSM_CHUNK, SSM_CHUNK), F32),
                        pltpu.VMEM((nchunk, SSM_CHUNK, LANES), F32),
                        pltpu.VMEM((nchunk, SSM_DT_COPIES * 2 * SSM_HEADS_PER_GROUP, SSM_CHUNK), F32),
                        pltpu.VMEM((SSM_STATE, gw), F32)],
        compiler_params=_params("parallel", "parallel"),
        name="ssd",
    )(xbc, xbc, xbc, dtp, conv_w, conv_w, conv_w, conv_b, conv_b, conv_b, dt_bias, a_neg, d_skip)


def _tail_kernel(x_ref, ya_ref, yb_ref, yc_ref, z_ref, ua_ref, ub_ref, uc_ref, p_ref,
                 nw_ref, woa_ref, wob_ref, woc_ref, wout_ref, pg_ref, wpg_ref, wple_ref, gn_ref,
                 o_ref, *h_ref, final):
    def mm(a, w_ref):
        return jnp.dot(a, w_ref[...], preferred_element_type=F32)

    ya = mm(ya_ref[...], woa_ref)
    yb = mm(yb_ref[...], wob_ref)
    yc_in = _rms(yc_ref[...] * _silu(z_ref[...]), nw_ref[...]).astype(BF16)
    yc = mm(yc_in, woc_ref)
    merged = (_sigmoid(ua_ref[...]) * ya + _sigmoid(ub_ref[...]) * yb
              + _sigmoid(uc_ref[...]) * yc)
    x1 = x_ref[...] + mm(merged.astype(BF16), wout_ref)
    gate = jax.nn.sigmoid(mm(_rms(x1, pg_ref[...]).astype(BF16), wpg_ref))
    x2 = x1 + mm(p_ref[...].astype(BF16), wple_ref) * gate
    if final:
        o_ref[...] = _rms(x2, gn_ref[...])
    else:
        o_ref[...] = x2
        h_ref[0][...] = _rms(x2, gn_ref[...]).astype(BF16)


MISC_WIDTH = SSM_INNER + DIL_GROUP_WIDTH + 3 * D_MODEL
MISC_GB_LANE_BLOCK = SSM_INNER // LANES
MISC_U_BLOCK = (SSM_INNER + DIL_GROUP_WIDTH) // D_MODEL


def _tail(x2d, ya, yb, yc, misc, p2d, nw, woa, wob, woc, wout, pg, wpg, wple, gn, final, tm=512):
    m = x2d.shape[0]
    out_specs = pl.BlockSpec((tm, D_MODEL), lambda i: (i, 0))
    out_shape = jax.ShapeDtypeStruct((m, D_MODEL), F32)
    if not final:
        out_specs = [out_specs, pl.BlockSpec((tm, D_MODEL), lambda i: (i, 0))]
        out_shape = [out_shape, jax.ShapeDtypeStruct((m, D_MODEL), BF16)]

    def rows(width, off=0):
        return pl.BlockSpec((tm, width), lambda i: (i, off))

    def whole(arr):
        return pl.BlockSpec(arr.shape, lambda i: (0, 0), pipeline_mode=pl.Buffered(1))

    ub = MISC_U_BLOCK
    return pl.pallas_call(
        functools.partial(_tail_kernel, final=final),
        grid=(m // tm,),
        in_specs=[rows(D_MODEL), rows(NA_WIDTH), rows(DIL_GROUP_WIDTH), rows(SSM_INNER),
                  rows(SSM_INNER, 0), rows(D_MODEL, ub), rows(D_MODEL, ub + 1), rows(D_MODEL, ub + 2),
                  rows(PLE_DIM),
                  whole(nw), whole(woa), whole(wob), whole(woc), whole(wout), whole(pg), whole(wpg),
                  whole(wple), whole(gn)],
        out_specs=out_specs,
        out_shape=out_shape,
        compiler_params=_params("parallel"),
        name="tail",
    )(x2d, ya, yb, yc, misc, misc, misc, misc, p2d, nw, woa, wob, woc, wout, pg, wpg, wple, gn)


def _prep_weights(w_in, conv_w, conv_b, a_log, dt_bias, d_skip):
    depth = w_in.shape[0]
    offs = np.concatenate([[0], np.cumsum(IN_SPLITS)])
    w16 = w_in.astype(BF16)
    (qa, ka, va, ga, qb, kb, vb, gb, xbc, z, dtr, ua, ub, uc) = [
        w16[:, :, int(offs[i]):int(offs[i + 1])] for i in range(len(IN_SPLITS))]
    w_a = w16[:, :, :int(offs[4])]
    gw = DIL_GROUP_WIDTH
    w_b = [jnp.concatenate([_rotary_layout(qb[:, :, g * gw:(g + 1) * gw]),
                            _rotary_layout(kb[:, :, g * gw:(g + 1) * gw]),
                            vb[:, :, g * gw:(g + 1) * gw]], axis=2)
           for g in range(len(DIL_PAIRS))]
    w_misc = jnp.concatenate([z, gb, ua, ub, uc], axis=2)

    hpg = SSM_HEADS_PER_GROUP

    def per_group(t):
        lead = t.shape[:-2]
        t = t.reshape(lead + (2, SSM_GROUPS, hpg))
        t = jnp.moveaxis(t, -2, -3).reshape(lead + (SSM_GROUPS, 2 * hpg))
        t = jnp.tile(t, (1,) * (t.ndim - 1) + (SSM_DT_COPIES,))
        pad = [(0, 0)] * (t.ndim - 1) + [(0, LANES - 2 * hpg * SSM_DT_COPIES)]
        return jnp.pad(t, pad).reshape(lead + (SSM_GROUPS * LANES,))

    w_dt = per_group(dtr.reshape(depth, D_MODEL, 2, SSM_HEADS))
    w_xbc = xbc
    dtb = per_group(dt_bias.astype(F32)).reshape(depth, SSM_GROUPS, 1, LANES)
    a_neg = per_group(-jnp.exp(a_log.astype(F32))).reshape(depth, SSM_GROUPS, 1, LANES)
    dsk = jnp.repeat(d_skip.astype(F32), HEAD_DIM, axis=1).reshape(depth, SSM_GROUPS, 1, SSM_GROUP_WIDTH)
    cw = conv_w.astype(F32).reshape(depth, 1, SSM_CONV, SSM_CONV_DIM)
    cbias = conv_b.astype(F32).reshape(depth, 1, 1, SSM_CONV_DIM)
    return w_a, w_b, w_misc, w_xbc, w_dt, dtb, a_neg, dsk, cw, cbias


def kernel(x, p, norm_w, w_in, na_rpb, conv_w, conv_b, a_log, dt_bias, d_skip, ssm_norm_w,
           w_oa, w_ob, w_oc, w_out, ple_norm_w, w_ple, w_ple_gate, final_norm_w):
    b, s, dm = x.shape
    depth = w_in.shape[0]
    m = b * s
    w_a, w_b, w_misc, w_xbc, w_dt, dtb, a_neg, dsk, cw, cbias = _prep_weights(
        w_in, conv_w, conv_b, a_log, dt_bias, d_skip)
    tabs = [_rotary_tables(s, d) for _, d in DIL_PAIRS]
    row = lambda v: v.astype(F32).reshape(1, -1)

    x2d = x.reshape(m, dm)
    h = _norm(x2d, row(norm_w[0]))
    for i in range(depth):
        final = i == depth - 1
        qkvg = _proj(h, w_a[i], NA_WIDTH, BF16, first_tile_scale=SCORE_SCALE).reshape(b, s, 4 * NA_WIDTH)
        misc = _proj(h, w_misc[i], 1024, BF16)
        xbc = _proj(h, w_xbc[i], SSM_CONV_DIM // 2, BF16).reshape(b, s, SSM_CONV_DIM)
        dtp = _proj(h, w_dt[i], SSM_GROUPS * LANES, F32).reshape(b, s, SSM_GROUPS * LANES)
        qkvs = [_proj_dil(h.reshape(b, s // d, d * dm), w_b[gi][i], tabs[gi], d)
                for gi, (_, d) in enumerate(DIL_PAIRS)]

        ya = _na_attention(qkvg, _na_bias_table(na_rpb[i])).reshape(m, NA_WIDTH)
        yb = _dil_attention(qkvs, misc.reshape(b, s, MISC_WIDTH), MISC_GB_LANE_BLOCK)
        yb = yb.reshape(m, DIL_GROUP_WIDTH)
        yc = _ssd(xbc, dtp, cw[i], cbias[i], dtb[i], a_neg[i], dsk[i]).reshape(m, SSM_INNER)

        gn = row(final_norm_w) if final else row(norm_w[i + 1])
        out = _tail(x2d, ya, yb, yc, misc, p[i].reshape(m, PLE_DIM), row(ssm_norm_w[i]),
                    w_oa[i].astype(BF16), w_ob[i].astype(BF16), w_oc[i].astype(BF16),
                    w_out[i].astype(BF16), row(ple_norm_w[i]), w_ple_gate[i].astype(BF16),
                    w_ple[i].astype(BF16), gn, final=final)
        if final:
            x2d = out
        else:
            x2d, h = out
    return x2d.reshape(b, s, dm)
```

```python
import functools
import math

import numpy as np
import jax
import jax.numpy as jnp
from jax import lax
from jax.experimental import pallas as pl
from jax.experimental.pallas import tpu as pltpu

F32 = jnp.float32
BF16 = jnp.bfloat16

LANES = 128
VMEM_LIMIT_BYTES = 56 * 1024 * 1024

D_MODEL = 1024
GRID_W = 64
HEAD_DIM = 64
EPS = 1e-6
PLE_DIM = 256

NA_HEADS = 16
NA_WIDTH = NA_HEADS * HEAD_DIM
NA_WIN_ROWS = 8
NA_WIN_COLS = 16

DIL_PAIRS = ((128, 1), (512, 4), (2048, 16))
DIL_HEADS_PER_GROUP = 8
DIL_GROUP_WIDTH = DIL_HEADS_PER_GROUP * HEAD_DIM
DIL_WIDTH = DIL_GROUP_WIDTH * len(DIL_PAIRS)
DIL_BLK = 64
ROPE_THETA = 500000.0
ROPE_DIM = HEAD_DIM // 4
ROPE_HALF = ROPE_DIM // 2

SSM_INNER = 1536
SSM_HEADS = 24
SSM_GROUPS = 4
SSM_HEADS_PER_GROUP = SSM_HEADS // SSM_GROUPS
SSM_GROUP_WIDTH = SSM_HEADS_PER_GROUP * HEAD_DIM
SSM_STATE = 128
SSM_CONV = 5
SSM_CHUNK = 128
SSM_CONV_DIM = SSM_INNER + 2 * SSM_GROUPS * SSM_STATE

IN_SPLITS = (NA_WIDTH, NA_WIDTH, NA_WIDTH, NA_WIDTH,
             DIL_WIDTH, DIL_WIDTH, DIL_WIDTH, DIL_GROUP_WIDTH,
             SSM_CONV_DIM, SSM_INNER, 2 * SSM_HEADS,
             D_MODEL, D_MODEL, D_MODEL)

LOG2E = math.log2(math.e)
SCORE_SCALE = HEAD_DIM ** -0.5 * LOG2E


def _params(*semantics):
    return pltpu.CompilerParams(dimension_semantics=semantics,
                                vmem_limit_bytes=VMEM_LIMIT_BYTES)


def _rms(x, g):
    return x * lax.rsqrt(jnp.mean(x * x, axis=-1, keepdims=True) + EPS) * g


def _silu(x):
    x = x.astype(F32)
    return x * jax.nn.sigmoid(x)


def _sigmoid(x):
    return jax.nn.sigmoid(x.astype(F32))


def _norm_kernel(x_ref, g_ref, h_ref):
    h_ref[...] = _rms(x_ref[...], g_ref[...]).astype(h_ref.dtype)


def _norm(x2d, g, tm=1024):
    m, k = x2d.shape
    return pl.pallas_call(
        _norm_kernel,
        grid=(m // tm,),
        in_specs=[pl.BlockSpec((tm, k), lambda i: (i, 0)), pl.BlockSpec((1, k), lambda i: (0, 0))],
        out_specs=pl.BlockSpec((tm, k), lambda i: (i, 0)),
        out_shape=jax.ShapeDtypeStruct((m, k), BF16),
        compiler_params=_params("parallel"),
        name="norm",
    )(x2d, g)


def _proj_kernel(h_ref, w_ref, o_ref, *, lead_scale):
    acc = jnp.dot(h_ref[...], w_ref[...], preferred_element_type=F32)
    if lead_scale is None:
        o_ref[...] = acc.astype(o_ref.dtype)
    else:
        ncols, scale = lead_scale

        @pl.when(pl.program_id(1) == 0)
        def _():
            o_ref[:, :ncols] = (acc[:, :ncols] * scale).astype(o_ref.dtype)
            o_ref[:, ncols:] = acc[:, ncols:].astype(o_ref.dtype)

        @pl.when(pl.program_id(1) != 0)
        def _():
            o_ref[...] = acc.astype(o_ref.dtype)


def _proj(h2d, w, tn, out_dtype, tm=1024, lead_scale=None):
    m, k = h2d.shape
    n = w.shape[1]
    return pl.pallas_call(
        functools.partial(_proj_kernel, lead_scale=lead_scale),
        grid=(m // tm, n // tn),
        in_specs=[pl.BlockSpec((tm, k), lambda i, j: (i, 0)),
                  pl.BlockSpec((k, tn), lambda i, j: (0, j))],
        out_specs=pl.BlockSpec((tm, tn), lambda i, j: (i, j)),
        out_shape=jax.ShapeDtypeStruct((m, n), out_dtype),
        compiler_params=_params("parallel", "arbitrary"),
        name="proj",
    )(h2d, w)


MXU_N = 256


def _proj_dil_kernel(h_ref, w_ref, c_ref, s_ref, o_ref):
    nres, rows = o_ref.shape[1:3]
    k, n = w_ref.shape
    tm = nres * rows
    gw = DIL_GROUP_WIDTH
    hp = jnp.concatenate([h_ref[0, :, t * k:(t + 1) * k] for t in range(nres)], axis=0)
    for nb in range(n // MXU_N):
        lo = nb * MXU_N
        acc = jnp.dot(hp, w_ref[:, lo:lo + MXU_N], preferred_element_type=F32)
        part = lo // gw
        if part == 2:
            o_ref[0, :, :, lo:lo + MXU_N] = acc.astype(o_ref.dtype).reshape(nres, rows, MXU_N)
            continue
        c = c_ref[part].reshape(tm, LANES)
        sn = s_ref[part].reshape(tm, LANES)
        for cb in range(MXU_N // LANES):
            blk = acc[:, cb * LANES:(cb + 1) * LANES]
            rot = blk * c + pltpu.roll(blk, LANES // 2, 1) * sn
            o_ref[0, :, :, lo + cb * LANES:lo + (cb + 1) * LANES] = (
                rot.astype(o_ref.dtype).reshape(nres, rows, LANES))


def _proj_dil(hd, w, tabs, d, tm=1024):
    b, l, dk = hd.shape
    k = dk // d
    n = w.shape[1]
    rows = min(l, tm)
    nres = tm // rows
    c, sn = tabs
    tab_spec = pl.BlockSpec((2, nres, rows, LANES), lambda bi, r, i: (0, r, i, 0))
    return pl.pallas_call(
        _proj_dil_kernel,
        grid=(b, d // nres, l // rows),
        in_specs=[pl.BlockSpec((1, rows, nres * k), lambda bi, r, i: (bi, i, r)),
                  pl.BlockSpec((k, n), lambda bi, r, i: (0, 0)),
                  tab_spec, tab_spec],
        out_specs=pl.BlockSpec((1, nres, rows, n), lambda bi, r, i: (bi, r, i, 0)),
        out_shape=jax.ShapeDtypeStruct((b, d, l, n), BF16),
        compiler_params=_params("parallel", "parallel", "parallel"),
        name=f"proj_dil{d}",
    )(hd, w, c, sn)


def _rotary_lane_sources():
    half = LANES // 2
    head = np.zeros(LANES, np.int64)
    dim = np.zeros(LANES, np.int64)
    for lane in range(LANES):
        off = lane % half
        is_b = lane >= half
        if off < 2 * ROPE_HALF:
            head[lane] = off // ROPE_HALF
            dim[lane] = off % ROPE_HALF + (ROPE_HALF if is_b else 0)
        else:
            head[lane] = 1 if is_b else 0
            dim[lane] = off
    return head, dim


def _rotary_tables(s, d):
    inv = ROPE_THETA ** (-jnp.arange(0, ROPE_DIM, 2, dtype=F32) / ROPE_DIM)
    ang = jnp.arange(s).astype(F32)[:, None] * inv[None, :]
    cos, sin = jnp.cos(ang), jnp.sin(ang)
    _, dim = _rotary_lane_sources()
    lanes = np.arange(LANES)
    rotary = (lanes % (LANES // 2)) < 2 * ROPE_HALF
    is_b = lanes >= LANES // 2
    idx = dim % ROPE_HALF
    c = jnp.where(rotary[None, :], cos[:, idx], 1.0)
    sn = jnp.where(rotary[None, :], jnp.where(is_b[None, :], sin[:, idx], -sin[:, idx]), 0.0)

    def arrange(t):
        t = t.reshape(s // d, d, LANES).transpose(1, 0, 2)
        return jnp.stack([t * SCORE_SCALE, t])

    return arrange(c), arrange(sn)


def _rotary_layout(t):
    lead = t.shape[:-1]
    t = t.reshape(lead + (t.shape[-1] // LANES, 2, HEAD_DIM))
    npair = t.shape[-3]
    a = t[..., :, :, 0:ROPE_HALF].reshape(lead + (npair, 2 * ROPE_HALF))
    b = t[..., :, :, ROPE_HALF:ROPE_DIM].reshape(lead + (npair, 2 * ROPE_HALF))
    rest0 = t[..., :, 0, ROPE_DIM:]
    rest1 = t[..., :, 1, ROPE_DIM:]
    return jnp.concatenate([a, rest0, b, rest1], axis=-1).reshape(lead + (npair * LANES,))


NA_UNROLL = 8


def _na_kernel(q_ref, k_ref, v_ref, g_ref, tb_ref, o_ref):
    rows = q_ref.shape[1] // GRID_W
    nkeys = NA_WIN_ROWS * GRID_W
    lane = lax.broadcasted_iota(jnp.int32, (GRID_W, LANES), 1)
    first = lane < HEAD_DIM

    def body(step, carry):
        units = []
        for i in range(NA_UNROLL):
            r = step * NA_UNROLL + i
            r0 = jnp.clip(r - NA_WIN_ROWS // 2, 0, rows - NA_WIN_ROWS)
            st = r0 - r + NA_WIN_ROWS - 1
            qs = pl.ds(pl.multiple_of(r * GRID_W, GRID_W), GRID_W)
            ks = pl.ds(pl.multiple_of(r0 * GRID_W, GRID_W), nkeys)
            q = q_ref[0, qs, :]
            kw = k_ref[0, ks, :]
            zero = jnp.zeros_like(q)
            q2 = jnp.concatenate([jnp.where(first, q, zero), jnp.where(first, zero, q)], axis=0)
            sc = lax.dot_general(q2, kw, (((1,), (1,)), ((), ())),
                                 preferred_element_type=F32)
            units.append((qs, ks, sc + tb_ref[0, st]))
        probs = []
        for qs, ks, sc in units:
            mx = jnp.max(sc, axis=-1, keepdims=True)
            e = jnp.exp2(sc - mx)
            probs.append((e.astype(BF16), jnp.sum(e, axis=-1, keepdims=True)))
        for (qs, ks, _), (p, den) in zip(units, probs):
            o2 = jnp.dot(p, v_ref[0, ks, :], preferred_element_type=F32) / den
            o = jnp.where(first, o2[0:GRID_W], o2[GRID_W:])
            o_ref[0, qs, :] = (o * _silu(g_ref[0, qs, :])).astype(o_ref.dtype)
        return carry

    lax.fori_loop(0, rows // NA_UNROLL, body, 0)


def _na_bias_table(rpb):
    h = rpb.shape[0]
    cq = np.arange(GRID_W)[:, None]
    ck = np.arange(GRID_W)[None, :]
    ws = np.clip(cq - NA_WIN_COLS // 2, 0, GRID_W - NA_WIN_COLS)
    in_win = (ck >= ws) & (ck < ws + NA_WIN_COLS)
    line = 2 * GRID_W
    lo = GRID_W - NA_WIN_COLS
    ext = jnp.pad(rpb.astype(F32) * LOG2E, ((0, 0), (0, 0), (lo, line - lo - rpb.shape[-1])))
    skew = jnp.broadcast_to(ext[:, :, None, :], ext.shape[:2] + (GRID_W, line))
    skew = skew.reshape(ext.shape[:2] + (GRID_W * line,))[:, :, :GRID_W * (line - 1)]
    toep = skew.reshape(ext.shape[:2] + (GRID_W, line - 1))[:, :, :, GRID_W - 1:]
    colb = jnp.where(in_win, toep, -jnp.inf)
    tb = jnp.stack([colb[:, st:st + NA_WIN_ROWS] for st in range(NA_WIN_ROWS)], axis=1)
    tb = tb.reshape(h // 2, 2, NA_WIN_ROWS, NA_WIN_ROWS, GRID_W, GRID_W)
    tb = tb.transpose(0, 2, 1, 4, 3, 5).reshape(h // 2, NA_WIN_ROWS, 2 * GRID_W, NA_WIN_ROWS * GRID_W)
    return tb


def _na_attention(qkvg, tb):
    b, s, _ = qkvg.shape
    npair = NA_WIDTH // LANES

    def col(off):
        return pl.BlockSpec((1, s, LANES), lambda bi, hp: (bi, 0, off + hp))

    return pl.pallas_call(
        _na_kernel,
        grid=(b, npair),
        in_specs=[col(0), col(npair), col(2 * npair), col(3 * npair),
                  pl.BlockSpec((1,) + tb.shape[1:], lambda bi, hp: (hp, 0, 0, 0))],
        out_specs=pl.BlockSpec((1, s, LANES), lambda bi, hp: (bi, 0, hp)),
        out_shape=jax.ShapeDtypeStruct((b, s, NA_WIDTH), BF16),
        compiler_params=_params("parallel", "parallel"),
        name="na_attention",
    )(qkvg, qkvg, qkvg, qkvg, tb)


DIL_QB = 2 * DIL_BLK
DIL_KB = 4 * DIL_BLK
DIL_UNROLL = 4


def _dil_kernel(q0, k0, v0, q1, k1, v1, q2, k2, v2, g_ref, o_ref, acc_ref, m_ref, w_ref):
    s = o_ref.shape[1]
    lane = lax.broadcasted_iota(jnp.int32, (DIL_QB, LANES), 1)
    first = lane < HEAD_DIM
    half_off = lane % (LANES // 2)
    q_first = (half_off < ROPE_HALF) | ((half_off >= 2 * ROPE_HALF) & (lane < LANES // 2))
    v_first = lax.broadcasted_iota(jnp.int32, (DIL_KB, LANES), 1) < HEAD_DIM
    qi = lax.broadcasted_iota(jnp.int32, (DIL_QB, DIL_KB), 0)
    ki = lax.broadcasted_iota(jnp.int32, (DIL_QB, DIL_KB), 1)

    for g, (q_ref, k_ref, v_ref) in enumerate(((q0, k0, v0), (q1, k1, v1), (q2, k2, v2))):
        d = DIL_PAIRS[g][1]
        l = s // d
        nu = l // DIL_QB

        def trip(step, carry, g=g, d=d, l=l, nu=nu, q_ref=q_ref, k_ref=k_ref, v_ref=v_ref):
            units = []
            for i in range(DIL_UNROLL):
                idx = step * DIL_UNROLL + i
                rho = idx // nu
                u = idx % nu
                q_start = pl.multiple_of(u * DIL_QB, DIL_QB)
                k_start = pl.multiple_of(jnp.clip(u * DIL_QB - DIL_BLK, 0, l - DIL_KB), DIL_BLK)
                q = q_ref[0, rho, pl.ds(q_start, DIL_QB), :]
                kw = k_ref[0, rho, pl.ds(k_start, DIL_KB), :]
                valid = jnp.abs((ki + k_start) - (qi + q_start)) <= DIL_BLK
                zero = jnp.zeros_like(q)
                scs = []
                for h in range(2):
                    qh = jnp.where(q_first, q, zero) if h == 0 else jnp.where(q_first, zero, q)
                    sc = lax.dot_general(qh, kw, (((1,), (1,)), ((), ())),
                                         preferred_element_type=F32)
                    scs.append(jnp.where(valid, sc, -jnp.inf))
                units.append((rho, q_start, k_start, scs))
            soft = []
            for rho, q_start, k_start, scs in units:
                es, mxs = [], []
                for sc in scs:
                    mx = jnp.max(sc, axis=-1, keepdims=True)
                    es.append(jnp.exp2(sc - mx).astype(BF16))
                    mxs.append(mx)
                soft.append((es, jnp.where(first, mxs[0], mxs[1])))
            for (rho, q_start, k_start, _), (es, mx) in zip(units, soft):
                vw = v_ref[0, rho, pl.ds(k_start, DIL_KB), :]
                one = jnp.ones_like(vw)
                o0 = jnp.dot(es[0], jnp.where(v_first, vw, one), preferred_element_type=F32)
                o1 = jnp.dot(es[1], jnp.where(v_first, one, vw), preferred_element_type=F32)
                num = jnp.where(first, o0, o1)
                den = pltpu.roll(jnp.where(first, o1, o0), HEAD_DIM, 1)
                if d == 1:
                    rows = pl.ds(q_start, DIL_QB)
                else:
                    rows = pl.ds(rho + d * q_start, DIL_QB, stride=d)
                if g == 0:
                    acc_ref[rows, :] = num
                    m_ref[rows, :] = mx
                    w_ref[rows, :] = den
                else:
                    m_old = m_ref[rows, :]
                    m_new = jnp.maximum(m_old, mx)
                    a = jnp.exp2(m_old - m_new)
                    bw = jnp.exp2(mx - m_new)
                    acc_ref[rows, :] = acc_ref[rows, :] * a + num * bw
                    w_ref[rows, :] = w_ref[rows, :] * a + den * bw
                    m_ref[rows, :] = m_new
            return carry

        lax.fori_loop(0, d * nu // DIL_UNROLL, trip, 0)

    def finish(i, carry):
        rows = pl.ds(pl.multiple_of(i * 256, 256), 256)
        y = acc_ref[rows, :] / w_ref[rows, :]
        o_ref[0, rows, :] = (y * _silu(g_ref[0, rows, :])).astype(o_ref.dtype)
        return carry

    lax.fori_loop(0, s // 256, finish, 0)


def _dil_attention(qkvs, gate, gate_off):
    b = gate.shape[0]
    s = gate.shape[1]
    npair = DIL_GROUP_WIDTH // LANES
    in_specs = []
    args = []
    for arr in qkvs:
        d, l = arr.shape[1], arr.shape[2]
        for part in range(3):
            in_specs.append(pl.BlockSpec((1, d, l, LANES),
                                         lambda bi, sp, part=part: (bi, 0, 0, part * npair + sp)))
            args.append(arr)
    in_specs.append(pl.BlockSpec((1, s, LANES), lambda bi, sp: (bi, 0, gate_off + sp)))
    args.append(gate)
    return pl.pallas_call(
        _dil_kernel,
        grid=(b, npair),
        in_specs=in_specs,
        out_specs=pl.BlockSpec((1, s, LANES), lambda bi, sp: (bi, 0, sp)),
        out_shape=jax.ShapeDtypeStruct((b, s, DIL_GROUP_WIDTH), BF16),
        scratch_shapes=[pltpu.VMEM((s, LANES), F32)] * 3,
        compiler_params=_params("parallel", "parallel"),
        name="dil_attention",
    )(*args)


def _split3(x):
    hi = x.astype(BF16)
    r1 = x - hi.astype(F32)
    mid = r1.astype(BF16)
    lo = (r1 - mid.astype(F32)).astype(BF16)
    return hi, mid, lo


SSM_HALO = 16
SSM_DT_COPIES = 2
SSM_UNROLL = 2


def _ssd_kernel(xs_ref, bm_ref, cm_ref, dt_ref, wx_ref, wb_ref, wc_ref, bx_ref, bb_ref, bc_ref,
                dtb_ref, a_ref, dsk_ref, y_ref, xs_s, bt_s, c_s, cb_s, acum_s, pt_s, st_s):
    s = xs_ref.shape[1]
    cl = SSM_CHUNK
    nchunk = s // cl
    hpg = SSM_HEADS_PER_GROUP
    npair = SSM_GROUP_WIDTH // LANES
    ncol = 2 * hpg

    def conv_chunk(c, carry):
        base = pl.multiple_of(c * cl, cl)
        prev_start = pl.multiple_of(jnp.maximum(base - SSM_HALO, 0), SSM_HALO)
        next_start = pl.multiple_of(jnp.minimum(base + cl, s - SSM_HALO), SSM_HALO)
        has_prev = jnp.where(c > 0, 1.0, 0.0)
        has_next = jnp.where(c < nchunk - 1, 1.0, 0.0)

        def conv(src, w_ref, bias_ref):
            ext = jnp.concatenate([src[0, pl.ds(prev_start, SSM_HALO), :].astype(F32) * has_prev,
                                   src[0, pl.ds(base, cl), :].astype(F32),
                                   src[0, pl.ds(next_start, SSM_HALO), :].astype(F32) * has_next], axis=0)
            acc = bias_ref[0]
            off = SSM_HALO - SSM_CONV // 2
            for tap in range(SSM_CONV):
                acc = acc + ext[off + tap:off + tap + cl, :] * w_ref[0, tap:tap + 1, :]
            return _silu(acc)

        xs_s[pl.ds(base, cl), :] = conv(xs_ref, wx_ref, bx_ref)
        bmat = conv(bm_ref, wb_ref, bb_ref).astype(BF16)
        cmat = conv(cm_ref, wc_ref, bc_ref).astype(BF16)
        bt_s[c] = bmat.astype(F32).T.astype(BF16)
        c_s[pl.ds(base, cl), :] = cmat
        cb_s[c] = lax.dot_general(cmat, bmat, (((1,), (1,)), ((), ())),
                                  preferred_element_type=F32)

        dt = jax.nn.softplus(dt_ref[0, pl.ds(base, cl), :] + dt_bias)
        pieces = _split3(dt * a_neg)
        acum_f = sum(jnp.dot(tril, p, preferred_element_type=F32) for p in pieces)
        acum_b = sum(jnp.dot(triu, p, preferred_element_type=F32) for p in pieces)
        acum = jnp.where(fwd_lane, acum_f, acum_b)
        a_end = jnp.where(fwd_lane[0:1], acum[cl - 1:cl, :], acum[0:1, :])
        w_state = dt * jnp.exp(a_end - acum)
        packed_t = jnp.where(lane < ncol, acum - jnp.log(dt), w_state).T
        acum_s[c] = acum
        pt_s[c] = packed_t[0:SSM_DT_COPIES * ncol, :]
        return carry

    ri = lax.broadcasted_iota(jnp.int32, (cl, cl), 0)
    ci = lax.broadcasted_iota(jnp.int32, (cl, cl), 1)
    lane = lax.broadcasted_iota(jnp.int32, (cl, LANES), 1)
    first = lane < HEAD_DIM
    fwd_lane = (lane % ncol) < hpg
    tril = jnp.where(ci <= ri, 1.0, 0.0).astype(BF16)
    triu = jnp.where(ci >= ri, 1.0, 0.0).astype(BF16)
    dt_bias = dtb_ref[0]
    a_neg = a_ref[0]

    lax.fori_loop(0, nchunk, conv_chunk, 0)

    for direction in range(2):
        if direction == 0:
            keep = ci <= ri
            end_row = cl - 1
        else:
            keep = ci >= ri
            end_row = 0

        st_s[...] = jnp.zeros_like(st_s)

        def local_part(c, direction=direction, keep=keep, end_row=end_row):
            rows = pl.ds(pl.multiple_of(c * cl, cl), cl)
            acum = acum_s[c]
            packed_t = pt_s[c]

            xs = xs_s[rows, :]
            bt = bt_s[c].astype(F32)
            cb = cb_s[c]
            yd_blocks, new_blocks, scale_blocks = [], [], []
            for pr in range(npair):
                xblk = xs[:, pr * LANES:(pr + 1) * LANES].astype(BF16)
                lhs, cols = [], []
                for j in (direction * hpg + 2 * pr, direction * hpg + 2 * pr + 1):
                    col = jnp.broadcast_to(acum[:, j:j + 1], (cl, cl))
                    row = packed_t[j:j + 1, :]
                    ws_row = packed_t[ncol + j:ncol + j + 1, :]
                    seg_dt = jnp.exp(jnp.where(keep, col - row, -jnp.inf))
                    lhs.append((cb * seg_dt).astype(BF16))
                    lhs.append((bt * ws_row).astype(BF16))
                    cols.append(col)
                prod = jnp.dot(jnp.concatenate(lhs, axis=0), xblk, preferred_element_type=F32)
                yd_blocks.append(jnp.where(first, prod[0:cl], prod[2 * cl:3 * cl]))
                new_blocks.append(jnp.where(first, prod[cl:2 * cl], prod[3 * cl:4 * cl]))
                scale_blocks.append(jnp.exp(jnp.where(first, cols[0], cols[1])))
            return rows, xs, yd_blocks, new_blocks, scale_blocks

        def trip(step, carry, direction=direction, end_row=end_row):
            chunks = [step * SSM_UNROLL + i for i in range(SSM_UNROLL)]
            if direction == 1:
                chunks = [nchunk - 1 - c for c in chunks]
            parts = [local_part(c) for c in chunks]
            state = [st_s[:, pr * LANES:(pr + 1) * LANES] for pr in range(npair)]
            for rows, xs, yd_blocks, new_blocks, scale_blocks in parts:
                cmat = c_s[rows, :]
                y_blocks = []
                for pr in range(npair):
                    y_off = jnp.dot(cmat, state[pr].astype(BF16), preferred_element_type=F32)
                    y_blocks.append(yd_blocks[pr] + y_off * scale_blocks[pr])
                    state[pr] = (state[pr] * scale_blocks[pr][end_row:end_row + 1, :]
                                 + new_blocks[pr])
                y = jnp.concatenate(y_blocks, axis=1)
                if direction == 0:
                    y_ref[0, rows, :] = y + dsk_ref[0] * xs
                else:
                    y_ref[0, rows, :] = y_ref[0, rows, :] + y
            for pr in range(npair):
                st_s[:, pr * LANES:(pr + 1) * LANES] = state[pr]
            return carry

        lax.fori_loop(0, nchunk // SSM_UNROLL, trip, 0)


def _ssd(xbc, dtp, conv_w, conv_b, dt_bias, a_neg, d_skip):
    b, s, _ = xbc.shape
    gw = SSM_GROUP_WIDTH
    b_off = SSM_INNER // LANES
    c_off = b_off + SSM_GROUPS
    nchunk = s // SSM_CHUNK

    def seq(width, off):
        return pl.BlockSpec((1, s, width), lambda bi, g: (bi, 0, off + g))

    def par(rows, width, off):
        return pl.BlockSpec((1, rows, width), lambda bi, g: (0, 0, off + g))

    return pl.pallas_call(
        _ssd_kernel,
        grid=(b, SSM_GROUPS),
        in_specs=[seq(gw, 0), seq(LANES, b_off), seq(LANES, c_off), seq(LANES, 0),
                  par(SSM_CONV, gw, 0), par(SSM_CONV, LANES, b_off), par(SSM_CONV, LANES, c_off),
                  par(1, gw, 0), par(1, LANES, b_off), par(1, LANES, c_off),
                  pl.BlockSpec((1, 1, LANES), lambda bi, g: (g, 0, 0)),
                  pl.BlockSpec((1, 1, LANES), lambda bi, g: (g, 0, 0)),
                  pl.BlockSpec((1, 1, gw), lambda bi, g: (g, 0, 0))],
        out_specs=pl.BlockSpec((1, s, gw), lambda bi, g: (bi, 0, g)),
        out_shape=jax.ShapeDtypeStruct((b, s, SSM_INNER), F32),
        scratch_shapes=[pltpu.VMEM((s, gw), F32),
                        pltpu.VMEM((nchunk, SSM_STATE, SSM_CHUNK), BF16),
                        pltpu.VMEM((s, LANES), BF16),
                        pltpu.VMEM((nchunk, SSM_CHUNK, SSM_CHUNK), F32),
                        pltpu.VMEM((nchunk, SSM_CHUNK, LANES), F32),
                        pltpu.VMEM((nchunk, SSM_DT_COPIES * 2 * SSM_HEADS_PER_GROUP, SSM_CHUNK), F32),
                        pltpu.VMEM((SSM_STATE, gw), F32)],
        compiler_params=_params("parallel", "parallel"),
        name="ssd",
    )(xbc, xbc, xbc, dtp, conv_w, conv_w, conv_w, conv_b, conv_b, conv_b, dt_bias, a_neg, d_skip)


def _tail_kernel(x_ref, ya_ref, yb_ref, yc_ref, z_ref, ua_ref, ub_ref, uc_ref, p_ref,
                 nw_ref, woa_ref, wob_ref, woc_ref, wout_ref, pg_ref, wpg_ref, wple_ref, gn_ref,
                 o_ref, *h_ref, final):
    def mm(a, w_ref):
        return jnp.dot(a, w_ref[...], preferred_element_type=F32)

    ya = mm(ya_ref[...], woa_ref)
    yb = mm(yb_ref[...], wob_ref)
    yc_in = _rms(yc_ref[...] * _silu(z_ref[...]), nw_ref[...]).astype(BF16)
    yc = mm(yc_in, woc_ref)
    merged = (_sigmoid(ua_ref[...]) * ya + _sigmoid(ub_ref[...]) * yb
              + _sigmoid(uc_ref[...]) * yc)
    x1 = x_ref[...] + mm(merged.astype(BF16), wout_ref)
    gate = jax.nn.sigmoid(mm(_rms(x1, pg_ref[...]).astype(BF16), wpg_ref))
    x2 = x1 + mm(p_ref[...].astype(BF16), wple_ref) * gate
    if final:
        o_ref[...] = _rms(x2, gn_ref[...])
    else:
        o_ref[...] = x2
        h_ref[0][...] = _rms(x2, gn_ref[...]).astype(BF16)


MISC_WIDTH = SSM_INNER + DIL_GROUP_WIDTH + 3 * D_MODEL
MISC_GB_LANE_BLOCK = SSM_INNER // LANES
MISC_U_BLOCK = (SSM_INNER + DIL_GROUP_WIDTH) // D_MODEL


def _tail(x2d, ya, yb, yc, misc, p2d, nw, woa, wob, woc, wout, pg, wpg, wple, gn, final, tm=512):
    m = x2d.shape[0]
    out_specs = pl.BlockSpec((tm, D_MODEL), lambda i: (i, 0))
    out_shape = jax.ShapeDtypeStruct((m, D_MODEL), F32)
    if not final:
        out_specs = [out_specs, pl.BlockSpec((tm, D_MODEL), lambda i: (i, 0))]
        out_shape = [out_shape, jax.ShapeDtypeStruct((m, D_MODEL), BF16)]

    def rows(width, off=0):
        return pl.BlockSpec((tm, width), lambda i: (i, off))

    def whole(arr):
        return pl.BlockSpec(arr.shape, lambda i: (0, 0), pipeline_mode=pl.Buffered(1))

    ub = MISC_U_BLOCK
    return pl.pallas_call(
        functools.partial(_tail_kernel, final=final),
        grid=(m // tm,),
        in_specs=[rows(D_MODEL), rows(NA_WIDTH), rows(DIL_GROUP_WIDTH), rows(SSM_INNER),
                  rows(SSM_INNER, 0), rows(D_MODEL, ub), rows(D_MODEL, ub + 1), rows(D_MODEL, ub + 2),
                  rows(PLE_DIM),
                  whole(nw), whole(woa), whole(wob), whole(woc), whole(wout), whole(pg), whole(wpg),
                  whole(wple), whole(gn)],
        out_specs=out_specs,
        out_shape=out_shape,
        compiler_params=_params("parallel"),
        name="tail",
    )(x2d, ya, yb, yc, misc, misc, misc, misc, p2d, nw, woa, wob, woc, wout, pg, wpg, wple, gn)


def _prep_weights(w_in, conv_w, conv_b, a_log, dt_bias, d_skip):
    depth = w_in.shape[0]
    offs = np.concatenate([[0], np.cumsum(IN_SPLITS)])
    w16 = w_in.astype(BF16)
    (qa, ka, va, ga, qb, kb, vb, gb, xbc, z, dtr, ua, ub, uc) = [
        w16[:, :, int(offs[i]):int(offs[i + 1])] for i in range(len(IN_SPLITS))]
    w_a = w16[:, :, :int(offs[4])]
    gw = DIL_GROUP_WIDTH
    w_b = [jnp.concatenate([_rotary_layout(qb[:, :, g * gw:(g + 1) * gw]),
                            _rotary_layout(kb[:, :, g * gw:(g + 1) * gw]),
                            vb[:, :, g * gw:(g + 1) * gw]], axis=2)
           for g in range(len(DIL_PAIRS))]
    w_misc = jnp.concatenate([z, gb, ua, ub, uc], axis=2)

    hpg = SSM_HEADS_PER_GROUP

    def per_group(t):
        lead = t.shape[:-2]
        t = t.reshape(lead + (2, SSM_GROUPS, hpg))
        t = jnp.moveaxis(t, -2, -3).reshape(lead + (SSM_GROUPS, 2 * hpg))
        t = jnp.tile(t, (1,) * (t.ndim - 1) + (SSM_DT_COPIES,))
        pad = [(0, 0)] * (t.ndim - 1) + [(0, LANES - 2 * hpg * SSM_DT_COPIES)]
        return jnp.pad(t, pad).reshape(lead + (SSM_GROUPS * LANES,))

    w_dt = per_group(dtr.reshape(depth, D_MODEL, 2, SSM_HEADS))
    w_xbc = xbc
    dtb = per_group(dt_bias.astype(F32)).reshape(depth, SSM_GROUPS, 1, LANES)
    a_neg = per_group(-jnp.exp(a_log.astype(F32))).reshape(depth, SSM_GROUPS, 1, LANES)
    dsk = jnp.repeat(d_skip.astype(F32), HEAD_DIM, axis=1).reshape(depth, SSM_GROUPS, 1, SSM_GROUP_WIDTH)
    cw = conv_w.astype(F32).reshape(depth, 1, SSM_CONV, SSM_CONV_DIM)
    cbias = conv_b.astype(F32).reshape(depth, 1, 1, SSM_CONV_DIM)
    return w_a, w_b, w_misc, w_xbc, w_dt, dtb, a_neg, dsk, cw, cbias


def kernel(x, p, norm_w, w_in, na_rpb, conv_w, conv_b, a_log, dt_bias, d_skip, ssm_norm_w,
           w_oa, w_ob, w_oc, w_out, ple_norm_w, w_ple, w_ple_gate, final_norm_w):
    b, s, dm = x.shape
    depth = w_in.shape[0]
    m = b * s
    w_a, w_b, w_misc, w_xbc, w_dt, dtb, a_neg, dsk, cw, cbias = _prep_weights(
        w_in, conv_w, conv_b, a_log, dt_bias, d_skip)
    tabs = [_rotary_tables(s, d) for _, d in DIL_PAIRS]
    row = lambda v: v.astype(F32).reshape(1, -1)

    x2d = x.reshape(m, dm)
    h = _norm(x2d, row(norm_w[0]))
    for i in range(depth):
        final = i == depth - 1
        qkvg = _proj(h, w_a[i], 2 * NA_WIDTH, BF16,
                     lead_scale=(NA_WIDTH, SCORE_SCALE)).reshape(b, s, 4 * NA_WIDTH)
        misc = _proj(h, w_misc[i], MISC_WIDTH // 2, BF16)
        xbc = _proj(h, w_xbc[i], SSM_CONV_DIM, BF16).reshape(b, s, SSM_CONV_DIM)
        dtp = _proj(h, w_dt[i], SSM_GROUPS * LANES, F32).reshape(b, s, SSM_GROUPS * LANES)
        qkvs = [_proj_dil(h.reshape(b, s // d, d * dm), w_b[gi][i], tabs[gi], d)
                for gi, (_, d) in enumerate(DIL_PAIRS)]

        ya = _na_attention(qkvg, _na_bias_table(na_rpb[i])).reshape(m, NA_WIDTH)
        yb = _dil_attention(qkvs, misc.reshape(b, s, MISC_WIDTH), MISC_GB_LANE_BLOCK)
        yb = yb.reshape(m, DIL_GROUP_WIDTH)
        yc = _ssd(xbc, dtp, cw[i], cbias[i], dtb[i], a_neg[i], dsk[i]).reshape(m, SSM_INNER)

        gn = row(final_norm_w) if final else row(norm_w[i + 1])
        out = _tail(x2d, ya, yb, yc, misc, p[i].reshape(m, PLE_DIM), row(ssm_norm_w[i]),
                    w_oa[i].astype(BF16), w_ob[i].astype(BF16), w_oc[i].astype(BF16),
                    w_out[i].astype(BF16), row(ple_norm_w[i]), w_ple_gate[i].astype(BF16),
                    w_ple[i].astype(BF16), gn, final=final)
        if final:
            x2d = out
        else:
            x2d, h = out
    return x2d.reshape(b, s, dm)
```

```python
import functools
import math

import numpy as np
import jax
import jax.numpy as jnp
from jax import lax
from jax.experimental import pallas as pl
from jax.experimental.pallas import tpu as pltpu

F32 = jnp.float32
BF16 = jnp.bfloat16

LANES = 128
VMEM_LIMIT_BYTES = 56 * 1024 * 1024

D_MODEL = 1024
GRID_W = 64
HEAD_DIM = 64
EPS = 1e-6
PLE_DIM = 256

NA_HEADS = 16
NA_WIDTH = NA_HEADS * HEAD_DIM
NA_WIN_ROWS = 8
NA_WIN_COLS = 16

DIL_PAIRS = ((128, 1), (512, 4), (2048, 16))
DIL_HEADS_PER_GROUP = 8
DIL_GROUP_WIDTH = DIL_HEADS_PER_GROUP * HEAD_DIM
DIL_WIDTH = DIL_GROUP_WIDTH * len(DIL_PAIRS)
DIL_BLK = 64
ROPE_THETA = 500000.0
ROPE_DIM = HEAD_DIM // 4
ROPE_HALF = ROPE_DIM // 2

SSM_INNER = 1536
SSM_HEADS = 24
SSM_GROUPS = 4
SSM_HEADS_PER_GROUP = SSM_HEADS // SSM_GROUPS
SSM_GROUP_WIDTH = SSM_HEADS_PER_GROUP * HEAD_DIM
SSM_STATE = 128
SSM_CONV = 5
SSM_CHUNK = 128
SSM_CONV_DIM = SSM_INNER + 2 * SSM_GROUPS * SSM_STATE

IN_SPLITS = (NA_WIDTH, NA_WIDTH, NA_WIDTH, NA_WIDTH,
             DIL_WIDTH, DIL_WIDTH, DIL_WIDTH, DIL_GROUP_WIDTH,
             SSM_CONV_DIM, SSM_INNER, 2 * SSM_HEADS,
             D_MODEL, D_MODEL, D_MODEL)

LOG2E = math.log2(math.e)
SCORE_SCALE = HEAD_DIM ** -0.5 * LOG2E


def _params(*semantics):
    return pltpu.CompilerParams(dimension_semantics=semantics,
                                vmem_limit_bytes=VMEM_LIMIT_BYTES)


def _rms(x, g):
    return x * lax.rsqrt(jnp.mean(x * x, axis=-1, keepdims=True) + EPS) * g


def _silu(x):
    x = x.astype(F32)
    return x * jax.nn.sigmoid(x)


def _sigmoid(x):
    return jax.nn.sigmoid(x.astype(F32))


def _norm_kernel(x_ref, g_ref, h_ref):
    h_ref[...] = _rms(x_ref[...], g_ref[...]).astype(h_ref.dtype)


def _norm(x2d, g, tm=1024):
    m, k = x2d.shape
    return pl.pallas_call(
        _norm_kernel,
        grid=(m // tm,),
        in_specs=[pl.BlockSpec((tm, k), lambda i: (i, 0)), pl.BlockSpec((1, k), lambda i: (0, 0))],
        out_specs=pl.BlockSpec((tm, k), lambda i: (i, 0)),
        out_shape=jax.ShapeDtypeStruct((m, k), BF16),
        compiler_params=_params("parallel"),
        name="norm",
    )(x2d, g)


def _proj_kernel(h_ref, w_ref, o_ref, *, lead_scale):
    acc = jnp.dot(h_ref[...], w_ref[...], preferred_element_type=F32)
    if lead_scale is None:
        o_ref[...] = acc.astype(o_ref.dtype)
    else:
        ncols, scale = lead_scale

        @pl.when(pl.program_id(1) == 0)
        def _():
            o_ref[:, :ncols] = (acc[:, :ncols] * scale).astype(o_ref.dtype)
            o_ref[:, ncols:] = acc[:, ncols:].astype(o_ref.dtype)

        @pl.when(pl.program_id(1) != 0)
        def _():
            o_ref[...] = acc.astype(o_ref.dtype)


def _proj(h2d, w, tn, out_dtype, tm=1024, lead_scale=None):
    m, k = h2d.shape
    n = w.shape[1]
    return pl.pallas_call(
        functools.partial(_proj_kernel, lead_scale=lead_scale),
        grid=(m // tm, n // tn),
        in_specs=[pl.BlockSpec((tm, k), lambda i, j: (i, 0)),
                  pl.BlockSpec((k, tn), lambda i, j: (0, j))],
        out_specs=pl.BlockSpec((tm, tn), lambda i, j: (i, j)),
        out_shape=jax.ShapeDtypeStruct((m, n), out_dtype),
        compiler_params=_params("parallel", "arbitrary"),
        name="proj",
    )(h2d, w)


MXU_N = 256


REGROUP_ROWS = 256


def _regroup_rows(h, d):
    tm = h.shape[0]
    if d == 1:
        return h
    per = REGROUP_ROWS // d
    r = lax.broadcasted_iota(jnp.int32, (REGROUP_ROWS, REGROUP_ROWS), 0)
    c = lax.broadcasted_iota(jnp.int32, (REGROUP_ROWS, REGROUP_ROWS), 1)
    perm = jnp.where(c == (r % per) * d + r // per, 1.0, 0.0).astype(BF16)
    nblk = tm // REGROUP_ROWS
    blocks = [jnp.dot(perm, h[i * REGROUP_ROWS:(i + 1) * REGROUP_ROWS],
                      preferred_element_type=F32).astype(BF16) for i in range(nblk)]
    return jnp.concatenate([blocks[i][rho * per:(rho + 1) * per]
                            for rho in range(d) for i in range(nblk)], axis=0)


def _proj_dil_kernel(h_ref, w_ref, c_ref, s_ref, o_ref):
    nres, rows = o_ref.shape[1:3]
    k, n = w_ref.shape
    tm = nres * rows
    gw = DIL_GROUP_WIDTH
    hp = _regroup_rows(h_ref[0], nres)
    for nb in range(n // MXU_N):
        lo = nb * MXU_N
        acc = jnp.dot(hp, w_ref[:, lo:lo + MXU_N], preferred_element_type=F32)
        part = lo // gw
        if part == 2:
            o_ref[0, :, :, lo:lo + MXU_N] = acc.astype(o_ref.dtype).reshape(nres, rows, MXU_N)
            continue
        c = c_ref[part].reshape(tm, LANES)
        sn = s_ref[part].reshape(tm, LANES)
        for cb in range(MXU_N // LANES):
            blk = acc[:, cb * LANES:(cb + 1) * LANES]
            rot = blk * c + pltpu.roll(blk, LANES // 2, 1) * sn
            o_ref[0, :, :, lo + cb * LANES:lo + (cb + 1) * LANES] = (
                rot.astype(o_ref.dtype).reshape(nres, rows, LANES))


def _proj_dil(h3, w, tabs, d, tm=1024):
    b, s, k = h3.shape
    n = w.shape[1]
    rows = tm // d
    c, sn = tabs
    tab_spec = pl.BlockSpec((2, d, rows, LANES), lambda bi, i: (0, 0, i, 0))
    return pl.pallas_call(
        _proj_dil_kernel,
        grid=(b, s // tm),
        in_specs=[pl.BlockSpec((1, tm, k), lambda bi, i: (bi, i, 0)),
                  pl.BlockSpec((k, n), lambda bi, i: (0, 0)),
                  tab_spec, tab_spec],
        out_specs=pl.BlockSpec((1, d, rows, n), lambda bi, i: (bi, 0, i, 0)),
        out_shape=jax.ShapeDtypeStruct((b, d, s // d, n), BF16),
        compiler_params=_params("parallel", "parallel"),
        name=f"proj_dil{d}",
    )(h3, w, c, sn)


def _rotary_lane_sources():
    half = LANES // 2
    head = np.zeros(LANES, np.int64)
    dim = np.zeros(LANES, np.int64)
    for lane in range(LANES):
        off = lane % half
        is_b = lane >= half
        if off < 2 * ROPE_HALF:
            head[lane] = off // ROPE_HALF
            dim[lane] = off % ROPE_HALF + (ROPE_HALF if is_b else 0)
        else:
            head[lane] = 1 if is_b else 0
            dim[lane] = off
    return head, dim


def _rotary_tables(s, d):
    inv = ROPE_THETA ** (-jnp.arange(0, ROPE_DIM, 2, dtype=F32) / ROPE_DIM)
    ang = jnp.arange(s).astype(F32)[:, None] * inv[None, :]
    cos, sin = jnp.cos(ang), jnp.sin(ang)
    _, dim = _rotary_lane_sources()
    lanes = np.arange(LANES)
    rotary = (lanes % (LANES // 2)) < 2 * ROPE_HALF
    is_b = lanes >= LANES // 2
    idx = dim % ROPE_HALF
    c = jnp.where(rotary[None, :], cos[:, idx], 1.0)
    sn = jnp.where(rotary[None, :], jnp.where(is_b[None, :], sin[:, idx], -sin[:, idx]), 0.0)

    def arrange(t):
        t = t.reshape(s // d, d, LANES).transpose(1, 0, 2)
        return jnp.stack([t * SCORE_SCALE, t])

    return arrange(c), arrange(sn)


def _rotary_layout(t):
    lead = t.shape[:-1]
    t = t.reshape(lead + (t.shape[-1] // LANES, 2, HEAD_DIM))
    npair = t.shape[-3]
    a = t[..., :, :, 0:ROPE_HALF].reshape(lead + (npair, 2 * ROPE_HALF))
    b = t[..., :, :, ROPE_HALF:ROPE_DIM].reshape(lead + (npair, 2 * ROPE_HALF))
    rest0 = t[..., :, 0, ROPE_DIM:]
    rest1 = t[..., :, 1, ROPE_DIM:]
    return jnp.concatenate([a, rest0, b, rest1], axis=-1).reshape(lead + (npair * LANES,))


NA_UNROLL = 8


def _na_kernel(q_ref, k_ref, v_ref, g_ref, tb_ref, o_ref):
    rows = q_ref.shape[1] // GRID_W
    nkeys = NA_WIN_ROWS * GRID_W
    lane = lax.broadcasted_iota(jnp.int32, (GRID_W, LANES), 1)
    first = lane < HEAD_DIM

    def body(step, carry):
        units = []
        for i in range(NA_UNROLL):
            r = step * NA_UNROLL + i
            r0 = jnp.clip(r - NA_WIN_ROWS // 2, 0, rows - NA_WIN_ROWS)
            st = r0 - r + NA_WIN_ROWS - 1
            qs = pl.ds(pl.multiple_of(r * GRID_W, GRID_W), GRID_W)
            ks = pl.ds(pl.multiple_of(r0 * GRID_W, GRID_W), nkeys)
            q = q_ref[0, qs, :]
            kw = k_ref[0, ks, :]
            zero = jnp.zeros_like(q)
            q2 = jnp.concatenate([jnp.where(first, q, zero), jnp.where(first, zero, q)], axis=0)
            sc = lax.dot_general(q2, kw, (((1,), (1,)), ((), ())),
                                 preferred_element_type=F32)
            units.append((qs, ks, sc + tb_ref[0, st]))
        probs = []
        for qs, ks, sc in units:
            mx = jnp.max(sc, axis=-1, keepdims=True)
            e = jnp.exp2(sc - mx)
            probs.append((e.astype(BF16), jnp.sum(e, axis=-1, keepdims=True)))
        for (qs, ks, _), (p, den) in zip(units, probs):
            o2 = jnp.dot(p, v_ref[0, ks, :], preferred_element_type=F32) / den
            o = jnp.where(first, o2[0:GRID_W], o2[GRID_W:])
            o_ref[0, qs, :] = (o * _silu(g_ref[0, qs, :])).astype(o_ref.dtype)
        return carry

    lax.fori_loop(0, rows // NA_UNROLL, body, 0)


def _na_bias_table(rpb):
    h = rpb.shape[0]
    cq = np.arange(GRID_W)[:, None]
    ck = np.arange(GRID_W)[None, :]
    ws = np.clip(cq - NA_WIN_COLS // 2, 0, GRID_W - NA_WIN_COLS)
    in_win = (ck >= ws) & (ck < ws + NA_WIN_COLS)
    line = 2 * GRID_W
    lo = GRID_W - NA_WIN_COLS
    ext = jnp.pad(rpb.astype(F32) * LOG2E, ((0, 0), (0, 0), (lo, line - lo - rpb.shape[-1])))
    skew = jnp.broadcast_to(ext[:, :, None, :], ext.shape[:2] + (GRID_W, line))
    skew = skew.reshape(ext.shape[:2] + (GRID_W * line,))[:, :, :GRID_W * (line - 1)]
    toep = skew.reshape(ext.shape[:2] + (GRID_W, line - 1))[:, :, :, GRID_W - 1:]
    colb = jnp.where(in_win, toep, -jnp.inf)
    tb = jnp.stack([colb[:, st:st + NA_WIN_ROWS] for st in range(NA_WIN_ROWS)], axis=1)
    tb = tb.reshape(h // 2, 2, NA_WIN_ROWS, NA_WIN_ROWS, GRID_W, GRID_W)
    tb = tb.transpose(0, 2, 1, 4, 3, 5).reshape(h // 2, NA_WIN_ROWS, 2 * GRID_W, NA_WIN_ROWS * GRID_W)
    return tb


def _na_attention(qkvg, tb):
    b, s, _ = qkvg.shape
    npair = NA_WIDTH // LANES

    def col(off):
        return pl.BlockSpec((1, s, LANES), lambda hp, bi: (bi, 0, off + hp))

    return pl.pallas_call(
        _na_kernel,
        grid=(npair, b),
        in_specs=[col(0), col(npair), col(2 * npair), col(3 * npair),
                  pl.BlockSpec((1,) + tb.shape[1:], lambda hp, bi: (hp, 0, 0, 0))],
        out_specs=pl.BlockSpec((1, s, LANES), lambda hp, bi: (bi, 0, hp)),
        out_shape=jax.ShapeDtypeStruct((b, s, NA_WIDTH), BF16),
        compiler_params=_params("parallel", "parallel"),
        name="na_attention",
    )(qkvg, qkvg, qkvg, qkvg, tb)


DIL_QB = 2 * DIL_BLK
DIL_KB = 4 * DIL_BLK
DIL_UNROLL = 4


def _dil_kernel(q0, k0, v0, q1, k1, v1, q2, k2, v2, g_ref, o_ref, acc_ref, m_ref, w_ref):
    s = o_ref.shape[1]
    lane = lax.broadcasted_iota(jnp.int32, (DIL_QB, LANES), 1)
    first = lane < HEAD_DIM
    half_off = lane % (LANES // 2)
    q_first = (half_off < ROPE_HALF) | ((half_off >= 2 * ROPE_HALF) & (lane < LANES // 2))
    v_first = lax.broadcasted_iota(jnp.int32, (DIL_KB, LANES), 1) < HEAD_DIM
    qi = lax.broadcasted_iota(jnp.int32, (DIL_QB, DIL_KB), 0)
    ki = lax.broadcasted_iota(jnp.int32, (DIL_QB, DIL_KB), 1)

    for g, (q_ref, k_ref, v_ref) in enumerate(((q0, k0, v0), (q1, k1, v1), (q2, k2, v2))):
        d = DIL_PAIRS[g][1]
        l = s // d
        nu = l // DIL_QB

        def trip(step, carry, g=g, d=d, l=l, nu=nu, q_ref=q_ref, k_ref=k_ref, v_ref=v_ref):
            units = []
            for i in range(DIL_UNROLL):
                idx = step * DIL_UNROLL + i
                rho = idx // nu
                u = idx % nu
                q_start = pl.multiple_of(u * DIL_QB, DIL_QB)
                k_start = pl.multiple_of(jnp.clip(u * DIL_QB - DIL_BLK, 0, l - DIL_KB), DIL_BLK)
                q = q_ref[0, rho, pl.ds(q_start, DIL_QB), :]
                kw = k_ref[0, rho, pl.ds(k_start, DIL_KB), :]
                valid = jnp.abs((ki + k_start) - (qi + q_start)) <= DIL_BLK
                zero = jnp.zeros_like(q)
                scs = []
                for h in range(2):
                    qh = jnp.where(q_first, q, zero) if h == 0 else jnp.where(q_first, zero, q)
                    sc = lax.dot_general(qh, kw, (((1,), (1,)), ((), ())),
                                         preferred_element_type=F32)
                    scs.append(jnp.where(valid, sc, -jnp.inf))
                units.append((rho, q_start, k_start, scs))
            soft = []
            for rho, q_start, k_start, scs in units:
                es, mxs = [], []
                for sc in scs:
                    mx = jnp.max(sc, axis=-1, keepdims=True)
                    es.append(jnp.exp2(sc - mx).astype(BF16))
                    mxs.append(mx)
                soft.append((es, jnp.where(first, mxs[0], mxs[1])))
            for (rho, q_start, k_start, _), (es, mx) in zip(units, soft):
                vw = v_ref[0, rho, pl.ds(k_start, DIL_KB), :]
                one = jnp.ones_like(vw)
                o0 = jnp.dot(es[0], jnp.where(v_first, vw, one), preferred_element_type=F32)
                o1 = jnp.dot(es[1], jnp.where(v_first, one, vw), preferred_element_type=F32)
                num = jnp.where(first, o0, o1)
                den = pltpu.roll(jnp.where(first, o1, o0), HEAD_DIM, 1)
                if d == 1:
                    rows = pl.ds(q_start, DIL_QB)
                else:
                    rows = pl.ds(rho + d * q_start, DIL_QB, stride=d)
                if g == 0:
                    acc_ref[rows, :] = num
                    m_ref[rows, :] = mx
                    w_ref[rows, :] = den
                else:
                    m_old = m_ref[rows, :]
                    m_new = jnp.maximum(m_old, mx)
                    a = jnp.exp2(m_old - m_new)
                    bw = jnp.exp2(mx - m_new)
                    acc_ref[rows, :] = acc_ref[rows, :] * a + num * bw
                    w_ref[rows, :] = w_ref[rows, :] * a + den * bw
                    m_ref[rows, :] = m_new
            return carry

        lax.fori_loop(0, d * nu // DIL_UNROLL, trip, 0)

    def finish(i, carry):
        rows = pl.ds(pl.multiple_of(i * 256, 256), 256)
        y = acc_ref[rows, :] / w_ref[rows, :]
        o_ref[0, rows, :] = (y * _silu(g_ref[0, rows, :])).astype(o_ref.dtype)
        return carry

    lax.fori_loop(0, s // 256, finish, 0)


def _dil_attention(qkvs, gate, gate_off):
    b = gate.shape[0]
    s = gate.shape[1]
    npair = DIL_GROUP_WIDTH // LANES
    in_specs = []
    args = []
    for arr in qkvs:
        d, l = arr.shape[1], arr.shape[2]
        for part in range(3):
            in_specs.append(pl.BlockSpec((1, d, l, LANES),
                                         lambda bi, sp, part=part: (bi, 0, 0, part * npair + sp)))
            args.append(arr)
    in_specs.append(pl.BlockSpec((1, s, LANES), lambda bi, sp: (bi, 0, gate_off + sp)))
    args.append(gate)
    return pl.pallas_call(
        _dil_kernel,
        grid=(b, npair),
        in_specs=in_specs,
        out_specs=pl.BlockSpec((1, s, LANES), lambda bi, sp: (bi, 0, sp)),
        out_shape=jax.ShapeDtypeStruct((b, s, DIL_GROUP_WIDTH), BF16),
        scratch_shapes=[pltpu.VMEM((s, LANES), F32)] * 3,
        compiler_params=_params("parallel", "parallel"),
        name="dil_attention",
    )(*args)


def _split3(x):
    hi = x.astype(BF16)
    r1 = x - hi.astype(F32)
    mid = r1.astype(BF16)
    lo = (r1 - mid.astype(F32)).astype(BF16)
    return hi, mid, lo


SSM_HALO = 16
SSM_DT_COPIES = 2
SSM_UNROLL = 4


def _ssd_kernel(xs_ref, bm_ref, cm_ref, dt_ref, wx_ref, wb_ref, wc_ref, bx_ref, bb_ref, bc_ref,
                dtb_ref, a_ref, dsk_ref, y_ref, xs_s, bt_s, c_s, cb_s, acum_s, pt_s, st_s):
    s = xs_ref.shape[1]
    cl = SSM_CHUNK
    nchunk = s // cl
    hpg = SSM_HEADS_PER_GROUP
    npair = SSM_GROUP_WIDTH // LANES
    ncol = 2 * hpg

    def conv_chunk(c, carry):
        base = pl.multiple_of(c * cl, cl)
        prev_start = pl.multiple_of(jnp.maximum(base - SSM_HALO, 0), SSM_HALO)
        next_start = pl.multiple_of(jnp.minimum(base + cl, s - SSM_HALO), SSM_HALO)
        has_prev = jnp.where(c > 0, 1.0, 0.0)
        has_next = jnp.where(c < nchunk - 1, 1.0, 0.0)

        def conv(src, w_ref, bias_ref):
            ext = jnp.concatenate([src[0, pl.ds(prev_start, SSM_HALO), :].astype(F32) * has_prev,
                                   src[0, pl.ds(base, cl), :].astype(F32),
                                   src[0, pl.ds(next_start, SSM_HALO), :].astype(F32) * has_next], axis=0)
            acc = bias_ref[0]
            off = SSM_HALO - SSM_CONV // 2
            for tap in range(SSM_CONV):
                acc = acc + ext[off + tap:off + tap + cl, :] * w_ref[0, tap:tap + 1, :]
            return _silu(acc)

        xs_s[pl.ds(base, cl), :] = conv(xs_ref, wx_ref, bx_ref)
        bmat = conv(bm_ref, wb_ref, bb_ref).astype(BF16)
        cmat = conv(cm_ref, wc_ref, bc_ref).astype(BF16)
        bt_s[c] = bmat.astype(F32).T.astype(BF16)
        c_s[pl.ds(base, cl), :] = cmat
        cb_s[c] = lax.dot_general(cmat, bmat, (((1,), (1,)), ((), ())),
                                  preferred_element_type=F32)

        dt = jax.nn.softplus(dt_ref[0, pl.ds(base, cl), :] + dt_bias)
        pieces = _split3(dt * a_neg)
        acum_f = sum(jnp.dot(tril, p, preferred_element_type=F32) for p in pieces)
        acum_b = sum(jnp.dot(triu, p, preferred_element_type=F32) for p in pieces)
        acum = jnp.where(fwd_lane, acum_f, acum_b)
        a_end = jnp.where(fwd_lane[0:1], acum[cl - 1:cl, :], acum[0:1, :])
        w_state = dt * jnp.exp(a_end - acum)
        packed_t = jnp.where(lane < ncol, acum - jnp.log(dt), w_state).T
        acum_s[c] = acum
        pt_s[c] = packed_t[0:SSM_DT_COPIES * ncol, :]
        return carry

    ri = lax.broadcasted_iota(jnp.int32, (cl, cl), 0)
    ci = lax.broadcasted_iota(jnp.int32, (cl, cl), 1)
    lane = lax.broadcasted_iota(jnp.int32, (cl, LANES), 1)
    first = lane < HEAD_DIM
    fwd_lane = (lane % ncol) < hpg
    tril = jnp.where(ci <= ri, 1.0, 0.0).astype(BF16)
    triu = jnp.where(ci >= ri, 1.0, 0.0).astype(BF16)
    dt_bias = dtb_ref[0]
    a_neg = a_ref[0]

    lax.fori_loop(0, nchunk, conv_chunk, 0, unroll=4)

    for direction in range(2):
        if direction == 0:
            keep = ci <= ri
            end_row = cl - 1
        else:
            keep = ci >= ri
            end_row = 0

        st_s[...] = jnp.zeros_like(st_s)

        def local_part(c, direction=direction, keep=keep, end_row=end_row):
            rows = pl.ds(pl.multiple_of(c * cl, cl), cl)
            acum = acum_s[c]
            packed_t = pt_s[c]

            xs = xs_s[rows, :]
            bt = bt_s[c].astype(F32)
            cb = cb_s[c]
            yd_blocks, new_blocks, scale_blocks = [], [], []
            for pr in range(npair):
                xblk = xs[:, pr * LANES:(pr + 1) * LANES].astype(BF16)
                lhs, cols = [], []
                for j in (direction * hpg + 2 * pr, direction * hpg + 2 * pr + 1):
                    col = jnp.broadcast_to(acum[:, j:j + 1], (cl, cl))
                    row = packed_t[j:j + 1, :]
                    ws_row = packed_t[ncol + j:ncol + j + 1, :]
                    seg_dt = jnp.exp(jnp.where(keep, col - row, -jnp.inf))
                    lhs.append((cb * seg_dt).astype(BF16))
                    lhs.append((bt * ws_row).astype(BF16))
                    cols.append(col)
                prod = jnp.dot(jnp.concatenate(lhs, axis=0), xblk, preferred_element_type=F32)
                yd_blocks.append(jnp.where(first, prod[0:cl], prod[2 * cl:3 * cl]))
                new_blocks.append(jnp.where(first, prod[cl:2 * cl], prod[3 * cl:4 * cl]))
                scale_blocks.append(jnp.exp(jnp.where(first, cols[0], cols[1])))
            return rows, xs, yd_blocks, new_blocks, scale_blocks

        def trip(step, carry, direction=direction, end_row=end_row):
            chunks = [step * SSM_UNROLL + i for i in range(SSM_UNROLL)]
            if direction == 1:
                chunks = [nchunk - 1 - c for c in chunks]
            parts = [local_part(c) for c in chunks]
            state = [st_s[:, pr * LANES:(pr + 1) * LANES] for pr in range(npair)]
            for rows, xs, yd_blocks, new_blocks, scale_blocks in parts:
                cmat = c_s[rows, :]
                y_blocks = []
                for pr in range(npair):
                    y_off = jnp.dot(cmat, state[pr].astype(BF16), preferred_element_type=F32)
                    y_blocks.append(yd_blocks[pr] + y_off * scale_blocks[pr])
                    state[pr] = (state[pr] * scale_blocks[pr][end_row:end_row + 1, :]
                                 + new_blocks[pr])
                y = jnp.concatenate(y_blocks, axis=1)
                if direction == 0:
                    y_ref[0, rows, :] = y + dsk_ref[0] * xs
                else:
                    y_ref[0, rows, :] = y_ref[0, rows, :] + y
            for pr in range(npair):
                st_s[:, pr * LANES:(pr + 1) * LANES] = state[pr]
            return carry

        lax.fori_loop(0, nchunk // SSM_UNROLL, trip, 0)


def _ssd(xbc, dtp, conv_w, conv_b, dt_bias, a_neg, d_skip):
    b, s, _ = xbc.shape
    gw = SSM_GROUP_WIDTH
    b_off = SSM_INNER // LANES
    c_off = b_off + SSM_GROUPS
    nchunk = s // SSM_CHUNK

    def seq(width, off):
        return pl.BlockSpec((1, s, width), lambda bi, g: (bi, 0, off + g))

    def par(rows, width, off):
        return pl.BlockSpec((1, rows, width), lambda bi, g: (0, 0, off + g))

    return pl.pallas_call(
        _ssd_kernel,
        grid=(b, SSM_GROUPS),
        in_specs=[seq(gw, 0), seq(LANES, b_off), seq(LANES, c_off), seq(LANES, 0),
                  par(SSM_CONV, gw, 0), par(SSM_CONV, LANES, b_off), par(SSM_CONV, LANES, c_off),
                  par(1, gw, 0), par(1, LANES, b_off), par(1, LANES, c_off),
                  pl.BlockSpec((1, 1, LANES), lambda bi, g: (g, 0, 0)),
                  pl.BlockSpec((1, 1, LANES), lambda bi, g: (g, 0, 0)),
                  pl.BlockSpec((1, 1, gw), lambda bi, g: (g, 0, 0))],
        out_specs=pl.BlockSpec((1, s, gw), lambda bi, g: (bi, 0, g)),
        out_shape=jax.ShapeDtypeStruct((b, s, SSM_INNER), F32),
        scratch_shapes=[pltpu.VMEM((s, gw), F32),
                        pltpu.VMEM((nchunk, SSM_STATE, SSM_CHUNK), BF16),
                        pltpu.VMEM((s, LANES), BF16),
                        pltpu.VMEM((nchunk, SSM_CHUNK, SSM_CHUNK), F32),
                        pltpu.VMEM((nchunk, SSM_CHUNK, LANES), F32),
                        pltpu.VMEM((nchunk, SSM_DT_COPIES * 2 * SSM_HEADS_PER_GROUP, SSM_CHUNK), F32),
                        pltpu.VMEM((SSM_STATE, gw), F32)],
        compiler_params=_params("parallel", "parallel"),
        name="ssd",
    )(xbc, xbc, xbc, dtp, conv_w, conv_w, conv_w, conv_b, conv_b, conv_b, dt_bias, a_neg, d_skip)


def _tail_kernel(x_ref, ya_ref, yb_ref, yc_ref, z_ref, ua_ref, ub_ref, uc_ref, p_ref,
                 nw_ref, woa_ref, wob_ref, woc_ref, wout_ref, pg_ref, wpg_ref, wple_ref, gn_ref,
                 o_ref, *h_ref, final):
    def mm(a, w_ref):
        return jnp.dot(a, w_ref[...], preferred_element_type=F32)

    ya = mm(ya_ref[...], woa_ref)
    yb = mm(yb_ref[...], wob_ref)
    yc_in = _rms(yc_ref[...] * _silu(z_ref[...]), nw_ref[...]).astype(BF16)
    yc = mm(yc_in, woc_ref)
    merged = (_sigmoid(ua_ref[...]) * ya + _sigmoid(ub_ref[...]) * yb
              + _sigmoid(uc_ref[...]) * yc)
    x1 = x_ref[...] + mm(merged.astype(BF16), wout_ref)
    gate = jax.nn.sigmoid(mm(_rms(x1, pg_ref[...]).astype(BF16), wpg_ref))
    x2 = x1 + mm(p_ref[...].astype(BF16), wple_ref) * gate
    if final:
        o_ref[...] = _rms(x2, gn_ref[...])
    else:
        o_ref[...] = x2
        h_ref[0][...] = _rms(x2, gn_ref[...]).astype(BF16)


MISC_WIDTH = SSM_INNER + DIL_GROUP_WIDTH + 3 * D_MODEL
MISC_GB_LANE_BLOCK = SSM_INNER // LANES
MISC_U_BLOCK = (SSM_INNER + DIL_GROUP_WIDTH) // D_MODEL


def _tail(x2d, ya, yb, yc, misc, p2d, nw, woa, wob, woc, wout, pg, wpg, wple, gn, final, tm=512):
    m = x2d.shape[0]
    out_specs = pl.BlockSpec((tm, D_MODEL), lambda i: (i, 0))
    out_shape = jax.ShapeDtypeStruct((m, D_MODEL), F32)
    if not final:
        out_specs = [out_specs, pl.BlockSpec((tm, D_MODEL), lambda i: (i, 0))]
        out_shape = [out_shape, jax.ShapeDtypeStruct((m, D_MODEL), BF16)]

    def rows(width, off=0):
        return pl.BlockSpec((tm, width), lambda i: (i, off))

    def whole(arr):
        return pl.BlockSpec(arr.shape, lambda i: (0, 0), pipeline_mode=pl.Buffered(1))

    ub = MISC_U_BLOCK
    return pl.pallas_call(
        functools.partial(_tail_kernel, final=final),
        grid=(m // tm,),
        in_specs=[rows(D_MODEL), rows(NA_WIDTH), rows(DIL_GROUP_WIDTH), rows(SSM_INNER),
                  rows(SSM_INNER, 0), rows(D_MODEL, ub), rows(D_MODEL, ub + 1), rows(D_MODEL, ub + 2),
                  rows(PLE_DIM),
                  whole(nw), whole(woa), whole(wob), whole(woc), whole(wout), whole(pg), whole(wpg),
                  whole(wple), whole(gn)],
        out_specs=out_specs,
        out_shape=out_shape,
        compiler_params=_params("parallel"),
        name="tail",
    )(x2d, ya, yb, yc, misc, misc, misc, misc, p2d, nw, woa, wob, woc, wout, pg, wpg, wple, gn)


def _prep_weights(w_in, conv_w, conv_b, a_log, dt_bias, d_skip):
    depth = w_in.shape[0]
    offs = np.concatenate([[0], np.cumsum(IN_SPLITS)])
    w16 = w_in.astype(BF16)
    (qa, ka, va, ga, qb, kb, vb, gb, xbc, z, dtr, ua, ub, uc) = [
        w16[:, :, int(offs[i]):int(offs[i + 1])] for i in range(len(IN_SPLITS))]
    w_a = w16[:, :, :int(offs[4])]
    gw = DIL_GROUP_WIDTH
    w_b = [jnp.concatenate([_rotary_layout(qb[:, :, g * gw:(g + 1) * gw]),
                            _rotary_layout(kb[:, :, g * gw:(g + 1) * gw]),
                            vb[:, :, g * gw:(g + 1) * gw]], axis=2)
           for g in range(len(DIL_PAIRS))]
    w_misc = jnp.concatenate([z, gb, ua, ub, uc], axis=2)

    hpg = SSM_HEADS_PER_GROUP

    def per_group(t):
        lead = t.shape[:-2]
        t = t.reshape(lead + (2, SSM_GROUPS, hpg))
        t = jnp.moveaxis(t, -2, -3).reshape(lead + (SSM_GROUPS, 2 * hpg))
        t = jnp.tile(t, (1,) * (t.ndim - 1) + (SSM_DT_COPIES,))
        pad = [(0, 0)] * (t.ndim - 1) + [(0, LANES - 2 * hpg * SSM_DT_COPIES)]
        return jnp.pad(t, pad).reshape(lead + (SSM_GROUPS * LANES,))

    w_dt = per_group(dtr.reshape(depth, D_MODEL, 2, SSM_HEADS))
    w_xbc = xbc
    dtb = per_group(dt_bias.astype(F32)).reshape(depth, SSM_GROUPS, 1, LANES)
    a_neg = per_group(-jnp.exp(a_log.astype(F32))).reshape(depth, SSM_GROUPS, 1, LANES)
    dsk = jnp.repeat(d_skip.astype(F32), HEAD_DIM, axis=1).reshape(depth, SSM_GROUPS, 1, SSM_GROUP_WIDTH)
    cw = conv_w.astype(F32).reshape(depth, 1, SSM_CONV, SSM_CONV_DIM)
    cbias = conv_b.astype(F32).reshape(depth, 1, 1, SSM_CONV_DIM)
    return w_a, w_b, w_misc, w_xbc, w_dt, dtb, a_neg, dsk, cw, cbias


def kernel(x, p, norm_w, w_in, na_rpb, conv_w, conv_b, a_log, dt_bias, d_skip, ssm_norm_w,
           w_oa, w_ob, w_oc, w_out, ple_norm_w, w_ple, w_ple_gate, final_norm_w):
    b, s, dm = x.shape
    depth = w_in.shape[0]
    m = b * s
    w_a, w_b, w_misc, w_xbc, w_dt, dtb, a_neg, dsk, cw, cbias = _prep_weights(
        w_in, conv_w, conv_b, a_log, dt_bias, d_skip)
    tabs = [_rotary_tables(s, d) for _, d in DIL_PAIRS]
    row = lambda v: v.astype(F32).reshape(1, -1)

    x2d = x.reshape(m, dm)
    h = _norm(x2d, row(norm_w[0]))
    for i in range(depth):
        final = i == depth - 1
        qkvg = _proj(h, w_a[i], 2 * NA_WIDTH, BF16,
                     lead_scale=(NA_WIDTH, SCORE_SCALE)).reshape(b, s, 4 * NA_WIDTH)
        misc = _proj(h, w_misc[i], MISC_WIDTH // 2, BF16)
        xbc = _proj(h, w_xbc[i], SSM_CONV_DIM, BF16).reshape(b, s, SSM_CONV_DIM)
        dtp = _proj(h, w_dt[i], SSM_GROUPS * LANES, F32).reshape(b, s, SSM_GROUPS * LANES)
        h3 = h.reshape(b, s, dm)
        qkvs = [_proj_dil(h3, w_b[gi][i], tabs[gi], d) for gi, (_, d) in enumerate(DIL_PAIRS)]

        ya = _na_attention(qkvg, _na_bias_table(na_rpb[i])).reshape(m, NA_WIDTH)
        yb = _dil_attention(qkvs, misc.reshape(b, s, MISC_WIDTH), MISC_GB_LANE_BLOCK)
        yb = yb.reshape(m, DIL_GROUP_WIDTH)
        yc = _ssd(xbc, dtp, cw[i], cbias[i], dtb[i], a_neg[i], dsk[i]).reshape(m, SSM_INNER)

        gn = row(final_norm_w) if final else row(norm_w[i + 1])
        out = _tail(x2d, ya, yb, yc, misc, p[i].reshape(m, PLE_DIM), row(ssm_norm_w[i]),
                    w_oa[i].astype(BF16), w_ob[i].astype(BF16), w_oc[i].astype(BF16),
                    w_out[i].astype(BF16), row(ple_norm_w[i]), w_ple_gate[i].astype(BF16),
                    w_ple[i].astype(BF16), gn, final=final)
        if final:
            x2d = out
        else:
            x2d, h = out
    return x2d.reshape(b, s, dm)
```

```python
import functools
import math

import numpy as np
import jax
import jax.numpy as jnp
from jax import lax
from jax.experimental import pallas as pl
from jax.experimental.pallas import tpu as pltpu

F32 = jnp.float32
BF16 = jnp.bfloat16

LANES = 128
VMEM_LIMIT_BYTES = 56 * 1024 * 1024

D_MODEL = 1024
GRID_W = 64
HEAD_DIM = 64
EPS = 1e-6
PLE_DIM = 256

NA_HEADS = 16
NA_WIDTH = NA_HEADS * HEAD_DIM
NA_WIN_ROWS = 8
NA_WIN_COLS = 16

DIL_PAIRS = ((128, 1), (512, 4), (2048, 16))
DIL_HEADS_PER_GROUP = 8
DIL_GROUP_WIDTH = DIL_HEADS_PER_GROUP * HEAD_DIM
DIL_WIDTH = DIL_GROUP_WIDTH * len(DIL_PAIRS)
DIL_BLK = 64
ROPE_THETA = 500000.0
ROPE_DIM = HEAD_DIM // 4
ROPE_HALF = ROPE_DIM // 2

SSM_INNER = 1536
SSM_HEADS = 24
SSM_GROUPS = 4
SSM_HEADS_PER_GROUP = SSM_HEADS // SSM_GROUPS
SSM_GROUP_WIDTH = SSM_HEADS_PER_GROUP * HEAD_DIM
SSM_STATE = 128
SSM_CONV = 5
SSM_CHUNK = 128
SSM_CONV_DIM = SSM_INNER + 2 * SSM_GROUPS * SSM_STATE

IN_SPLITS = (NA_WIDTH, NA_WIDTH, NA_WIDTH, NA_WIDTH,
             DIL_WIDTH, DIL_WIDTH, DIL_WIDTH, DIL_GROUP_WIDTH,
             SSM_CONV_DIM, SSM_INNER, 2 * SSM_HEADS,
             D_MODEL, D_MODEL, D_MODEL)

LOG2E = math.log2(math.e)
SCORE_SCALE = HEAD_DIM ** -0.5 * LOG2E


def _params(*semantics):
    return pltpu.CompilerParams(dimension_semantics=semantics,
                                vmem_limit_bytes=VMEM_LIMIT_BYTES)


def _rms(x, g):
    return x * lax.rsqrt(jnp.mean(x * x, axis=-1, keepdims=True) + EPS) * g


def _silu(x):
    x = x.astype(F32)
    return x * jax.nn.sigmoid(x)


def _sigmoid(x):
    return jax.nn.sigmoid(x.astype(F32))


def _norm_kernel(x_ref, g_ref, h_ref):
    h_ref[...] = _rms(x_ref[...], g_ref[...]).astype(h_ref.dtype)


def _norm(x2d, g, tm=1024):
    m, k = x2d.shape
    return pl.pallas_call(
        _norm_kernel,
        grid=(m // tm,),
        in_specs=[pl.BlockSpec((tm, k), lambda i: (i, 0)), pl.BlockSpec((1, k), lambda i: (0, 0))],
        out_specs=pl.BlockSpec((tm, k), lambda i: (i, 0)),
        out_shape=jax.ShapeDtypeStruct((m, k), BF16),
        compiler_params=_params("parallel"),
        name="norm",
    )(x2d, g)


def _proj_kernel(h_ref, w_ref, o_ref, *, lead_scale):
    acc = jnp.dot(h_ref[...], w_ref[...], preferred_element_type=F32)
    if lead_scale is None:
        o_ref[...] = acc.astype(o_ref.dtype)
    else:
        ncols, scale = lead_scale

        @pl.when(pl.program_id(1) == 0)
        def _():
            o_ref[:, :ncols] = (acc[:, :ncols] * scale).astype(o_ref.dtype)
            o_ref[:, ncols:] = acc[:, ncols:].astype(o_ref.dtype)

        @pl.when(pl.program_id(1) != 0)
        def _():
            o_ref[...] = acc.astype(o_ref.dtype)


def _proj(h2d, w, tn, out_dtype, tm=1024, lead_scale=None):
    m, k = h2d.shape
    n = w.shape[1]
    return pl.pallas_call(
        functools.partial(_proj_kernel, lead_scale=lead_scale),
        grid=(m // tm, n // tn),
        in_specs=[pl.BlockSpec((tm, k), lambda i, j: (i, 0)),
                  pl.BlockSpec((k, tn), lambda i, j: (0, j))],
        out_specs=pl.BlockSpec((tm, tn), lambda i, j: (i, j)),
        out_shape=jax.ShapeDtypeStruct((m, n), out_dtype),
        compiler_params=_params("parallel", "arbitrary"),
        name="proj",
    )(h2d, w)


MXU_N = 256


REGROUP_ROWS = 256


def _regroup_rows(h, d):
    tm = h.shape[0]
    if d == 1:
        return h
    per = REGROUP_ROWS // d
    r = lax.broadcasted_iota(jnp.int32, (REGROUP_ROWS, REGROUP_ROWS), 0)
    c = lax.broadcasted_iota(jnp.int32, (REGROUP_ROWS, REGROUP_ROWS), 1)
    perm = jnp.where(c == (r % per) * d + r // per, 1.0, 0.0).astype(BF16)
    nblk = tm // REGROUP_ROWS
    blocks = [jnp.dot(perm, h[i * REGROUP_ROWS:(i + 1) * REGROUP_ROWS],
                      preferred_element_type=F32).astype(BF16) for i in range(nblk)]
    return jnp.concatenate([blocks[i][rho * per:(rho + 1) * per]
                            for rho in range(d) for i in range(nblk)], axis=0)


def _proj_dil_kernel(h_ref, w_ref, c_ref, s_ref, o_ref):
    nres, rows = o_ref.shape[1:3]
    k, n = w_ref.shape
    tm = nres * rows
    gw = DIL_GROUP_WIDTH
    hp = _regroup_rows(h_ref[0], nres)
    for nb in range(n // MXU_N):
        lo = nb * MXU_N
        acc = jnp.dot(hp, w_ref[:, lo:lo + MXU_N], preferred_element_type=F32)
        part = lo // gw
        if part == 2:
            o_ref[0, :, :, lo:lo + MXU_N] = acc.astype(o_ref.dtype).reshape(nres, rows, MXU_N)
            continue
        c = c_ref[part].reshape(tm, LANES)
        sn = s_ref[part].reshape(tm, LANES)
        for cb in range(MXU_N // LANES):
            blk = acc[:, cb * LANES:(cb + 1) * LANES]
            rot = blk * c + pltpu.roll(blk, LANES // 2, 1) * sn
            o_ref[0, :, :, lo + cb * LANES:lo + (cb + 1) * LANES] = (
                rot.astype(o_ref.dtype).reshape(nres, rows, LANES))


def _proj_dil(h3, w, tabs, d, tm=1024):
    b, s, k = h3.shape
    n = w.shape[1]
    rows = tm // d
    c, sn = tabs
    tab_spec = pl.BlockSpec((2, d, rows, LANES), lambda bi, i: (0, 0, i, 0))
    return pl.pallas_call(
        _proj_dil_kernel,
        grid=(b, s // tm),
        in_specs=[pl.BlockSpec((1, tm, k), lambda bi, i: (bi, i, 0)),
                  pl.BlockSpec((k, n), lambda bi, i: (0, 0)),
                  tab_spec, tab_spec],
        out_specs=pl.BlockSpec((1, d, rows, n), lambda bi, i: (bi, 0, i, 0)),
        out_shape=jax.ShapeDtypeStruct((b, d, s // d, n), BF16),
        compiler_params=_params("parallel", "parallel"),
        name=f"proj_dil{d}",
    )(h3, w, c, sn)


def _rotary_lane_sources():
    half = LANES // 2
    head = np.zeros(LANES, np.int64)
    dim = np.zeros(LANES, np.int64)
    for lane in range(LANES):
        off = lane % half
        is_b = lane >= half
        if off < 2 * ROPE_HALF:
            head[lane] = off // ROPE_HALF
            dim[lane] = off % ROPE_HALF + (ROPE_HALF if is_b else 0)
        else:
            head[lane] = 1 if is_b else 0
            dim[lane] = off
    return head, dim


def _rotary_tables(s, d):
    inv = ROPE_THETA ** (-jnp.arange(0, ROPE_DIM, 2, dtype=F32) / ROPE_DIM)
    ang = jnp.arange(s).astype(F32)[:, None] * inv[None, :]
    cos, sin = jnp.cos(ang), jnp.sin(ang)
    _, dim = _rotary_lane_sources()
    lanes = np.arange(LANES)
    rotary = (lanes % (LANES // 2)) < 2 * ROPE_HALF
    is_b = lanes >= LANES // 2
    idx = dim % ROPE_HALF
    c = jnp.where(rotary[None, :], cos[:, idx], 1.0)
    sn = jnp.where(rotary[None, :], jnp.where(is_b[None, :], sin[:, idx], -sin[:, idx]), 0.0)

    def arrange(t):
        t = t.reshape(s // d, d, LANES).transpose(1, 0, 2)
        return jnp.stack([t * SCORE_SCALE, t])

    return arrange(c), arrange(sn)


def _rotary_layout(t):
    lead = t.shape[:-1]
    t = t.reshape(lead + (t.shape[-1] // LANES, 2, HEAD_DIM))
    npair = t.shape[-3]
    a = t[..., :, :, 0:ROPE_HALF].reshape(lead + (npair, 2 * ROPE_HALF))
    b = t[..., :, :, ROPE_HALF:ROPE_DIM].reshape(lead + (npair, 2 * ROPE_HALF))
    rest0 = t[..., :, 0, ROPE_DIM:]
    rest1 = t[..., :, 1, ROPE_DIM:]
    return jnp.concatenate([a, rest0, b, rest1], axis=-1).reshape(lead + (npair * LANES,))


NA_UNROLL = 16


def _na_kernel(q_ref, k_ref, v_ref, g_ref, tb_ref, o_ref):
    rows = q_ref.shape[1] // GRID_W
    nkeys = NA_WIN_ROWS * GRID_W
    lane = lax.broadcasted_iota(jnp.int32, (GRID_W, LANES), 1)
    first = lane < HEAD_DIM

    def body(step, carry):
        units = []
        for i in range(NA_UNROLL):
            r = step * NA_UNROLL + i
            r0 = jnp.clip(r - NA_WIN_ROWS // 2, 0, rows - NA_WIN_ROWS)
            st = r0 - r + NA_WIN_ROWS - 1
            qs = pl.ds(pl.multiple_of(r * GRID_W, GRID_W), GRID_W)
            ks = pl.ds(pl.multiple_of(r0 * GRID_W, GRID_W), nkeys)
            q = q_ref[0, qs, :]
            kw = k_ref[0, ks, :]
            zero = jnp.zeros_like(q)
            q2 = jnp.concatenate([jnp.where(first, q, zero), jnp.where(first, zero, q)], axis=0)
            sc = lax.dot_general(q2, kw, (((1,), (1,)), ((), ())),
                                 preferred_element_type=F32)
            units.append((qs, ks, sc + tb_ref[0, st]))
        probs = []
        for qs, ks, sc in units:
            mx = jnp.max(sc, axis=-1, keepdims=True)
            e = jnp.exp2(sc - mx)
            probs.append((e.astype(BF16), jnp.sum(e, axis=-1, keepdims=True)))
        for (qs, ks, _), (p, den) in zip(units, probs):
            o2 = jnp.dot(p, v_ref[0, ks, :], preferred_element_type=F32) / den
            o = jnp.where(first, o2[0:GRID_W], o2[GRID_W:])
            o_ref[0, qs, :] = (o * _silu(g_ref[0, qs, :])).astype(o_ref.dtype)
        return carry

    lax.fori_loop(0, rows // NA_UNROLL, body, 0)


def _na_bias_table(rpb):
    h = rpb.shape[0]
    cq = np.arange(GRID_W)[:, None]
    ck = np.arange(GRID_W)[None, :]
    ws = np.clip(cq - NA_WIN_COLS // 2, 0, GRID_W - NA_WIN_COLS)
    in_win = (ck >= ws) & (ck < ws + NA_WIN_COLS)
    line = 2 * GRID_W
    lo = GRID_W - NA_WIN_COLS
    ext = jnp.pad(rpb.astype(F32) * LOG2E, ((0, 0), (0, 0), (lo, line - lo - rpb.shape[-1])))
    skew = jnp.broadcast_to(ext[:, :, None, :], ext.shape[:2] + (GRID_W, line))
    skew = skew.reshape(ext.shape[:2] + (GRID_W * line,))[:, :, :GRID_W * (line - 1)]
    toep = skew.reshape(ext.shape[:2] + (GRID_W, line - 1))[:, :, :, GRID_W - 1:]
    colb = jnp.where(in_win, toep, -jnp.inf)
    tb = jnp.stack([colb[:, st:st + NA_WIN_ROWS] for st in range(NA_WIN_ROWS)], axis=1)
    tb = tb.reshape(h // 2, 2, NA_WIN_ROWS, NA_WIN_ROWS, GRID_W, GRID_W)
    tb = tb.transpose(0, 2, 1, 4, 3, 5).reshape(h // 2, NA_WIN_ROWS, 2 * GRID_W, NA_WIN_ROWS * GRID_W)
    return tb


def _na_attention(qkvg, tb):
    b, s, _ = qkvg.shape
    npair = NA_WIDTH // LANES

    def col(off):
        return pl.BlockSpec((1, s, LANES), lambda hp, bi: (bi, 0, off + hp))

    return pl.pallas_call(
        _na_kernel,
        grid=(npair, b),
        in_specs=[col(0), col(npair), col(2 * npair), col(3 * npair),
                  pl.BlockSpec((1,) + tb.shape[1:], lambda hp, bi: (hp, 0, 0, 0))],
        out_specs=pl.BlockSpec((1, s, LANES), lambda hp, bi: (bi, 0, hp)),
        out_shape=jax.ShapeDtypeStruct((b, s, NA_WIDTH), BF16),
        compiler_params=_params("parallel", "parallel"),
        name="na_attention",
    )(qkvg, qkvg, qkvg, qkvg, tb)


DIL_QB = 2 * DIL_BLK
DIL_KB = 4 * DIL_BLK
DIL_UNROLL = 8


def _dil_kernel(q0, k0, v0, q1, k1, v1, q2, k2, v2, g_ref, o_ref, acc_ref, m_ref, w_ref):
    s = o_ref.shape[1]
    lane = lax.broadcasted_iota(jnp.int32, (DIL_QB, LANES), 1)
    first = lane < HEAD_DIM
    half_off = lane % (LANES // 2)
    q_first = (half_off < ROPE_HALF) | ((half_off >= 2 * ROPE_HALF) & (lane < LANES // 2))
    v_first = lax.broadcasted_iota(jnp.int32, (DIL_KB, LANES), 1) < HEAD_DIM
    qi = lax.broadcasted_iota(jnp.int32, (DIL_QB, DIL_KB), 0)
    ki = lax.broadcasted_iota(jnp.int32, (DIL_QB, DIL_KB), 1)

    for g, (q_ref, k_ref, v_ref) in enumerate(((q0, k0, v0), (q1, k1, v1), (q2, k2, v2))):
        d = DIL_PAIRS[g][1]
        l = s // d
        nu = l // DIL_QB

        def trip(step, carry, g=g, d=d, l=l, nu=nu, q_ref=q_ref, k_ref=k_ref, v_ref=v_ref):
            units = []
            for i in range(DIL_UNROLL):
                idx = step * DIL_UNROLL + i
                rho = idx // nu
                u = idx % nu
                q_start = pl.multiple_of(u * DIL_QB, DIL_QB)
                k_start = pl.multiple_of(jnp.clip(u * DIL_QB - DIL_BLK, 0, l - DIL_KB), DIL_BLK)
                q = q_ref[0, rho, pl.ds(q_start, DIL_QB), :]
                kw = k_ref[0, rho, pl.ds(k_start, DIL_KB), :]
                valid = jnp.abs((ki + k_start) - (qi + q_start)) <= DIL_BLK
                zero = jnp.zeros_like(q)
                scs = []
                for h in range(2):
                    qh = jnp.where(q_first, q, zero) if h == 0 else jnp.where(q_first, zero, q)
                    sc = lax.dot_general(qh, kw, (((1,), (1,)), ((), ())),
                                         preferred_element_type=F32)
                    scs.append(jnp.where(valid, sc, -jnp.inf))
                units.append((rho, q_start, k_start, scs))
            soft = []
            for rho, q_start, k_start, scs in units:
                es, mxs = [], []
                for sc in scs:
                    mx = jnp.max(sc, axis=-1, keepdims=True)
                    es.append(jnp.exp2(sc - mx).astype(BF16))
                    mxs.append(mx)
                soft.append((es, jnp.where(first, mxs[0], mxs[1])))
            for (rho, q_start, k_start, _), (es, mx) in zip(units, soft):
                vw = v_ref[0, rho, pl.ds(k_start, DIL_KB), :]
                one = jnp.ones_like(vw)
                o0 = jnp.dot(es[0], jnp.where(v_first, vw, one), preferred_element_type=F32)
                o1 = jnp.dot(es[1], jnp.where(v_first, one, vw), preferred_element_type=F32)
                num = jnp.where(first, o0, o1)
                den = pltpu.roll(jnp.where(first, o1, o0), HEAD_DIM, 1)
                if d == 1:
                    rows = pl.ds(q_start, DIL_QB)
                else:
                    rows = pl.ds(rho + d * q_start, DIL_QB, stride=d)
                if g == 0:
                    acc_ref[rows, :] = num
                    m_ref[rows, :] = mx
                    w_ref[rows, :] = den
                else:
                    m_old = m_ref[rows, :]
                    m_new = jnp.maximum(m_old, mx)
                    a = jnp.exp2(m_old - m_new)
                    bw = jnp.exp2(mx - m_new)
                    acc_ref[rows, :] = acc_ref[rows, :] * a + num * bw
                    w_ref[rows, :] = w_ref[rows, :] * a + den * bw
                    m_ref[rows, :] = m_new
            return carry

        lax.fori_loop(0, d * nu // DIL_UNROLL, trip, 0)

    def finish(i, carry):
        rows = pl.ds(pl.multiple_of(i * 256, 256), 256)
        y = acc_ref[rows, :] / w_ref[rows, :]
        o_ref[0, rows, :] = (y * _silu(g_ref[0, rows, :])).astype(o_ref.dtype)
        return carry

    lax.fori_loop(0, s // 256, finish, 0)


def _dil_attention(qkvs, gate, gate_off):
    b = gate.shape[0]
    s = gate.shape[1]
    npair = DIL_GROUP_WIDTH // LANES
    in_specs = []
    args = []
    for arr in qkvs:
        d, l = arr.shape[1], arr.shape[2]
        for part in range(3):
            in_specs.append(pl.BlockSpec((1, d, l, LANES),
                                         lambda bi, sp, part=part: (bi, 0, 0, part * npair + sp)))
            args.append(arr)
    in_specs.append(pl.BlockSpec((1, s, LANES), lambda bi, sp: (bi, 0, gate_off + sp)))
    args.append(gate)
    return pl.pallas_call(
        _dil_kernel,
        grid=(b, npair),
        in_specs=in_specs,
        out_specs=pl.BlockSpec((1, s, LANES), lambda bi, sp: (bi, 0, sp)),
        out_shape=jax.ShapeDtypeStruct((b, s, DIL_GROUP_WIDTH), BF16),
        scratch_shapes=[pltpu.VMEM((s, LANES), F32)] * 3,
        compiler_params=_params("parallel", "parallel"),
        name="dil_attention",
    )(*args)


def _split3(x):
    hi = x.astype(BF16)
    r1 = x - hi.astype(F32)
    mid = r1.astype(BF16)
    lo = (r1 - mid.astype(F32)).astype(BF16)
    return hi, mid, lo


SSM_HALO = 16
SSM_DT_COPIES = 2
SSM_UNROLL = 4


def _ssd_kernel(xs_ref, bm_ref, cm_ref, dt_ref, wx_ref, wb_ref, wc_ref, bx_ref, bb_ref, bc_ref,
                dtb_ref, a_ref, dsk_ref, y_ref, xs_s, bt_s, c_s, cb_s, acum_s, pt_s, st_s):
    s = xs_ref.shape[1]
    cl = SSM_CHUNK
    nchunk = s // cl
    hpg = SSM_HEADS_PER_GROUP
    npair = SSM_GROUP_WIDTH // LANES
    ncol = 2 * hpg

    def conv_chunk(c, carry):
        base = pl.multiple_of(c * cl, cl)
        prev_start = pl.multiple_of(jnp.maximum(base - SSM_HALO, 0), SSM_HALO)
        next_start = pl.multiple_of(jnp.minimum(base + cl, s - SSM_HALO), SSM_HALO)
        has_prev = jnp.where(c > 0, 1.0, 0.0)
        has_next = jnp.where(c < nchunk - 1, 1.0, 0.0)

        def conv(src, w_ref, bias_ref):
            ext = jnp.concatenate([src[0, pl.ds(prev_start, SSM_HALO), :].astype(F32) * has_prev,
                                   src[0, pl.ds(base, cl), :].astype(F32),
                                   src[0, pl.ds(next_start, SSM_HALO), :].astype(F32) * has_next], axis=0)
            acc = bias_ref[0]
            off = SSM_HALO - SSM_CONV // 2
            for tap in range(SSM_CONV):
                acc = acc + ext[off + tap:off + tap + cl, :] * w_ref[0, tap:tap + 1, :]
            return _silu(acc)

        xs_s[pl.ds(base, cl), :] = conv(xs_ref, wx_ref, bx_ref)
        bmat = conv(bm_ref, wb_ref, bb_ref).astype(BF16)
        cmat = conv(cm_ref, wc_ref, bc_ref).astype(BF16)
        bt_s[c] = bmat.astype(F32).T.astype(BF16)
        c_s[pl.ds(base, cl), :] = cmat
        cb_s[c] = lax.dot_general(cmat, bmat, (((1,), (1,)), ((), ())),
                                  preferred_element_type=F32)

        dt = jax.nn.softplus(dt_ref[0, pl.ds(base, cl), :] + dt_bias)
        pieces = _split3(dt * a_neg)
        acum_f = sum(jnp.dot(tril, p, preferred_element_type=F32) for p in pieces)
        acum_b = sum(jnp.dot(triu, p, preferred_element_type=F32) for p in pieces)
        acum = jnp.where(fwd_lane, acum_f, acum_b)
        a_end = jnp.where(fwd_lane[0:1], acum[cl - 1:cl, :], acum[0:1, :])
        w_state = dt * jnp.exp(a_end - acum)
        packed_t = jnp.where(lane < ncol, acum - jnp.log(dt), w_state).T
        acum_s[c] = acum
        pt_s[c] = packed_t[0:SSM_DT_COPIES * ncol, :]
        return carry

    ri = lax.broadcasted_iota(jnp.int32, (cl, cl), 0)
    ci = lax.broadcasted_iota(jnp.int32, (cl, cl), 1)
    lane = lax.broadcasted_iota(jnp.int32, (cl, LANES), 1)
    first = lane < HEAD_DIM
    fwd_lane = (lane % ncol) < hpg
    tril = jnp.where(ci <= ri, 1.0, 0.0).astype(BF16)
    triu = jnp.where(ci >= ri, 1.0, 0.0).astype(BF16)
    dt_bias = dtb_ref[0]
    a_neg = a_ref[0]

    lax.fori_loop(0, nchunk, conv_chunk, 0, unroll=4)

    for direction in range(2):
        if direction == 0:
            keep = ci <= ri
            end_row = cl - 1
        else:
            keep = ci >= ri
            end_row = 0

        st_s[...] = jnp.zeros_like(st_s)

        def local_part(c, direction=direction, keep=keep, end_row=end_row):
            rows = pl.ds(pl.multiple_of(c * cl, cl), cl)
            acum = acum_s[c]
            packed_t = pt_s[c]

            xs = xs_s[rows, :]
            bt = bt_s[c].astype(F32)
            cb = cb_s[c]
            yd_blocks, new_blocks, scale_blocks = [], [], []
            for pr in range(npair):
                xblk = xs[:, pr * LANES:(pr + 1) * LANES].astype(BF16)
                lhs, cols = [], []
                for j in (direction * hpg + 2 * pr, direction * hpg + 2 * pr + 1):
                    col = jnp.broadcast_to(acum[:, j:j + 1], (cl, cl))
                    row = packed_t[j:j + 1, :]
                    ws_row = packed_t[ncol + j:ncol + j + 1, :]
                    seg_dt = jnp.exp(jnp.where(keep, col - row, -jnp.inf))
                    lhs.append((cb * seg_dt).astype(BF16))
                    lhs.append((bt * ws_row).astype(BF16))
                    cols.append(col)
                prod = jnp.dot(jnp.concatenate(lhs, axis=0), xblk, preferred_element_type=F32)
                yd_blocks.append(jnp.where(first, prod[0:cl], prod[2 * cl:3 * cl]))
                new_blocks.append(jnp.where(first, prod[cl:2 * cl], prod[3 * cl:4 * cl]))
                scale_blocks.append(jnp.exp(jnp.where(first, cols[0], cols[1])))
            return rows, xs, yd_blocks, new_blocks, scale_blocks

        def trip(step, carry, direction=direction, end_row=end_row):
            chunks = [step * SSM_UNROLL + i for i in range(SSM_UNROLL)]
            if direction == 1:
                chunks = [nchunk - 1 - c for c in chunks]
            parts = [local_part(c) for c in chunks]
            state = [st_s[:, pr * LANES:(pr + 1) * LANES] for pr in range(npair)]
            for rows, xs, yd_blocks, new_blocks, scale_blocks in parts:
                cmat = c_s[rows, :]
                y_blocks = []
                for pr in range(npair):
                    y_off = jnp.dot(cmat, state[pr].astype(BF16), preferred_element_type=F32)
                    y_blocks.append(yd_blocks[pr] + y_off * scale_blocks[pr])
                    state[pr] = (state[pr] * scale_blocks[pr][end_row:end_row + 1, :]
                                 + new_blocks[pr])
                y = jnp.concatenate(y_blocks, axis=1)
                if direction == 0:
                    y_ref[0, rows, :] = y + dsk_ref[0] * xs
                else:
                    y_ref[0, rows, :] = y_ref[0, rows, :] + y
            for pr in range(npair):
                st_s[:, pr * LANES:(pr + 1) * LANES] = state[pr]
            return carry

        lax.fori_loop(0, nchunk // SSM_UNROLL, trip, 0)


def _ssd(xbc, dtp, conv_w, conv_b, dt_bias, a_neg, d_skip):
    b, s, _ = xbc.shape
    gw = SSM_GROUP_WIDTH
    b_off = SSM_INNER // LANES
    c_off = b_off + SSM_GROUPS
    nchunk = s // SSM_CHUNK

    def seq(width, off):
        return pl.BlockSpec((1, s, width), lambda bi, g: (bi, 0, off + g))

    def par(rows, width, off):
        return pl.BlockSpec((1, rows, width), lambda bi, g: (0, 0, off + g))

    return pl.pallas_call(
        _ssd_kernel,
        grid=(b, SSM_GROUPS),
        in_specs=[seq(gw, 0), seq(LANES, b_off), seq(LANES, c_off), seq(LANES, 0),
                  par(SSM_CONV, gw, 0), par(SSM_CONV, LANES, b_off), par(SSM_CONV, LANES, c_off),
                  par(1, gw, 0), par(1, LANES, b_off), par(1, LANES, c_off),
                  pl.BlockSpec((1, 1, LANES), lambda bi, g: (g, 0, 0)),
                  pl.BlockSpec((1, 1, LANES), lambda bi, g: (g, 0, 0)),
                  pl.BlockSpec((1, 1, gw), lambda bi, g: (g, 0, 0))],
        out_specs=pl.BlockSpec((1, s, gw), lambda bi, g: (bi, 0, g)),
        out_shape=jax.ShapeDtypeStruct((b, s, SSM_INNER), F32),
        scratch_shapes=[pltpu.VMEM((s, gw), F32),
                        pltpu.VMEM((nchunk, SSM_STATE, SSM_CHUNK), BF16),
                        pltpu.VMEM((s, LANES), BF16),
                        pltpu.VMEM((nchunk, SSM_CHUNK, SSM_CHUNK), F32),
                        pltpu.VMEM((nchunk, SSM_CHUNK, LANES), F32),
                        pltpu.VMEM((nchunk, SSM_DT_COPIES * 2 * SSM_HEADS_PER_GROUP, SSM_CHUNK), F32),
                        pltpu.VMEM((SSM_STATE, gw), F32)],
        compiler_params=_params("parallel", "parallel"),
        name="ssd",
    )(xbc, xbc, xbc, dtp, conv_w, conv_w, conv_w, conv_b, conv_b, conv_b, dt_bias, a_neg, d_skip)


def _tail_kernel(x_ref, ya_ref, yb_ref, yc_ref, z_ref, ua_ref, ub_ref, uc_ref, p_ref,
                 nw_ref, woa_ref, wob_ref, woc_ref, wout_ref, pg_ref, wpg_ref, wple_ref, gn_ref,
                 o_ref, *h_ref, final):
    def mm(a, w_ref):
        return jnp.dot(a, w_ref[...], preferred_element_type=F32)

    ya = mm(ya_ref[...], woa_ref)
    yb = mm(yb_ref[...], wob_ref)
    yc_in = _rms(yc_ref[...] * _silu(z_ref[...]), nw_ref[...]).astype(BF16)
    yc = mm(yc_in, woc_ref)
    merged = (_sigmoid(ua_ref[...]) * ya + _sigmoid(ub_ref[...]) * yb
              + _sigmoid(uc_ref[...]) * yc)
    x1 = x_ref[...] + mm(merged.astype(BF16), wout_ref)
    gate = jax.nn.sigmoid(mm(_rms(x1, pg_ref[...]).astype(BF16), wpg_ref))
    x2 = x1 + mm(p_ref[...].astype(BF16), wple_ref) * gate
    if final:
        o_ref[...] = _rms(x2, gn_ref[...])
    else:
        o_ref[...] = x2
        h_ref[0][...] = _rms(x2, gn_ref[...]).astype(BF16)


MISC_WIDTH = SSM_INNER + DIL_GROUP_WIDTH + 3 * D_MODEL
MISC_GB_LANE_BLOCK = SSM_INNER // LANES
MISC_U_BLOCK = (SSM_INNER + DIL_GROUP_WIDTH) // D_MODEL


def _tail(x2d, ya, yb, yc, misc, p2d, nw, woa, wob, woc, wout, pg, wpg, wple, gn, final, tm=512):
    m = x2d.shape[0]
    out_specs = pl.BlockSpec((tm, D_MODEL), lambda i: (i, 0))
    out_shape = jax.ShapeDtypeStruct((m, D_MODEL), F32)
    if not final:
        out_specs = [out_specs, pl.BlockSpec((tm, D_MODEL), lambda i: (i, 0))]
        out_shape = [out_shape, jax.ShapeDtypeStruct((m, D_MODEL), BF16)]

    def rows(width, off=0):
        return pl.BlockSpec((tm, width), lambda i: (i, off))

    def whole(arr):
        return pl.BlockSpec(arr.shape, lambda i: (0, 0), pipeline_mode=pl.Buffered(1))

    ub = MISC_U_BLOCK
    return pl.pallas_call(
        functools.partial(_tail_kernel, final=final),
        grid=(m // tm,),
        in_specs=[rows(D_MODEL), rows(NA_WIDTH), rows(DIL_GROUP_WIDTH), rows(SSM_INNER),
                  rows(SSM_INNER, 0), rows(D_MODEL, ub), rows(D_MODEL, ub + 1), rows(D_MODEL, ub + 2),
                  rows(PLE_DIM),
                  whole(nw), whole(woa), whole(wob), whole(woc), whole(wout), whole(pg), whole(wpg),
                  whole(wple), whole(gn)],
        out_specs=out_specs,
        out_shape=out_shape,
        compiler_params=_params("parallel"),
        name="tail",
    )(x2d, ya, yb, yc, misc, misc, misc, misc, p2d, nw, woa, wob, woc, wout, pg, wpg, wple, gn)


def _prep_weights(w_in, conv_w, conv_b, a_log, dt_bias, d_skip):
    depth = w_in.shape[0]
    offs = np.concatenate([[0], np.cumsum(IN_SPLITS)])
    w16 = w_in.astype(BF16)
    (qa, ka, va, ga, qb, kb, vb, gb, xbc, z, dtr, ua, ub, uc) = [
        w16[:, :, int(offs[i]):int(offs[i + 1])] for i in range(len(IN_SPLITS))]
    w_a = w16[:, :, :int(offs[4])]
    gw = DIL_GROUP_WIDTH
    w_b = [jnp.concatenate([_rotary_layout(qb[:, :, g * gw:(g + 1) * gw]),
                            _rotary_layout(kb[:, :, g * gw:(g + 1) * gw]),
                            vb[:, :, g * gw:(g + 1) * gw]], axis=2)
           for g in range(len(DIL_PAIRS))]
    w_misc = jnp.concatenate([z, gb, ua, ub, uc], axis=2)

    hpg = SSM_HEADS_PER_GROUP

    def per_group(t):
        lead = t.shape[:-2]
        t = t.reshape(lead + (2, SSM_GROUPS, hpg))
        t = jnp.moveaxis(t, -2, -3).reshape(lead + (SSM_GROUPS, 2 * hpg))
        t = jnp.tile(t, (1,) * (t.ndim - 1) + (SSM_DT_COPIES,))
        pad = [(0, 0)] * (t.ndim - 1) + [(0, LANES - 2 * hpg * SSM_DT_COPIES)]
        return jnp.pad(t, pad).reshape(lead + (SSM_GROUPS * LANES,))

    w_dt = per_group(dtr.reshape(depth, D_MODEL, 2, SSM_HEADS))
    w_xbc = xbc
    dtb = per_group(dt_bias.astype(F32)).reshape(depth, SSM_GROUPS, 1, LANES)
    a_neg = per_group(-jnp.exp(a_log.astype(F32))).reshape(depth, SSM_GROUPS, 1, LANES)
    dsk = jnp.repeat(d_skip.astype(F32), HEAD_DIM, axis=1).reshape(depth, SSM_GROUPS, 1, SSM_GROUP_WIDTH)
    cw = conv_w.astype(F32).reshape(depth, 1, SSM_CONV, SSM_CONV_DIM)
    cbias = conv_b.astype(F32).reshape(depth, 1, 1, SSM_CONV_DIM)
    return w_a, w_b, w_misc, w_xbc, w_dt, dtb, a_neg, dsk, cw, cbias


def kernel(x, p, norm_w, w_in, na_rpb, conv_w, conv_b, a_log, dt_bias, d_skip, ssm_norm_w,
           w_oa, w_ob, w_oc, w_out, ple_norm_w, w_ple, w_ple_gate, final_norm_w):
    b, s, dm = x.shape
    depth = w_in.shape[0]
    m = b * s
    w_a, w_b, w_misc, w_xbc, w_dt, dtb, a_neg, dsk, cw, cbias = _prep_weights(
        w_in, conv_w, conv_b, a_log, dt_bias, d_skip)
    tabs = [_rotary_tables(s, d) for _, d in DIL_PAIRS]
    row = lambda v: v.astype(F32).reshape(1, -1)

    x2d = x.reshape(m, dm)
    h = _norm(x2d, row(norm_w[0]))
    for i in range(depth):
        final = i == depth - 1
        qkvg = _proj(h, w_a[i], 2 * NA_WIDTH, BF16,
                     lead_scale=(NA_WIDTH, SCORE_SCALE)).reshape(b, s, 4 * NA_WIDTH)
        misc = _proj(h, w_misc[i], MISC_WIDTH // 2, BF16)
        xbc = _proj(h, w_xbc[i], SSM_CONV_DIM, BF16).reshape(b, s, SSM_CONV_DIM)
        dtp = _proj(h, w_dt[i], SSM_GROUPS * LANES, F32).reshape(b, s, SSM_GROUPS * LANES)
        h3 = h.reshape(b, s, dm)
        qkvs = [_proj_dil(h3, w_b[gi][i], tabs[gi], d) for gi, (_, d) in enumerate(DIL_PAIRS)]

        ya = _na_attention(qkvg, _na_bias_table(na_rpb[i])).reshape(m, NA_WIDTH)
        yb = _dil_attention(qkvs, misc.reshape(b, s, MISC_WIDTH), MISC_GB_LANE_BLOCK)
        yb = yb.reshape(m, DIL_GROUP_WIDTH)
        yc = _ssd(xbc, dtp, cw[i], cbias[i], dtb[i], a_neg[i], dsk[i]).reshape(m, SSM_INNER)

        gn = row(final_norm_w) if final else row(norm_w[i + 1])
        out = _tail(x2d, ya, yb, yc, misc, p[i].reshape(m, PLE_DIM), row(ssm_norm_w[i]),
                    w_oa[i].astype(BF16), w_ob[i].astype(BF16), w_oc[i].astype(BF16),
                    w_out[i].astype(BF16), row(ple_norm_w[i]), w_ple_gate[i].astype(BF16),
                    w_ple[i].astype(BF16), gn, final=final)
        if final:
            x2d = out
        else:
            x2d, h = out
    return x2d.reshape(b, s, dm)
```

```python
import functools
import math

import numpy as np
import jax
import jax.numpy as jnp
from jax import lax
from jax.experimental import pallas as pl
from jax.experimental.pallas import tpu as pltpu

F32 = jnp.float32
BF16 = jnp.bfloat16

LANES = 128
VMEM_LIMIT_BYTES = 56 * 1024 * 1024

D_MODEL = 1024
GRID_W = 64
HEAD_DIM = 64
EPS = 1e-6
PLE_DIM = 256

NA_HEADS = 16
NA_WIDTH = NA_HEADS * HEAD_DIM
NA_WIN_ROWS = 8
NA_WIN_COLS = 16

DIL_PAIRS = ((128, 1), (512, 4), (2048, 16))
DIL_HEADS_PER_GROUP = 8
DIL_GROUP_WIDTH = DIL_HEADS_PER_GROUP * HEAD_DIM
DIL_WIDTH = DIL_GROUP_WIDTH * len(DIL_PAIRS)
DIL_BLK = 64
ROPE_THETA = 500000.0
ROPE_DIM = HEAD_DIM // 4
ROPE_HALF = ROPE_DIM // 2

SSM_INNER = 1536
SSM_HEADS = 24
SSM_GROUPS = 4
SSM_HEADS_PER_GROUP = SSM_HEADS // SSM_GROUPS
SSM_GROUP_WIDTH = SSM_HEADS_PER_GROUP * HEAD_DIM
SSM_STATE = 128
SSM_CONV = 5
SSM_CHUNK = 128
SSM_CONV_DIM = SSM_INNER + 2 * SSM_GROUPS * SSM_STATE

IN_SPLITS = (NA_WIDTH, NA_WIDTH, NA_WIDTH, NA_WIDTH,
             DIL_WIDTH, DIL_WIDTH, DIL_WIDTH, DIL_GROUP_WIDTH,
             SSM_CONV_DIM, SSM_INNER, 2 * SSM_HEADS,
             D_MODEL, D_MODEL, D_MODEL)

LOG2E = math.log2(math.e)
SCORE_SCALE = HEAD_DIM ** -0.5 * LOG2E


def _params(*semantics):
    return pltpu.CompilerParams(dimension_semantics=semantics,
                                vmem_limit_bytes=VMEM_LIMIT_BYTES)


def _rms(x, g):
    return x * lax.rsqrt(jnp.mean(x * x, axis=-1, keepdims=True) + EPS) * g


def _silu(x):
    x = x.astype(F32)
    return x * jax.nn.sigmoid(x)


def _sigmoid(x):
    return jax.nn.sigmoid(x.astype(F32))


def _norm_kernel(x_ref, g_ref, h_ref):
    h_ref[...] = _rms(x_ref[...], g_ref[...]).astype(h_ref.dtype)


def _norm(x2d, g, tm=1024):
    m, k = x2d.shape
    return pl.pallas_call(
        _norm_kernel,
        grid=(m // tm,),
        in_specs=[pl.BlockSpec((tm, k), lambda i: (i, 0)), pl.BlockSpec((1, k), lambda i: (0, 0))],
        out_specs=pl.BlockSpec((tm, k), lambda i: (i, 0)),
        out_shape=jax.ShapeDtypeStruct((m, k), BF16),
        compiler_params=_params("parallel"),
        name="norm",
    )(x2d, g)


def _proj_kernel(h_ref, w_ref, o_ref, *, lead_scale):
    acc = jnp.dot(h_ref[...], w_ref[...], preferred_element_type=F32)
    if lead_scale is None:
        o_ref[...] = acc.astype(o_ref.dtype)
    else:
        ncols, scale = lead_scale

        @pl.when(pl.program_id(1) == 0)
        def _():
            o_ref[:, :ncols] = (acc[:, :ncols] * scale).astype(o_ref.dtype)
            o_ref[:, ncols:] = acc[:, ncols:].astype(o_ref.dtype)

        @pl.when(pl.program_id(1) != 0)
        def _():
            o_ref[...] = acc.astype(o_ref.dtype)


def _proj(h2d, w, tn, out_dtype, tm=1024, lead_scale=None):
    m, k = h2d.shape
    n = w.shape[1]
    return pl.pallas_call(
        functools.partial(_proj_kernel, lead_scale=lead_scale),
        grid=(m // tm, n // tn),
        in_specs=[pl.BlockSpec((tm, k), lambda i, j: (i, 0)),
                  pl.BlockSpec((k, tn), lambda i, j: (0, j))],
        out_specs=pl.BlockSpec((tm, tn), lambda i, j: (i, j)),
        out_shape=jax.ShapeDtypeStruct((m, n), out_dtype),
        compiler_params=_params("parallel", "arbitrary"),
        name="proj",
    )(h2d, w)


MXU_N = 256


REGROUP_ROWS = 256


def _regroup_rows(h, d):
    tm = h.shape[0]
    if d == 1:
        return h
    per = REGROUP_ROWS // d
    r = lax.broadcasted_iota(jnp.int32, (REGROUP_ROWS, REGROUP_ROWS), 0)
    c = lax.broadcasted_iota(jnp.int32, (REGROUP_ROWS, REGROUP_ROWS), 1)
    perm = jnp.where(c == (r % per) * d + r // per, 1.0, 0.0).astype(BF16)
    nblk = tm // REGROUP_ROWS
    blocks = [jnp.dot(perm, h[i * REGROUP_ROWS:(i + 1) * REGROUP_ROWS],
                      preferred_element_type=F32).astype(BF16) for i in range(nblk)]
    return jnp.concatenate([blocks[i][rho * per:(rho + 1) * per]
                            for rho in range(d) for i in range(nblk)], axis=0)


def _proj_dil_kernel(h_ref, w_ref, c_ref, s_ref, o_ref):
    nres, rows = o_ref.shape[1:3]
    k, n = w_ref.shape
    tm = nres * rows
    gw = DIL_GROUP_WIDTH
    hp = _regroup_rows(h_ref[0], nres)
    for nb in range(n // MXU_N):
        lo = nb * MXU_N
        acc = jnp.dot(hp, w_ref[:, lo:lo + MXU_N], preferred_element_type=F32)
        part = lo // gw
        if part == 2:
            o_ref[0, :, :, lo:lo + MXU_N] = acc.astype(o_ref.dtype).reshape(nres, rows, MXU_N)
            continue
        c = c_ref[part].reshape(tm, LANES)
        sn = s_ref[part].reshape(tm, LANES)
        for cb in range(MXU_N // LANES):
            blk = acc[:, cb * LANES:(cb + 1) * LANES]
            rot = blk * c + pltpu.roll(blk, LANES // 2, 1) * sn
            o_ref[0, :, :, lo + cb * LANES:lo + (cb + 1) * LANES] = (
                rot.astype(o_ref.dtype).reshape(nres, rows, LANES))


def _proj_dil(h3, w, tabs, d, tm=1024):
    b, s, k = h3.shape
    n = w.shape[1]
    rows = tm // d
    c, sn = tabs
    tab_spec = pl.BlockSpec((2, d, rows, LANES), lambda bi, i: (0, 0, i, 0))
    return pl.pallas_call(
        _proj_dil_kernel,
        grid=(b, s // tm),
        in_specs=[pl.BlockSpec((1, tm, k), lambda bi, i: (bi, i, 0)),
                  pl.BlockSpec((k, n), lambda bi, i: (0, 0)),
                  tab_spec, tab_spec],
        out_specs=pl.BlockSpec((1, d, rows, n), lambda bi, i: (bi, 0, i, 0)),
        out_shape=jax.ShapeDtypeStruct((b, d, s // d, n), BF16),
        compiler_params=_params("parallel", "parallel"),
        name=f"proj_dil{d}",
    )(h3, w, c, sn)


def _rotary_tables(s, d):
    inv = ROPE_THETA ** (-jnp.arange(0, ROPE_DIM, 2, dtype=F32) / ROPE_DIM)
    ang = jnp.arange(s).astype(F32)[:, None] * inv[None, :]
    cos, sin = jnp.cos(ang), jnp.sin(ang)
    rest = LANES // 2 - 2 * ROPE_HALF
    ones = jnp.ones((s, rest), F32)
    zeros = jnp.zeros((s, rest), F32)
    c = jnp.concatenate([cos, cos, ones, cos, cos, ones], axis=1)
    sn = jnp.concatenate([-sin, -sin, zeros, sin, sin, zeros], axis=1)

    def arrange(t):
        t = t.reshape(s // d, d, LANES).transpose(1, 0, 2)
        return jnp.stack([t * SCORE_SCALE, t])

    return arrange(c), arrange(sn)


def _rotary_layout(t):
    lead = t.shape[:-1]
    t = t.reshape(lead + (t.shape[-1] // LANES, 2, HEAD_DIM))
    npair = t.shape[-3]
    a = t[..., :, :, 0:ROPE_HALF].reshape(lead + (npair, 2 * ROPE_HALF))
    b = t[..., :, :, ROPE_HALF:ROPE_DIM].reshape(lead + (npair, 2 * ROPE_HALF))
    rest0 = t[..., :, 0, ROPE_DIM:]
    rest1 = t[..., :, 1, ROPE_DIM:]
    return jnp.concatenate([a, rest0, b, rest1], axis=-1).reshape(lead + (npair * LANES,))


NA_UNROLL = 16


def _na_kernel(q_ref, k_ref, v_ref, g_ref, tb_ref, o_ref):
    rows = q_ref.shape[1] // GRID_W
    nkeys = NA_WIN_ROWS * GRID_W
    lane = lax.broadcasted_iota(jnp.int32, (GRID_W, LANES), 1)
    first = lane < HEAD_DIM

    def body(step, carry):
        units = []
        for i in range(NA_UNROLL):
            r = step * NA_UNROLL + i
            r0 = jnp.clip(r - NA_WIN_ROWS // 2, 0, rows - NA_WIN_ROWS)
            st = r0 - r + NA_WIN_ROWS - 1
            qs = pl.ds(pl.multiple_of(r * GRID_W, GRID_W), GRID_W)
            ks = pl.ds(pl.multiple_of(r0 * GRID_W, GRID_W), nkeys)
            q = q_ref[0, qs, :]
            kw = k_ref[0, ks, :]
            zero = jnp.zeros_like(q)
            q2 = jnp.concatenate([jnp.where(first, q, zero), jnp.where(first, zero, q)], axis=0)
            sc = lax.dot_general(q2, kw, (((1,), (1,)), ((), ())),
                                 preferred_element_type=F32)
            units.append((qs, ks, sc + tb_ref[0, st]))
        probs = []
        for qs, ks, sc in units:
            mx = jnp.max(sc, axis=-1, keepdims=True)
            e = jnp.exp2(sc - mx)
            probs.append((e.astype(BF16), jnp.sum(e, axis=-1, keepdims=True)))
        for (qs, ks, _), (p, den) in zip(units, probs):
            o2 = jnp.dot(p, v_ref[0, ks, :], preferred_element_type=F32) / den
            o = jnp.where(first, o2[0:GRID_W], o2[GRID_W:])
            o_ref[0, qs, :] = (o * _silu(g_ref[0, qs, :])).astype(o_ref.dtype)
        return carry

    lax.fori_loop(0, rows // NA_UNROLL, body, 0)


def _na_bias_table(rpb):
    h = rpb.shape[0]
    cq = np.arange(GRID_W)[:, None]
    ck = np.arange(GRID_W)[None, :]
    ws = np.clip(cq - NA_WIN_COLS // 2, 0, GRID_W - NA_WIN_COLS)
    in_win = (ck >= ws) & (ck < ws + NA_WIN_COLS)
    line = 2 * GRID_W
    lo = GRID_W - NA_WIN_COLS
    ext = jnp.pad(rpb.astype(F32) * LOG2E, ((0, 0), (0, 0), (lo, line - lo - rpb.shape[-1])))
    skew = jnp.broadcast_to(ext[:, :, None, :], ext.shape[:2] + (GRID_W, line))
    skew = skew.reshape(ext.shape[:2] + (GRID_W * line,))[:, :, :GRID_W * (line - 1)]
    toep = skew.reshape(ext.shape[:2] + (GRID_W, line - 1))[:, :, :, GRID_W - 1:]
    colb = jnp.where(in_win, toep, -jnp.inf)
    tb = jnp.stack([colb[:, st:st + NA_WIN_ROWS] for st in range(NA_WIN_ROWS)], axis=1)
    tb = tb.reshape(h // 2, 2, NA_WIN_ROWS, NA_WIN_ROWS, GRID_W, GRID_W)
    tb = tb.transpose(0, 2, 1, 4, 3, 5).reshape(h // 2, NA_WIN_ROWS, 2 * GRID_W, NA_WIN_ROWS * GRID_W)
    return tb


def _na_attention(qkvg, tb):
    b, s, _ = qkvg.shape
    npair = NA_WIDTH // LANES

    def col(off):
        return pl.BlockSpec((1, s, LANES), lambda hp, bi: (bi, 0, off + hp))

    return pl.pallas_call(
        _na_kernel,
        grid=(npair, b),
        in_specs=[col(0), col(npair), col(2 * npair), col(3 * npair),
                  pl.BlockSpec((1,) + tb.shape[1:], lambda hp, bi: (hp, 0, 0, 0))],
        out_specs=pl.BlockSpec((1, s, LANES), lambda hp, bi: (bi, 0, hp)),
        out_shape=jax.ShapeDtypeStruct((b, s, NA_WIDTH), BF16),
        compiler_params=_params("parallel", "parallel"),
        name="na_attention",
    )(qkvg, qkvg, qkvg, qkvg, tb)


DIL_QB = 2 * DIL_BLK
DIL_KB = 4 * DIL_BLK
DIL_UNROLL = 8


def _dil_kernel(q0, k0, v0, q1, k1, v1, q2, k2, v2, g_ref, o_ref, acc_ref, m_ref, w_ref):
    s = o_ref.shape[1]
    lane = lax.broadcasted_iota(jnp.int32, (DIL_QB, LANES), 1)
    first = lane < HEAD_DIM
    half_off = lane % (LANES // 2)
    q_first = (half_off < ROPE_HALF) | ((half_off >= 2 * ROPE_HALF) & (lane < LANES // 2))
    v_first = lax.broadcasted_iota(jnp.int32, (DIL_KB, LANES), 1) < HEAD_DIM
    qi = lax.broadcasted_iota(jnp.int32, (DIL_QB, DIL_KB), 0)
    ki = lax.broadcasted_iota(jnp.int32, (DIL_QB, DIL_KB), 1)

    for g, (q_ref, k_ref, v_ref) in enumerate(((q0, k0, v0), (q1, k1, v1), (q2, k2, v2))):
        d = DIL_PAIRS[g][1]
        l = s // d
        nu = l // DIL_QB

        def trip(step, carry, g=g, d=d, l=l, nu=nu, q_ref=q_ref, k_ref=k_ref, v_ref=v_ref):
            units = []
            for i in range(DIL_UNROLL):
                idx = step * DIL_UNROLL + i
                rho = idx // nu
                u = idx % nu
                q_start = pl.multiple_of(u * DIL_QB, DIL_QB)
                k_start = pl.multiple_of(jnp.clip(u * DIL_QB - DIL_BLK, 0, l - DIL_KB), DIL_BLK)
                q = q_ref[0, rho, pl.ds(q_start, DIL_QB), :]
                kw = k_ref[0, rho, pl.ds(k_start, DIL_KB), :]
                valid = jnp.abs((ki + k_start) - (qi + q_start)) <= DIL_BLK
                zero = jnp.zeros_like(q)
                scs = []
                for h in range(2):
                    qh = jnp.where(q_first, q, zero) if h == 0 else jnp.where(q_first, zero, q)
                    sc = lax.dot_general(qh, kw, (((1,), (1,)), ((), ())),
                                         preferred_element_type=F32)
                    scs.append(jnp.where(valid, sc, -jnp.inf))
                units.append((rho, q_start, k_start, scs))
            soft = []
            for rho, q_start, k_start, scs in units:
                es, mxs = [], []
                for sc in scs:
                    mx = jnp.max(sc, axis=-1, keepdims=True)
                    es.append(jnp.exp2(sc - mx).astype(BF16))
                    mxs.append(mx)
                soft.append((es, jnp.where(first, mxs[0], mxs[1])))
            for (rho, q_start, k_start, _), (es, mx) in zip(units, soft):
                vw = v_ref[0, rho, pl.ds(k_start, DIL_KB), :]
                one = jnp.ones_like(vw)
                o0 = jnp.dot(es[0], jnp.where(v_first, vw, one), preferred_element_type=F32)
                o1 = jnp.dot(es[1], jnp.where(v_first, one, vw), preferred_element_type=F32)
                num = jnp.where(first, o0, o1)
                den = pltpu.roll(jnp.where(first, o1, o0), HEAD_DIM, 1)
                if d == 1:
                    rows = pl.ds(q_start, DIL_QB)
                else:
                    rows = pl.ds(rho + d * q_start, DIL_QB, stride=d)
                if g == 0:
                    acc_ref[rows, :] = num
                    m_ref[rows, :] = mx
                    w_ref[rows, :] = den
                else:
                    m_old = m_ref[rows, :]
                    m_new = jnp.maximum(m_old, mx)
                    a = jnp.exp2(m_old - m_new)
                    bw = jnp.exp2(mx - m_new)
                    acc_ref[rows, :] = acc_ref[rows, :] * a + num * bw
                    w_ref[rows, :] = w_ref[rows, :] * a + den * bw
                    m_ref[rows, :] = m_new
            return carry

        lax.fori_loop(0, d * nu // DIL_UNROLL, trip, 0)

    def finish(i, carry):
        rows = pl.ds(pl.multiple_of(i * 256, 256), 256)
        y = acc_ref[rows, :] / w_ref[rows, :]
        o_ref[0, rows, :] = (y * _silu(g_ref[0, rows, :])).astype(o_ref.dtype)
        return carry

    lax.fori_loop(0, s // 256, finish, 0)


def _dil_attention(qkvs, gate, gate_off):
    b = gate.shape[0]
    s = gate.shape[1]
    npair = DIL_GROUP_WIDTH // LANES
    in_specs = []
    args = []
    for arr in qkvs:
        d, l = arr.shape[1], arr.shape[2]
        for part in range(3):
            in_specs.append(pl.BlockSpec((1, d, l, LANES),
                                         lambda bi, sp, part=part: (bi, 0, 0, part * npair + sp)))
            args.append(arr)
    in_specs.append(pl.BlockSpec((1, s, LANES), lambda bi, sp: (bi, 0, gate_off + sp)))
    args.append(gate)
    return pl.pallas_call(
        _dil_kernel,
        grid=(b, npair),
        in_specs=in_specs,
        out_specs=pl.BlockSpec((1, s, LANES), lambda bi, sp: (bi, 0, sp)),
        out_shape=jax.ShapeDtypeStruct((b, s, DIL_GROUP_WIDTH), BF16),
        scratch_shapes=[pltpu.VMEM((s, LANES), F32)] * 3,
        compiler_params=_params("parallel", "parallel"),
        name="dil_attention",
    )(*args)


def _split3(x):
    hi = x.astype(BF16)
    r1 = x - hi.astype(F32)
    mid = r1.astype(BF16)
    lo = (r1 - mid.astype(F32)).astype(BF16)
    return hi, mid, lo


SSM_HALO = 16
SSM_DT_COPIES = 2
SSM_UNROLL = 4


def _ssd_kernel(xs_ref, bm_ref, cm_ref, dt_ref, wx_ref, wb_ref, wc_ref, bx_ref, bb_ref, bc_ref,
                dtb_ref, a_ref, dsk_ref, y_ref, xs_s, bt_s, c_s, cb_s, acum_s, pt_s, st_s):
    s = xs_ref.shape[1]
    cl = SSM_CHUNK
    nchunk = s // cl
    hpg = SSM_HEADS_PER_GROUP
    npair = SSM_GROUP_WIDTH // LANES
    ncol = 2 * hpg

    def conv_chunk(c, carry):
        base = pl.multiple_of(c * cl, cl)
        prev_start = pl.multiple_of(jnp.maximum(base - SSM_HALO, 0), SSM_HALO)
        next_start = pl.multiple_of(jnp.minimum(base + cl, s - SSM_HALO), SSM_HALO)
        has_prev = jnp.where(c > 0, 1.0, 0.0)
        has_next = jnp.where(c < nchunk - 1, 1.0, 0.0)

        def conv(src, w_ref, bias_ref):
            ext = jnp.concatenate([src[0, pl.ds(prev_start, SSM_HALO), :].astype(F32) * has_prev,
                                   src[0, pl.ds(base, cl), :].astype(F32),
                                   src[0, pl.ds(next_start, SSM_HALO), :].astype(F32) * has_next], axis=0)
            acc = bias_ref[0]
            off = SSM_HALO - SSM_CONV // 2
            for tap in range(SSM_CONV):
                acc = acc + ext[off + tap:off + tap + cl, :] * w_ref[0, tap:tap + 1, :]
            return _silu(acc)

        xs_s[pl.ds(base, cl), :] = conv(xs_ref, wx_ref, bx_ref)
        bmat = conv(bm_ref, wb_ref, bb_ref).astype(BF16)
        cmat = conv(cm_ref, wc_ref, bc_ref).astype(BF16)
        bt_s[c] = bmat.astype(F32).T.astype(BF16)
        c_s[pl.ds(base, cl), :] = cmat
        cb_s[c] = lax.dot_general(cmat, bmat, (((1,), (1,)), ((), ())),
                                  preferred_element_type=F32)

        dt = jax.nn.softplus(dt_ref[0, pl.ds(base, cl), :] + dt_bias)
        pieces = _split3(dt * a_neg)
        acum_f = sum(jnp.dot(tril, p, preferred_element_type=F32) for p in pieces)
        acum_b = sum(jnp.dot(triu, p, preferred_element_type=F32) for p in pieces)
        acum = jnp.where(fwd_lane, acum_f, acum_b)
        a_end = jnp.where(fwd_lane[0:1], acum[cl - 1:cl, :], acum[0:1, :])
        w_state = dt * jnp.exp(a_end - acum)
        packed_t = jnp.where(lane < ncol, acum * LOG2E - jnp.log2(dt), w_state).T
        acum_s[c] = acum * LOG2E
        pt_s[c] = packed_t[0:SSM_DT_COPIES * ncol, :]
        return carry

    ri = lax.broadcasted_iota(jnp.int32, (cl, cl), 0)
    ci = lax.broadcasted_iota(jnp.int32, (cl, cl), 1)
    lane = lax.broadcasted_iota(jnp.int32, (cl, LANES), 1)
    first = lane < HEAD_DIM
    fwd_lane = (lane % ncol) < hpg
    tril = jnp.where(ci <= ri, 1.0, 0.0).astype(BF16)
    triu = jnp.where(ci >= ri, 1.0, 0.0).astype(BF16)
    dt_bias = dtb_ref[0]
    a_neg = a_ref[0]

    lax.fori_loop(0, nchunk, conv_chunk, 0, unroll=4)

    for direction in range(2):
        if direction == 0:
            keep = ci <= ri
            end_row = cl - 1
        else:
            keep = ci >= ri
            end_row = 0

        st_s[...] = jnp.zeros_like(st_s)

        def local_part(c, direction=direction, keep=keep, end_row=end_row):
            rows = pl.ds(pl.multiple_of(c * cl, cl), cl)
            acum = acum_s[c]
            packed_t = pt_s[c]

            xs = xs_s[rows, :]
            bt = bt_s[c].astype(F32)
            cb = cb_s[c]
            yd_blocks, new_blocks, scale_blocks = [], [], []
            for pr in range(npair):
                xblk = xs[:, pr * LANES:(pr + 1) * LANES].astype(BF16)
                lhs, cols = [], []
                for j in (direction * hpg + 2 * pr, direction * hpg + 2 * pr + 1):
                    col = jnp.broadcast_to(acum[:, j:j + 1], (cl, cl))
                    row = packed_t[j:j + 1, :]
                    ws_row = packed_t[ncol + j:ncol + j + 1, :]
                    seg_dt = jnp.exp2(jnp.where(keep, col - row, -jnp.inf))
                    lhs.append((cb * seg_dt).astype(BF16))
                    lhs.append((bt * ws_row).astype(BF16))
                    cols.append(col)
                prod = jnp.dot(jnp.concatenate(lhs, axis=0), xblk, preferred_element_type=F32)
                yd_blocks.append(jnp.where(first, prod[0:cl], prod[2 * cl:3 * cl]))
                new_blocks.append(jnp.where(first, prod[cl:2 * cl], prod[3 * cl:4 * cl]))
                scale_blocks.append(jnp.exp2(jnp.where(first, cols[0], cols[1])))
            return rows, xs, yd_blocks, new_blocks, scale_blocks

        def trip(step, carry, direction=direction, end_row=end_row):
            chunks = [step * SSM_UNROLL + i for i in range(SSM_UNROLL)]
            if direction == 1:
                chunks = [nchunk - 1 - c for c in chunks]
            parts = [local_part(c) for c in chunks]
            state = [st_s[:, pr * LANES:(pr + 1) * LANES] for pr in range(npair)]
            for rows, xs, yd_blocks, new_blocks, scale_blocks in parts:
                cmat = c_s[rows, :]
                y_blocks = []
                for pr in range(npair):
                    y_off = jnp.dot(cmat, state[pr].astype(BF16), preferred_element_type=F32)
                    y_blocks.append(yd_blocks[pr] + y_off * scale_blocks[pr])
                    state[pr] = (state[pr] * scale_blocks[pr][end_row:end_row + 1, :]
                                 + new_blocks[pr])
                y = jnp.concatenate(y_blocks, axis=1)
                if direction == 0:
                    y_ref[0, rows, :] = y + dsk_ref[0] * xs
                else:
                    y_ref[0, rows, :] = y_ref[0, rows, :] + y
            for pr in range(npair):
                st_s[:, pr * LANES:(pr + 1) * LANES] = state[pr]
            return carry

        lax.fori_loop(0, nchunk // SSM_UNROLL, trip, 0)


def _ssd(xbc, dtp, conv_w, conv_b, dt_bias, a_neg, d_skip):
    b, s, _ = xbc.shape
    gw = SSM_GROUP_WIDTH
    b_off = SSM_INNER // LANES
    c_off = b_off + SSM_GROUPS
    nchunk = s // SSM_CHUNK

    def seq(width, off):
        return pl.BlockSpec((1, s, width), lambda bi, g: (bi, 0, off + g))

    def par(rows, width, off):
        return pl.BlockSpec((1, rows, width), lambda bi, g: (0, 0, off + g))

    return pl.pallas_call(
        _ssd_kernel,
        grid=(b, SSM_GROUPS),
        in_specs=[seq(gw, 0), seq(LANES, b_off), seq(LANES, c_off), seq(LANES, 0),
                  par(SSM_CONV, gw, 0), par(SSM_CONV, LANES, b_off), par(SSM_CONV, LANES, c_off),
                  par(1, gw, 0), par(1, LANES, b_off), par(1, LANES, c_off),
                  pl.BlockSpec((1, 1, LANES), lambda bi, g: (g, 0, 0)),
                  pl.BlockSpec((1, 1, LANES), lambda bi, g: (g, 0, 0)),
                  pl.BlockSpec((1, 1, gw), lambda bi, g: (g, 0, 0))],
        out_specs=pl.BlockSpec((1, s, gw), lambda bi, g: (bi, 0, g)),
        out_shape=jax.ShapeDtypeStruct((b, s, SSM_INNER), F32),
        scratch_shapes=[pltpu.VMEM((s, gw), F32),
                        pltpu.VMEM((nchunk, SSM_STATE, SSM_CHUNK), BF16),
                        pltpu.VMEM((s, LANES), BF16),
                        pltpu.VMEM((nchunk, SSM_CHUNK, SSM_CHUNK), F32),
                        pltpu.VMEM((nchunk, SSM_CHUNK, LANES), F32),
                        pltpu.VMEM((nchunk, SSM_DT_COPIES * 2 * SSM_HEADS_PER_GROUP, SSM_CHUNK), F32),
                        pltpu.VMEM((SSM_STATE, gw), F32)],
        compiler_params=_params("parallel", "parallel"),
        name="ssd",
    )(xbc, xbc, xbc, dtp, conv_w, conv_w, conv_w, conv_b, conv_b, conv_b, dt_bias, a_neg, d_skip)


def _tail_kernel(x_ref, ya_ref, yb_ref, yc_ref, z_ref, ua_ref, ub_ref, uc_ref, p_ref,
                 nw_ref, woa_ref, wob_ref, woc_ref, wout_ref, pg_ref, wpg_ref, wple_ref, gn_ref,
                 o_ref, *h_ref, final):
    def mm(a, w_ref):
        return jnp.dot(a, w_ref[...], preferred_element_type=F32)

    ya = mm(ya_ref[...], woa_ref)
    yb = mm(yb_ref[...], wob_ref)
    yc_in = _rms(yc_ref[...] * _silu(z_ref[...]), nw_ref[...]).astype(BF16)
    yc = mm(yc_in, woc_ref)
    merged = (_sigmoid(ua_ref[...]) * ya + _sigmoid(ub_ref[...]) * yb
              + _sigmoid(uc_ref[...]) * yc)
    x1 = x_ref[...] + mm(merged.astype(BF16), wout_ref)
    gate = jax.nn.sigmoid(mm(_rms(x1, pg_ref[...]).astype(BF16), wpg_ref))
    x2 = x1 + mm(p_ref[...].astype(BF16), wple_ref) * gate
    if final:
        o_ref[...] = _rms(x2, gn_ref[...])
    else:
        o_ref[...] = x2
        h_ref[0][...] = _rms(x2, gn_ref[...]).astype(BF16)


MISC_WIDTH = SSM_INNER + DIL_GROUP_WIDTH + 3 * D_MODEL
MISC_GB_LANE_BLOCK = SSM_INNER // LANES
MISC_U_BLOCK = (SSM_INNER + DIL_GROUP_WIDTH) // D_MODEL


def _tail(x2d, ya, yb, yc, misc, p2d, nw, woa, wob, woc, wout, pg, wpg, wple, gn, final, tm=512):
    m = x2d.shape[0]
    out_specs = pl.BlockSpec((tm, D_MODEL), lambda i: (i, 0))
    out_shape = jax.ShapeDtypeStruct((m, D_MODEL), F32)
    if not final:
        out_specs = [out_specs, pl.BlockSpec((tm, D_MODEL), lambda i: (i, 0))]
        out_shape = [out_shape, jax.ShapeDtypeStruct((m, D_MODEL), BF16)]

    def rows(width, off=0):
        return pl.BlockSpec((tm, width), lambda i: (i, off))

    def whole(arr):
        return pl.BlockSpec(arr.shape, lambda i: (0, 0), pipeline_mode=pl.Buffered(1))

    ub = MISC_U_BLOCK
    return pl.pallas_call(
        functools.partial(_tail_kernel, final=final),
        grid=(m // tm,),
        in_specs=[rows(D_MODEL), rows(NA_WIDTH), rows(DIL_GROUP_WIDTH), rows(SSM_INNER),
                  rows(SSM_INNER, 0), rows(D_MODEL, ub), rows(D_MODEL, ub + 1), rows(D_MODEL, ub + 2),
                  rows(PLE_DIM),
                  whole(nw), whole(woa), whole(wob), whole(woc), whole(wout), whole(pg), whole(wpg),
                  whole(wple), whole(gn)],
        out_specs=out_specs,
        out_shape=out_shape,
        compiler_params=_params("parallel"),
        name="tail",
    )(x2d, ya, yb, yc, misc, misc, misc, misc, p2d, nw, woa, wob, woc, wout, pg, wpg, wple, gn)


def _prep_weights(w_in, conv_w, conv_b, a_log, dt_bias, d_skip):
    depth = w_in.shape[0]
    offs = np.concatenate([[0], np.cumsum(IN_SPLITS)])
    w16 = w_in.astype(BF16)
    (qa, ka, va, ga, qb, kb, vb, gb, xbc, z, dtr, ua, ub, uc) = [
        w16[:, :, int(offs[i]):int(offs[i + 1])] for i in range(len(IN_SPLITS))]
    w_a = w16[:, :, :int(offs[4])]
    gw = DIL_GROUP_WIDTH
    w_b = [jnp.concatenate([_rotary_layout(qb[:, :, g * gw:(g + 1) * gw]),
                            _rotary_layout(kb[:, :, g * gw:(g + 1) * gw]),
                            vb[:, :, g * gw:(g + 1) * gw]], axis=2)
           for g in range(len(DIL_PAIRS))]
    w_misc = jnp.concatenate([z, gb, ua, ub, uc], axis=2)

    hpg = SSM_HEADS_PER_GROUP

    def per_group(t):
        lead = t.shape[:-2]
        t = t.reshape(lead + (2, SSM_GROUPS, hpg))
        t = jnp.moveaxis(t, -2, -3).reshape(lead + (SSM_GROUPS, 2 * hpg))
        t = jnp.tile(t, (1,) * (t.ndim - 1) + (SSM_DT_COPIES,))
        pad = [(0, 0)] * (t.ndim - 1) + [(0, LANES - 2 * hpg * SSM_DT_COPIES)]
        return jnp.pad(t, pad).reshape(lead + (SSM_GROUPS * LANES,))

    w_dt = per_group(dtr.reshape(depth, D_MODEL, 2, SSM_HEADS))
    w_xbc = xbc
    dtb = per_group(dt_bias.astype(F32)).reshape(depth, SSM_GROUPS, 1, LANES)
    a_neg = per_group(-jnp.exp(a_log.astype(F32))).reshape(depth, SSM_GROUPS, 1, LANES)
    dsk = jnp.repeat(d_skip.astype(F32), HEAD_DIM, axis=1).reshape(depth, SSM_GROUPS, 1, SSM_GROUP_WIDTH)
    cw = conv_w.astype(F32).reshape(depth, 1, SSM_CONV, SSM_CONV_DIM)
    cbias = conv_b.astype(F32).reshape(depth, 1, 1, SSM_CONV_DIM)
    return w_a, w_b, w_misc, w_xbc, w_dt, dtb, a_neg, dsk, cw, cbias


def kernel(x, p, norm_w, w_in, na_rpb, conv_w, conv_b, a_log, dt_bias, d_skip, ssm_norm_w,
           w_oa, w_ob, w_oc, w_out, ple_norm_w, w_ple, w_ple_gate, final_norm_w):
    b, s, dm = x.shape
    depth = w_in.shape[0]
    m = b * s
    w_a, w_b, w_misc, w_xbc, w_dt, dtb, a_neg, dsk, cw, cbias = _prep_weights(
        w_in, conv_w, conv_b, a_log, dt_bias, d_skip)
    tabs = [_rotary_tables(s, d) for _, d in DIL_PAIRS]
    bias_tables = _na_bias_table(na_rpb.reshape((-1,) + na_rpb.shape[2:]))
    bias_tables = bias_tables.reshape((depth, NA_HEADS // 2) + bias_tables.shape[1:])
    row = lambda v: v.astype(F32).reshape(1, -1)

    x2d = x.reshape(m, dm)
    h = _norm(x2d, row(norm_w[0]))
    for i in range(depth):
        final = i == depth - 1
        qkvg = _proj(h, w_a[i], 2 * NA_WIDTH, BF16,
                     lead_scale=(NA_WIDTH, SCORE_SCALE)).reshape(b, s, 4 * NA_WIDTH)
        misc = _proj(h, w_misc[i], MISC_WIDTH // 2, BF16)
        xbc = _proj(h, w_xbc[i], SSM_CONV_DIM, BF16).reshape(b, s, SSM_CONV_DIM)
        dtp = _proj(h, w_dt[i], SSM_GROUPS * LANES, F32).reshape(b, s, SSM_GROUPS * LANES)
        h3 = h.reshape(b, s, dm)
        qkvs = [_proj_dil(h3, w_b[gi][i], tabs[gi], d) for gi, (_, d) in enumerate(DIL_PAIRS)]

        ya = _na_attention(qkvg, bias_tables[i]).reshape(m, NA_WIDTH)
        yb = _dil_attention(qkvs, misc.reshape(b, s, MISC_WIDTH), MISC_GB_LANE_BLOCK)
        yb = yb.reshape(m, DIL_GROUP_WIDTH)
        yc = _ssd(xbc, dtp, cw[i], cbias[i], dtb[i], a_neg[i], dsk[i]).reshape(m, SSM_INNER)

        gn = row(final_norm_w) if final else row(norm_w[i + 1])
        out = _tail(x2d, ya, yb, yc, misc, p[i].reshape(m, PLE_DIM), row(ssm_norm_w[i]),
                    w_oa[i].astype(BF16), w_ob[i].astype(BF16), w_oc[i].astype(BF16),
                    w_out[i].astype(BF16), row(ple_norm_w[i]), w_ple_gate[i].astype(BF16),
                    w_ple[i].astype(BF16), gn, final=final)
        if final:
            x2d = out
        else:
            x2d, h = out
    return x2d.reshape(b, s, dm)
```

```python
import functools
import math

import numpy as np
import jax
import jax.numpy as jnp
from jax import lax
from jax.experimental import pallas as pl
from jax.experimental.pallas import tpu as pltpu

F32 = jnp.float32
BF16 = jnp.bfloat16

LANES = 128
VMEM_LIMIT_BYTES = 56 * 1024 * 1024

D_MODEL = 1024
GRID_W = 64
HEAD_DIM = 64
EPS = 1e-6
PLE_DIM = 256

NA_HEADS = 16
NA_WIDTH = NA_HEADS * HEAD_DIM
NA_WIN_ROWS = 8
NA_WIN_COLS = 16

DIL_PAIRS = ((128, 1), (512, 4), (2048, 16))
DIL_HEADS_PER_GROUP = 8
DIL_GROUP_WIDTH = DIL_HEADS_PER_GROUP * HEAD_DIM
DIL_WIDTH = DIL_GROUP_WIDTH * len(DIL_PAIRS)
DIL_BLK = 64
ROPE_THETA = 500000.0
ROPE_DIM = HEAD_DIM // 4
ROPE_HALF = ROPE_DIM // 2

SSM_INNER = 1536
SSM_HEADS = 24
SSM_GROUPS = 4
SSM_HEADS_PER_GROUP = SSM_HEADS // SSM_GROUPS
SSM_GROUP_WIDTH = SSM_HEADS_PER_GROUP * HEAD_DIM
SSM_STATE = 128
SSM_CONV = 5
SSM_CHUNK = 128
SSM_CONV_DIM = SSM_INNER + 2 * SSM_GROUPS * SSM_STATE

IN_SPLITS = (NA_WIDTH, NA_WIDTH, NA_WIDTH, NA_WIDTH,
             DIL_WIDTH, DIL_WIDTH, DIL_WIDTH, DIL_GROUP_WIDTH,
             SSM_CONV_DIM, SSM_INNER, 2 * SSM_HEADS,
             D_MODEL, D_MODEL, D_MODEL)

LOG2E = math.log2(math.e)
SCORE_SCALE = HEAD_DIM ** -0.5 * LOG2E


def _params(*semantics):
    return pltpu.CompilerParams(dimension_semantics=semantics,
                                vmem_limit_bytes=VMEM_LIMIT_BYTES)


def _rms(x, g):
    return x * lax.rsqrt(jnp.mean(x * x, axis=-1, keepdims=True) + EPS) * g


def _silu(x):
    x = x.astype(F32)
    return x * jax.nn.sigmoid(x)


def _sigmoid(x):
    return jax.nn.sigmoid(x.astype(F32))


def _norm_kernel(x_ref, g_ref, h_ref):
    h_ref[...] = _rms(x_ref[...], g_ref[...]).astype(h_ref.dtype)


def _norm(x2d, g, tm=1024):
    m, k = x2d.shape
    return pl.pallas_call(
        _norm_kernel,
        grid=(m // tm,),
        in_specs=[pl.BlockSpec((tm, k), lambda i: (i, 0)), pl.BlockSpec((1, k), lambda i: (0, 0))],
        out_specs=pl.BlockSpec((tm, k), lambda i: (i, 0)),
        out_shape=jax.ShapeDtypeStruct((m, k), BF16),
        compiler_params=_params("parallel"),
        name="norm",
    )(x2d, g)


def _proj_kernel(h_ref, w_ref, *rest):
    o_ref = rest[-1]
    acc = jnp.dot(h_ref[...], w_ref[...], preferred_element_type=F32)
    if len(rest) == 2:
        acc = acc * rest[0][...]
    o_ref[...] = acc.astype(o_ref.dtype)


def _proj(h2d, w, tn, out_dtype, tm=1024, col_scale=None):
    m, k = h2d.shape
    n = w.shape[1]
    in_specs = [pl.BlockSpec((tm, k), lambda i, j: (i, 0)),
                pl.BlockSpec((k, tn), lambda i, j: (0, j))]
    args = [h2d, w]
    if col_scale is not None:
        in_specs.append(pl.BlockSpec((1, tn), lambda i, j: (0, j)))
        args.append(col_scale)
    return pl.pallas_call(
        _proj_kernel,
        grid=(m // tm, n // tn),
        in_specs=in_specs,
        out_specs=pl.BlockSpec((tm, tn), lambda i, j: (i, j)),
        out_shape=jax.ShapeDtypeStruct((m, n), out_dtype),
        compiler_params=_params("parallel", "arbitrary"),
        name="proj",
    )(*args)


MXU_N = 256


REGROUP_ROWS = 256


def _regroup_rows(h, d):
    tm = h.shape[0]
    if d == 1:
        return h
    per = REGROUP_ROWS // d
    r = lax.broadcasted_iota(jnp.int32, (REGROUP_ROWS, REGROUP_ROWS), 0)
    c = lax.broadcasted_iota(jnp.int32, (REGROUP_ROWS, REGROUP_ROWS), 1)
    perm = jnp.where(c == (r % per) * d + r // per, 1.0, 0.0).astype(BF16)
    nblk = tm // REGROUP_ROWS
    blocks = [jnp.dot(perm, h[i * REGROUP_ROWS:(i + 1) * REGROUP_ROWS],
                      preferred_element_type=F32).astype(BF16) for i in range(nblk)]
    return jnp.concatenate([blocks[i][rho * per:(rho + 1) * per]
                            for rho in range(d) for i in range(nblk)], axis=0)


def _proj_dil_kernel(h_ref, w_ref, c_ref, s_ref, o_ref):
    nres, rows = o_ref.shape[1:3]
    k, n = w_ref.shape
    tm = nres * rows
    gw = DIL_GROUP_WIDTH
    hp = _regroup_rows(h_ref[0], nres)
    for nb in range(n // MXU_N):
        lo = nb * MXU_N
        acc = jnp.dot(hp, w_ref[:, lo:lo + MXU_N], preferred_element_type=F32)
        part = lo // gw
        if part == 2:
            o_ref[0, :, :, lo:lo + MXU_N] = acc.astype(o_ref.dtype).reshape(nres, rows, MXU_N)
            continue
        c = c_ref[part].reshape(tm, LANES)
        sn = s_ref[part].reshape(tm, LANES)
        for cb in range(MXU_N // LANES):
            blk = acc[:, cb * LANES:(cb + 1) * LANES]
            rot = blk * c + pltpu.roll(blk, LANES // 2, 1) * sn
            o_ref[0, :, :, lo + cb * LANES:lo + (cb + 1) * LANES] = (
                rot.astype(o_ref.dtype).reshape(nres, rows, LANES))


def _proj_dil(h3, w, tabs, d, tm=1024):
    b, s, k = h3.shape
    n = w.shape[1]
    rows = tm // d
    c, sn = tabs
    tab_spec = pl.BlockSpec((2, d, rows, LANES), lambda bi, i: (0, 0, i, 0))
    return pl.pallas_call(
        _proj_dil_kernel,
        grid=(b, s // tm),
        in_specs=[pl.BlockSpec((1, tm, k), lambda bi, i: (bi, i, 0)),
                  pl.BlockSpec((k, n), lambda bi, i: (0, 0)),
                  tab_spec, tab_spec],
        out_specs=pl.BlockSpec((1, d, rows, n), lambda bi, i: (bi, 0, i, 0)),
        out_shape=jax.ShapeDtypeStruct((b, d, s // d, n), BF16),
        compiler_params=_params("parallel", "parallel"),
        name=f"proj_dil{d}",
    )(h3, w, c, sn)


def _rotary_tables(s, d):
    inv = ROPE_THETA ** (-jnp.arange(0, ROPE_DIM, 2, dtype=F32) / ROPE_DIM)
    ang = jnp.arange(s).astype(F32)[:, None] * inv[None, :]
    cos, sin = jnp.cos(ang), jnp.sin(ang)
    rest = LANES // 2 - 2 * ROPE_HALF
    ones = jnp.ones((s, rest), F32)
    zeros = jnp.zeros((s, rest), F32)
    c = jnp.concatenate([cos, cos, ones, cos, cos, ones], axis=1)
    sn = jnp.concatenate([-sin, -sin, zeros, sin, sin, zeros], axis=1)

    def arrange(t):
        t = t.reshape(s // d, d, LANES).transpose(1, 0, 2)
        return jnp.stack([t * SCORE_SCALE, t])

    return arrange(c), arrange(sn)


def _rotary_layout(t):
    lead = t.shape[:-1]
    t = t.reshape(lead + (t.shape[-1] // LANES, 2, HEAD_DIM))
    npair = t.shape[-3]
    a = t[..., :, :, 0:ROPE_HALF].reshape(lead + (npair, 2 * ROPE_HALF))
    b = t[..., :, :, ROPE_HALF:ROPE_DIM].reshape(lead + (npair, 2 * ROPE_HALF))
    rest0 = t[..., :, 0, ROPE_DIM:]
    rest1 = t[..., :, 1, ROPE_DIM:]
    return jnp.concatenate([a, rest0, b, rest1], axis=-1).reshape(lead + (npair * LANES,))


NA_UNROLL = 16


def _na_kernel(q_ref, k_ref, v_ref, g_ref, tb_ref, o_ref):
    rows = q_ref.shape[1] // GRID_W
    nkeys = NA_WIN_ROWS * GRID_W
    lane = lax.broadcasted_iota(jnp.int32, (GRID_W, LANES), 1)
    first = lane < HEAD_DIM

    def body(step, carry):
        units = []
        for i in range(NA_UNROLL):
            r = step * NA_UNROLL + i
            r0 = jnp.clip(r - NA_WIN_ROWS // 2, 0, rows - NA_WIN_ROWS)
            st = r0 - r + NA_WIN_ROWS - 1
            qs = pl.ds(pl.multiple_of(r * GRID_W, GRID_W), GRID_W)
            ks = pl.ds(pl.multiple_of(r0 * GRID_W, GRID_W), nkeys)
            q = q_ref[0, qs, :]
            kw = k_ref[0, ks, :]
            zero = jnp.zeros_like(q)
            q2 = jnp.concatenate([jnp.where(first, q, zero), jnp.where(first, zero, q)], axis=0)
            sc = lax.dot_general(q2, kw, (((1,), (1,)), ((), ())),
                                 preferred_element_type=F32)
            units.append((qs, ks, sc + tb_ref[0, st]))
        probs = []
        for qs, ks, sc in units:
            mx = jnp.max(sc, axis=-1, keepdims=True)
            e = jnp.exp2(sc - mx)
            probs.append((e.astype(BF16), jnp.sum(e, axis=-1, keepdims=True)))
        for (qs, ks, _), (p, den) in zip(units, probs):
            o2 = jnp.dot(p, v_ref[0, ks, :], preferred_element_type=F32) / den
            o = jnp.where(first, o2[0:GRID_W], o2[GRID_W:])
            o_ref[0, qs, :] = (o * _silu(g_ref[0, qs, :])).astype(o_ref.dtype)
        return carry

    lax.fori_loop(0, rows // NA_UNROLL, body, 0)


def _na_bias_table(rpb):
    h = rpb.shape[0]
    cq = np.arange(GRID_W)[:, None]
    ck = np.arange(GRID_W)[None, :]
    ws = np.clip(cq - NA_WIN_COLS // 2, 0, GRID_W - NA_WIN_COLS)
    in_win = (ck >= ws) & (ck < ws + NA_WIN_COLS)
    line = 2 * GRID_W
    lo = GRID_W - NA_WIN_COLS
    ext = jnp.pad(rpb.astype(F32) * LOG2E, ((0, 0), (0, 0), (lo, line - lo - rpb.shape[-1])))
    skew = jnp.broadcast_to(ext[:, :, None, :], ext.shape[:2] + (GRID_W, line))
    skew = skew.reshape(ext.shape[:2] + (GRID_W * line,))[:, :, :GRID_W * (line - 1)]
    toep = skew.reshape(ext.shape[:2] + (GRID_W, line - 1))[:, :, :, GRID_W - 1:]
    colb = jnp.where(in_win, toep, -jnp.inf)
    tb = jnp.stack([colb[:, st:st + NA_WIN_ROWS] for st in range(NA_WIN_ROWS)], axis=1)
    tb = tb.reshape(h // 2, 2, NA_WIN_ROWS, NA_WIN_ROWS, GRID_W, GRID_W)
    tb = tb.transpose(0, 2, 1, 4, 3, 5).reshape(h // 2, NA_WIN_ROWS, 2 * GRID_W, NA_WIN_ROWS * GRID_W)
    return tb


def _na_attention(qkvg, tb):
    b, s, _ = qkvg.shape
    npair = NA_WIDTH // LANES

    def col(off):
        return pl.BlockSpec((1, s, LANES), lambda hp, bi: (bi, 0, off + hp))

    return pl.pallas_call(
        _na_kernel,
        grid=(npair, b),
        in_specs=[col(0), col(npair), col(2 * npair), col(3 * npair),
                  pl.BlockSpec((1,) + tb.shape[1:], lambda hp, bi: (hp, 0, 0, 0))],
        out_specs=pl.BlockSpec((1, s, LANES), lambda hp, bi: (bi, 0, hp)),
        out_shape=jax.ShapeDtypeStruct((b, s, NA_WIDTH), BF16),
        compiler_params=_params("parallel", "parallel"),
        name="na_attention",
    )(qkvg, qkvg, qkvg, qkvg, tb)


DIL_QB = 2 * DIL_BLK
DIL_KB = 4 * DIL_BLK
DIL_UNROLL = 8


def _dil_kernel(q0, k0, v0, q1, k1, v1, q2, k2, v2, g_ref, o_ref, acc_ref, m_ref, w_ref):
    s = o_ref.shape[1]
    lane = lax.broadcasted_iota(jnp.int32, (DIL_QB, LANES), 1)
    first = lane < HEAD_DIM
    half_off = lane % (LANES // 2)
    q_first = (half_off < ROPE_HALF) | ((half_off >= 2 * ROPE_HALF) & (lane < LANES // 2))
    v_first = lax.broadcasted_iota(jnp.int32, (DIL_KB, LANES), 1) < HEAD_DIM
    qi = lax.broadcasted_iota(jnp.int32, (DIL_QB, DIL_KB), 0)
    ki = lax.broadcasted_iota(jnp.int32, (DIL_QB, DIL_KB), 1)

    groups = ((q0, k0, v0), (q1, k1, v1), (q2, k2, v2))
    first_group = len(groups) - 1
    for g in range(first_group, -1, -1):
        q_ref, k_ref, v_ref = groups[g]
        d = DIL_PAIRS[g][1]
        l = s // d
        nu = l // DIL_QB

        def trip(step, carry, g=g, d=d, l=l, nu=nu, q_ref=q_ref, k_ref=k_ref, v_ref=v_ref):
            units = []
            for i in range(DIL_UNROLL):
                idx = step * DIL_UNROLL + i
                rho = idx // nu
                u = idx % nu
                q_start = pl.multiple_of(u * DIL_QB, DIL_QB)
                k_start = pl.multiple_of(jnp.clip(u * DIL_QB - DIL_BLK, 0, l - DIL_KB), DIL_BLK)
                q = q_ref[0, rho, pl.ds(q_start, DIL_QB), :]
                kw = k_ref[0, rho, pl.ds(k_start, DIL_KB), :]
                valid = jnp.abs((ki + k_start) - (qi + q_start)) <= DIL_BLK
                zero = jnp.zeros_like(q)
                scs = []
                for h in range(2):
                    qh = jnp.where(q_first, q, zero) if h == 0 else jnp.where(q_first, zero, q)
                    sc = lax.dot_general(qh, kw, (((1,), (1,)), ((), ())),
                                         preferred_element_type=F32)
                    scs.append(jnp.where(valid, sc, -jnp.inf))
                units.append((rho, q_start, k_start, scs))
            soft = []
            for rho, q_start, k_start, scs in units:
                es, mxs = [], []
                for sc in scs:
                    mx = jnp.max(sc, axis=-1, keepdims=True)
                    es.append(jnp.exp2(sc - mx).astype(BF16))
                    mxs.append(mx)
                soft.append((es, jnp.where(first, mxs[0], mxs[1])))
            for (rho, q_start, k_start, _), (es, mx) in zip(units, soft):
                vw = v_ref[0, rho, pl.ds(k_start, DIL_KB), :]
                one = jnp.ones_like(vw)
                o0 = jnp.dot(es[0], jnp.where(v_first, vw, one), preferred_element_type=F32)
                o1 = jnp.dot(es[1], jnp.where(v_first, one, vw), preferred_element_type=F32)
                num = jnp.where(first, o0, o1)
                den = pltpu.roll(jnp.where(first, o1, o0), HEAD_DIM, 1)
                if d == 1:
                    rows = pl.ds(q_start, DIL_QB)
                else:
                    rows = pl.ds(rho + d * q_start, DIL_QB, stride=d)
                if g == first_group:
                    acc_ref[rows, :] = num
                    m_ref[rows, :] = mx
                    w_ref[rows, :] = den
                else:
                    m_old = m_ref[rows, :]
                    m_new = jnp.maximum(m_old, mx)
                    a = jnp.exp2(m_old - m_new)
                    bw = jnp.exp2(mx - m_new)
                    acc_ref[rows, :] = acc_ref[rows, :] * a + num * bw
                    w_ref[rows, :] = w_ref[rows, :] * a + den * bw
                    m_ref[rows, :] = m_new
            return carry

        lax.fori_loop(0, d * nu // DIL_UNROLL, trip, 0)

    def finish(i, carry):
        rows = pl.ds(pl.multiple_of(i * 256, 256), 256)
        y = acc_ref[rows, :] / w_ref[rows, :]
        o_ref[0, rows, :] = (y * _silu(g_ref[0, rows, :])).astype(o_ref.dtype)
        return carry

    lax.fori_loop(0, s // 256, finish, 0)


def _dil_attention(qkvs, gate, gate_off):
    b = gate.shape[0]
    s = gate.shape[1]
    npair = DIL_GROUP_WIDTH // LANES
    in_specs = []
    args = []
    for arr in qkvs:
        d, l = arr.shape[1], arr.shape[2]
        for part in range(3):
            in_specs.append(pl.BlockSpec((1, d, l, LANES),
                                         lambda bi, sp, part=part: (bi, 0, 0, part * npair + sp)))
            args.append(arr)
    in_specs.append(pl.BlockSpec((1, s, LANES), lambda bi, sp: (bi, 0, gate_off + sp)))
    args.append(gate)
    return pl.pallas_call(
        _dil_kernel,
        grid=(b, npair),
        in_specs=in_specs,
        out_specs=pl.BlockSpec((1, s, LANES), lambda bi, sp: (bi, 0, sp)),
        out_shape=jax.ShapeDtypeStruct((b, s, DIL_GROUP_WIDTH), BF16),
        scratch_shapes=[pltpu.VMEM((s, LANES), F32)] * 3,
        compiler_params=_params("parallel", "parallel"),
        name="dil_attention",
    )(*args)


def _split3(x):
    hi = x.astype(BF16)
    r1 = x - hi.astype(F32)
    mid = r1.astype(BF16)
    lo = (r1 - mid.astype(F32)).astype(BF16)
    return hi, mid, lo


SSM_HALO = 16
SSM_DT_COPIES = 2
SSM_UNROLL = 4


def _ssd_kernel(xs_ref, bm_ref, cm_ref, dt_ref, wx_ref, wb_ref, wc_ref, bx_ref, bb_ref, bc_ref,
                dtb_ref, a_ref, dsk_ref, y_ref, xs_s, bt_s, c_s, cb_s, acum_s, pt_s, st_s):
    s = xs_ref.shape[1]
    cl = SSM_CHUNK
    nchunk = s // cl
    hpg = SSM_HEADS_PER_GROUP
    npair = SSM_GROUP_WIDTH // LANES
    ncol = 2 * hpg

    def conv_chunk(c, carry):
        base = pl.multiple_of(c * cl, cl)
        prev_start = pl.multiple_of(jnp.maximum(base - SSM_HALO, 0), SSM_HALO)
        next_start = pl.multiple_of(jnp.minimum(base + cl, s - SSM_HALO), SSM_HALO)
        has_prev = jnp.where(c > 0, 1.0, 0.0)
        has_next = jnp.where(c < nchunk - 1, 1.0, 0.0)

        def conv(src, w_ref, bias_ref):
            ext = jnp.concatenate([src[0, pl.ds(prev_start, SSM_HALO), :].astype(F32) * has_prev,
                                   src[0, pl.ds(base, cl), :].astype(F32),
                                   src[0, pl.ds(next_start, SSM_HALO), :].astype(F32) * has_next], axis=0)
            acc = bias_ref[0]
            off = SSM_HALO - SSM_CONV // 2
            for tap in range(SSM_CONV):
                acc = acc + ext[off + tap:off + tap + cl, :] * w_ref[0, tap:tap + 1, :]
            return _silu(acc)

        xs_s[pl.ds(base, cl), :] = conv(xs_ref, wx_ref, bx_ref)
        bmat = conv(bm_ref, wb_ref, bb_ref).astype(BF16)
        cmat = conv(cm_ref, wc_ref, bc_ref).astype(BF16)
        bt_s[c] = bmat.astype(F32).T.astype(BF16)
        c_s[pl.ds(base, cl), :] = cmat
        cb_s[c] = lax.dot_general(cmat, bmat, (((1,), (1,)), ((), ())),
                                  preferred_element_type=F32)

        dt = jax.nn.softplus(dt_ref[0, pl.ds(base, cl), :] + dt_bias)
        pieces = _split3(dt * a_neg)
        acum_f = sum(jnp.dot(tril, p, preferred_element_type=F32) for p in pieces)
        acum_b = sum(jnp.dot(triu, p, preferred_element_type=F32) for p in pieces)
        acum = jnp.where(fwd_lane, acum_f, acum_b)
        a_end = jnp.where(fwd_lane[0:1], acum[cl - 1:cl, :], acum[0:1, :])
        w_state = dt * jnp.exp(a_end - acum)
        packed_t = jnp.where(lane < ncol, acum * LOG2E - jnp.log2(dt), w_state).T
        acum_s[c] = acum * LOG2E
        pt_s[c] = packed_t[0:SSM_DT_COPIES * ncol, :]
        return carry

    ri = lax.broadcasted_iota(jnp.int32, (cl, cl), 0)
    ci = lax.broadcasted_iota(jnp.int32, (cl, cl), 1)
    lane = lax.broadcasted_iota(jnp.int32, (cl, LANES), 1)
    first = lane < HEAD_DIM
    fwd_lane = (lane % ncol) < hpg
    tril = jnp.where(ci <= ri, 1.0, 0.0).astype(BF16)
    triu = jnp.where(ci >= ri, 1.0, 0.0).astype(BF16)
    dt_bias = dtb_ref[0]
    a_neg = a_ref[0]

    lax.fori_loop(0, nchunk, conv_chunk, 0, unroll=4)

    for direction in range(2):
        if direction == 0:
            keep = ci <= ri
            end_row = cl - 1
        else:
            keep = ci >= ri
            end_row = 0

        st_s[...] = jnp.zeros_like(st_s)

        def local_part(c, direction=direction, keep=keep, end_row=end_row):
            rows = pl.ds(pl.multiple_of(c * cl, cl), cl)
            acum = acum_s[c]
            packed_t = pt_s[c]

            xs = xs_s[rows, :]
            bt = bt_s[c].astype(F32)
            cb = cb_s[c]
            yd_blocks, new_blocks, scale_blocks = [], [], []
            for pr in range(npair):
                xblk = xs[:, pr * LANES:(pr + 1) * LANES].astype(BF16)
                lhs, cols = [], []
                for j in (direction * hpg + 2 * pr, direction * hpg + 2 * pr + 1):
                    col = jnp.broadcast_to(acum[:, j:j + 1], (cl, cl))
                    row = packed_t[j:j + 1, :]
                    ws_row = packed_t[ncol + j:ncol + j + 1, :]
                    seg_dt = jnp.exp2(jnp.where(keep, col - row, -jnp.inf))
                    lhs.append((cb * seg_dt).astype(BF16))
                    lhs.append((bt * ws_row).astype(BF16))
                    cols.append(col)
                prod = jnp.dot(jnp.concatenate(lhs, axis=0), xblk, preferred_element_type=F32)
                yd_blocks.append(jnp.where(first, prod[0:cl], prod[2 * cl:3 * cl]))
                new_blocks.append(jnp.where(first, prod[cl:2 * cl], prod[3 * cl:4 * cl]))
                scale_blocks.append(jnp.exp2(jnp.where(first, cols[0], cols[1])))
            return rows, xs, yd_blocks, new_blocks, scale_blocks

        def trip(step, carry, direction=direction, end_row=end_row):
            chunks = [step * SSM_UNROLL + i for i in range(SSM_UNROLL)]
            if direction == 1:
                chunks = [nchunk - 1 - c for c in chunks]
            parts = [local_part(c) for c in chunks]
            state = [st_s[:, pr * LANES:(pr + 1) * LANES] for pr in range(npair)]
            for rows, xs, yd_blocks, new_blocks, scale_blocks in parts:
                cmat = c_s[rows, :]
                y_blocks = []
                for pr in range(npair):
                    y_off = jnp.dot(cmat, state[pr].astype(BF16), preferred_element_type=F32)
                    y_blocks.append(yd_blocks[pr] + y_off * scale_blocks[pr])
                    state[pr] = (state[pr] * scale_blocks[pr][end_row:end_row + 1, :]
                                 + new_blocks[pr])
                y = jnp.concatenate(y_blocks, axis=1)
                if direction == 0:
                    y_ref[0, rows, :] = y + dsk_ref[0] * xs
                else:
                    y_ref[0, rows, :] = y_ref[0, rows, :] + y
            for pr in range(npair):
                st_s[:, pr * LANES:(pr + 1) * LANES] = state[pr]
            return carry

        lax.fori_loop(0, nchunk // SSM_UNROLL, trip, 0)


def _ssd(xbc, dtp, conv_w, conv_b, dt_bias, a_neg, d_skip):
    b, s, _ = xbc.shape
    gw = SSM_GROUP_WIDTH
    b_off = SSM_INNER // LANES
    c_off = b_off + SSM_GROUPS
    nchunk = s // SSM_CHUNK

    def seq(width, off):
        return pl.BlockSpec((1, s, width), lambda bi, g: (bi, 0, off + g))

    def par(rows, width, off):
        return pl.BlockSpec((1, rows, width), lambda bi, g: (0, 0, off + g))

    return pl.pallas_call(
        _ssd_kernel,
        grid=(b, SSM_GROUPS),
        in_specs=[seq(gw, 0), seq(LANES, b_off), seq(LANES, c_off), seq(LANES, 0),
                  par(SSM_CONV, gw, 0), par(SSM_CONV, LANES, b_off), par(SSM_CONV, LANES, c_off),
                  par(1, gw, 0), par(1, LANES, b_off), par(1, LANES, c_off),
                  pl.BlockSpec((1, 1, LANES), lambda bi, g: (g, 0, 0)),
                  pl.BlockSpec((1, 1, LANES), lambda bi, g: (g, 0, 0)),
                  pl.BlockSpec((1, 1, gw), lambda bi, g: (g, 0, 0))],
        out_specs=pl.BlockSpec((1, s, gw), lambda bi, g: (bi, 0, g)),
        out_shape=jax.ShapeDtypeStruct((b, s, SSM_INNER), F32),
        scratch_shapes=[pltpu.VMEM((s, gw), F32),
                        pltpu.VMEM((nchunk, SSM_STATE, SSM_CHUNK), BF16),
                        pltpu.VMEM((s, LANES), BF16),
                        pltpu.VMEM((nchunk, SSM_CHUNK, SSM_CHUNK), F32),
                        pltpu.VMEM((nchunk, SSM_CHUNK, LANES), F32),
                        pltpu.VMEM((nchunk, SSM_DT_COPIES * 2 * SSM_HEADS_PER_GROUP, SSM_CHUNK), F32),
                        pltpu.VMEM((SSM_STATE, gw), F32)],
        compiler_params=_params("parallel", "parallel"),
        name="ssd",
    )(xbc, xbc, xbc, dtp, conv_w, conv_w, conv_w, conv_b, conv_b, conv_b, dt_bias, a_neg, d_skip)


def _tail_kernel(x_ref, ya_ref, yb_ref, yc_ref, z_ref, ua_ref, ub_ref, uc_ref, p_ref,
                 nw_ref, woa_ref, wob_ref, woc_ref, wout_ref, pg_ref, wpg_ref, wple_ref, gn_ref,
                 o_ref, *h_ref, final):
    def mm(a, w_ref):
        return jnp.dot(a, w_ref[...], preferred_element_type=F32)

    ya = mm(ya_ref[...], woa_ref)
    yb = mm(yb_ref[...], wob_ref)
    yc_in = _rms(yc_ref[...] * _silu(z_ref[...]), nw_ref[...]).astype(BF16)
    yc = mm(yc_in, woc_ref)
    merged = (_sigmoid(ua_ref[...]) * ya + _sigmoid(ub_ref[...]) * yb
              + _sigmoid(uc_ref[...]) * yc)
    x1 = x_ref[...] + mm(merged.astype(BF16), wout_ref)
    gate = jax.nn.sigmoid(mm(_rms(x1, pg_ref[...]).astype(BF16), wpg_ref))
    x2 = x1 + mm(p_ref[...].astype(BF16), wple_ref) * gate
    if final:
        o_ref[...] = _rms(x2, gn_ref[...])
    else:
        o_ref[...] = x2
        h_ref[0][...] = _rms(x2, gn_ref[...]).astype(BF16)


MISC_WIDTH = SSM_INNER + DIL_GROUP_WIDTH + 3 * D_MODEL
MISC_GB_LANE_BLOCK = SSM_INNER // LANES
MISC_U_BLOCK = (SSM_INNER + DIL_GROUP_WIDTH) // D_MODEL


def _tail(x2d, ya, yb, yc, misc, p2d, nw, woa, wob, woc, wout, pg, wpg, wple, gn, final, tm=512):
    m = x2d.shape[0]
    out_specs = pl.BlockSpec((tm, D_MODEL), lambda i: (i, 0))
    out_shape = jax.ShapeDtypeStruct((m, D_MODEL), F32)
    if not final:
        out_specs = [out_specs, pl.BlockSpec((tm, D_MODEL), lambda i: (i, 0))]
        out_shape = [out_shape, jax.ShapeDtypeStruct((m, D_MODEL), BF16)]

    def rows(width, off=0):
        return pl.BlockSpec((tm, width), lambda i: (i, off))

    def whole(arr):
        return pl.BlockSpec(arr.shape, lambda i: (0, 0), pipeline_mode=pl.Buffered(1))

    ub = MISC_U_BLOCK
    return pl.pallas_call(
        functools.partial(_tail_kernel, final=final),
        grid=(m // tm,),
        in_specs=[rows(D_MODEL), rows(NA_WIDTH), rows(DIL_GROUP_WIDTH), rows(SSM_INNER),
                  rows(SSM_INNER, 0), rows(D_MODEL, ub), rows(D_MODEL, ub + 1), rows(D_MODEL, ub + 2),
                  rows(PLE_DIM),
                  whole(nw), whole(woa), whole(wob), whole(woc), whole(wout), whole(pg), whole(wpg),
                  whole(wple), whole(gn)],
        out_specs=out_specs,
        out_shape=out_shape,
        compiler_params=_params("parallel"),
        name="tail",
    )(x2d, ya, yb, yc, misc, misc, misc, misc, p2d, nw, woa, wob, woc, wout, pg, wpg, wple, gn)


def _prep_weights(w_in, conv_w, conv_b, a_log, dt_bias, d_skip):
    depth = w_in.shape[0]
    offs = np.concatenate([[0], np.cumsum(IN_SPLITS)])
    w16 = w_in.astype(BF16)
    (qa, ka, va, ga, qb, kb, vb, gb, xbc, z, dtr, ua, ub, uc) = [
        w16[:, :, int(offs[i]):int(offs[i + 1])] for i in range(len(IN_SPLITS))]
    w_a = w16[:, :, :int(offs[4])]
    gw = DIL_GROUP_WIDTH
    w_b = [jnp.concatenate([_rotary_layout(qb[:, :, g * gw:(g + 1) * gw]),
                            _rotary_layout(kb[:, :, g * gw:(g + 1) * gw]),
                            vb[:, :, g * gw:(g + 1) * gw]], axis=2)
           for g in range(len(DIL_PAIRS))]
    w_misc = jnp.concatenate([z, gb, ua, ub, uc], axis=2)

    hpg = SSM_HEADS_PER_GROUP

    def per_group(t):
        lead = t.shape[:-2]
        t = t.reshape(lead + (2, SSM_GROUPS, hpg))
        t = jnp.moveaxis(t, -2, -3).reshape(lead + (SSM_GROUPS, 2 * hpg))
        t = jnp.tile(t, (1,) * (t.ndim - 1) + (SSM_DT_COPIES,))
        pad = [(0, 0)] * (t.ndim - 1) + [(0, LANES - 2 * hpg * SSM_DT_COPIES)]
        return jnp.pad(t, pad).reshape(lead + (SSM_GROUPS * LANES,))

    w_dt = per_group(dtr.reshape(depth, D_MODEL, 2, SSM_HEADS))
    w_xbc = xbc
    dtb = per_group(dt_bias.astype(F32)).reshape(depth, SSM_GROUPS, 1, LANES)
    a_neg = per_group(-jnp.exp(a_log.astype(F32))).reshape(depth, SSM_GROUPS, 1, LANES)
    dsk = jnp.repeat(d_skip.astype(F32), HEAD_DIM, axis=1).reshape(depth, SSM_GROUPS, 1, SSM_GROUP_WIDTH)
    cw = conv_w.astype(F32).reshape(depth, 1, SSM_CONV, SSM_CONV_DIM)
    cbias = conv_b.astype(F32).reshape(depth, 1, 1, SSM_CONV_DIM)
    return w_a, w_b, w_misc, w_xbc, w_dt, dtb, a_neg, dsk, cw, cbias


def kernel(x, p, norm_w, w_in, na_rpb, conv_w, conv_b, a_log, dt_bias, d_skip, ssm_norm_w,
           w_oa, w_ob, w_oc, w_out, ple_norm_w, w_ple, w_ple_gate, final_norm_w):
    b, s, dm = x.shape
    depth = w_in.shape[0]
    m = b * s
    w_a, w_b, w_misc, w_xbc, w_dt, dtb, a_neg, dsk, cw, cbias = _prep_weights(
        w_in, conv_w, conv_b, a_log, dt_bias, d_skip)
    tabs = [_rotary_tables(s, d) for _, d in DIL_PAIRS]
    na_col_scale = jnp.concatenate([jnp.full((1, NA_WIDTH), SCORE_SCALE, F32),
                                    jnp.ones((1, 3 * NA_WIDTH), F32)], axis=1)
    bias_tables = _na_bias_table(na_rpb.reshape((-1,) + na_rpb.shape[2:]))
    bias_tables = bias_tables.reshape((depth, NA_HEADS // 2) + bias_tables.shape[1:])
    row = lambda v: v.astype(F32).reshape(1, -1)

    x2d = x.reshape(m, dm)
    h = _norm(x2d, row(norm_w[0]))
    for i in range(depth):
        final = i == depth - 1
        qkvg = _proj(h, w_a[i], 2 * NA_WIDTH, BF16, col_scale=na_col_scale).reshape(b, s, 4 * NA_WIDTH)
        misc = _proj(h, w_misc[i], MISC_WIDTH // 2, BF16)
        xbc = _proj(h, w_xbc[i], SSM_CONV_DIM, BF16).reshape(b, s, SSM_CONV_DIM)
        dtp = _proj(h, w_dt[i], SSM_GROUPS * LANES, F32).reshape(b, s, SSM_GROUPS * LANES)
        h3 = h.reshape(b, s, dm)
        qkvs = [_proj_dil(h3, w_b[gi][i], tabs[gi], d) for gi, (_, d) in enumerate(DIL_PAIRS)]

        ya = _na_attention(qkvg, bias_tables[i]).reshape(m, NA_WIDTH)
        yb = _dil_attention(qkvs, misc.reshape(b, s, MISC_WIDTH), MISC_GB_LANE_BLOCK)
        yb = yb.reshape(m, DIL_GROUP_WIDTH)
        yc = _ssd(xbc, dtp, cw[i], cbias[i], dtb[i], a_neg[i], dsk[i]).reshape(m, SSM_INNER)

        gn = row(final_norm_w) if final else row(norm_w[i + 1])
        out = _tail(x2d, ya, yb, yc, misc, p[i].reshape(m, PLE_DIM), row(ssm_norm_w[i]),
                    w_oa[i].astype(BF16), w_ob[i].astype(BF16), w_oc[i].astype(BF16),
                    w_out[i].astype(BF16), row(ple_norm_w[i]), w_ple_gate[i].astype(BF16),
                    w_ple[i].astype(BF16), gn, final=final)
        if final:
            x2d = out
        else:
            x2d, h = out
    return x2d.reshape(b, s, dm)
```

```python
import functools
import math

import numpy as np
import jax
import jax.numpy as jnp
from jax import lax
from jax.experimental import pallas as pl
from jax.experimental.pallas import tpu as pltpu

F32 = jnp.float32
BF16 = jnp.bfloat16

LANES = 128
VMEM_LIMIT_BYTES = 56 * 1024 * 1024

D_MODEL = 1024
GRID_W = 64
HEAD_DIM = 64
EPS = 1e-6
PLE_DIM = 256

NA_HEADS = 16
NA_WIDTH = NA_HEADS * HEAD_DIM
NA_WIN_ROWS = 8
NA_WIN_COLS = 16

DIL_PAIRS = ((128, 1), (512, 4), (2048, 16))
DIL_HEADS_PER_GROUP = 8
DIL_GROUP_WIDTH = DIL_HEADS_PER_GROUP * HEAD_DIM
DIL_WIDTH = DIL_GROUP_WIDTH * len(DIL_PAIRS)
DIL_BLK = 64
ROPE_THETA = 500000.0
ROPE_DIM = HEAD_DIM // 4
ROPE_HALF = ROPE_DIM // 2

SSM_INNER = 1536
SSM_HEADS = 24
SSM_GROUPS = 4
SSM_HEADS_PER_GROUP = SSM_HEADS // SSM_GROUPS
SSM_GROUP_WIDTH = SSM_HEADS_PER_GROUP * HEAD_DIM
SSM_STATE = 128
SSM_CONV = 5
SSM_CHUNK = 128
SSM_CONV_DIM = SSM_INNER + 2 * SSM_GROUPS * SSM_STATE

IN_SPLITS = (NA_WIDTH, NA_WIDTH, NA_WIDTH, NA_WIDTH,
             DIL_WIDTH, DIL_WIDTH, DIL_WIDTH, DIL_GROUP_WIDTH,
             SSM_CONV_DIM, SSM_INNER, 2 * SSM_HEADS,
             D_MODEL, D_MODEL, D_MODEL)

LOG2E = math.log2(math.e)
SCORE_SCALE = HEAD_DIM ** -0.5 * LOG2E


def _params(*semantics):
    return pltpu.CompilerParams(dimension_semantics=semantics,
                                vmem_limit_bytes=VMEM_LIMIT_BYTES)


def _rms(x, g):
    return x * lax.rsqrt(jnp.mean(x * x, axis=-1, keepdims=True) + EPS) * g


def _silu(x):
    x = x.astype(F32)
    return x * jax.nn.sigmoid(x)


def _sigmoid(x):
    return jax.nn.sigmoid(x.astype(F32))


def _norm_kernel(x_ref, g_ref, h_ref):
    h_ref[...] = _rms(x_ref[...], g_ref[...]).astype(h_ref.dtype)


def _norm(x2d, g, tm=1024):
    m, k = x2d.shape
    return pl.pallas_call(
        _norm_kernel,
        grid=(m // tm,),
        in_specs=[pl.BlockSpec((tm, k), lambda i: (i, 0)), pl.BlockSpec((1, k), lambda i: (0, 0))],
        out_specs=pl.BlockSpec((tm, k), lambda i: (i, 0)),
        out_shape=jax.ShapeDtypeStruct((m, k), BF16),
        compiler_params=_params("parallel"),
        name="norm",
    )(x2d, g)


def _proj_kernel(h_ref, w_ref, *rest):
    o_ref = rest[-1]
    acc = jnp.dot(h_ref[...], w_ref[...], preferred_element_type=F32)
    if len(rest) == 2:
        acc = acc * rest[0][...]
    o_ref[...] = acc.astype(o_ref.dtype)


def _proj(h2d, w, tn, out_dtype, tm=1024, col_scale=None):
    m, k = h2d.shape
    n = w.shape[1]
    in_specs = [pl.BlockSpec((tm, k), lambda i, j: (i, 0)),
                pl.BlockSpec((k, tn), lambda i, j: (0, j))]
    args = [h2d, w]
    if col_scale is not None:
        in_specs.append(pl.BlockSpec((1, tn), lambda i, j: (0, j)))
        args.append(col_scale)
    return pl.pallas_call(
        _proj_kernel,
        grid=(m // tm, n // tn),
        in_specs=in_specs,
        out_specs=pl.BlockSpec((tm, tn), lambda i, j: (i, j)),
        out_shape=jax.ShapeDtypeStruct((m, n), out_dtype),
        compiler_params=_params("parallel", "arbitrary"),
        name="proj",
    )(*args)


MXU_N = 256


REGROUP_ROWS = 256


def _regroup_rows(h, d):
    tm = h.shape[0]
    if d == 1:
        return h
    per = REGROUP_ROWS // d
    r = lax.broadcasted_iota(jnp.int32, (REGROUP_ROWS, REGROUP_ROWS), 0)
    c = lax.broadcasted_iota(jnp.int32, (REGROUP_ROWS, REGROUP_ROWS), 1)
    perm = jnp.where(c == (r % per) * d + r // per, 1.0, 0.0).astype(BF16)
    nblk = tm // REGROUP_ROWS
    blocks = [jnp.dot(perm, h[i * REGROUP_ROWS:(i + 1) * REGROUP_ROWS],
                      preferred_element_type=F32).astype(BF16) for i in range(nblk)]
    return jnp.concatenate([blocks[i][rho * per:(rho + 1) * per]
                            for rho in range(d) for i in range(nblk)], axis=0)


def _proj_dil_kernel(h_ref, w_ref, c_ref, s_ref, o_ref):
    nres, rows = o_ref.shape[1:3]
    k, n = w_ref.shape
    tm = nres * rows
    gw = DIL_GROUP_WIDTH
    hp = _regroup_rows(h_ref[0], nres)
    for nb in range(n // MXU_N):
        lo = nb * MXU_N
        acc = jnp.dot(hp, w_ref[:, lo:lo + MXU_N], preferred_element_type=F32)
        part = lo // gw
        if part == 2:
            o_ref[0, :, :, lo:lo + MXU_N] = acc.astype(o_ref.dtype).reshape(nres, rows, MXU_N)
            continue
        c = c_ref[part].reshape(tm, LANES)
        sn = s_ref[part].reshape(tm, LANES)
        for cb in range(MXU_N // LANES):
            blk = acc[:, cb * LANES:(cb + 1) * LANES]
            rot = blk * c + pltpu.roll(blk, LANES // 2, 1) * sn
            o_ref[0, :, :, lo + cb * LANES:lo + (cb + 1) * LANES] = (
                rot.astype(o_ref.dtype).reshape(nres, rows, LANES))


def _proj_dil(h3, w, tabs, d, tm=1024):
    b, s, k = h3.shape
    n = w.shape[1]
    rows = tm // d
    c, sn = tabs
    tab_spec = pl.BlockSpec((2, d, rows, LANES), lambda bi, i: (0, 0, i, 0))
    return pl.pallas_call(
        _proj_dil_kernel,
        grid=(b, s // tm),
        in_specs=[pl.BlockSpec((1, tm, k), lambda bi, i: (bi, i, 0)),
                  pl.BlockSpec((k, n), lambda bi, i: (0, 0)),
                  tab_spec, tab_spec],
        out_specs=pl.BlockSpec((1, d, rows, n), lambda bi, i: (bi, 0, i, 0)),
        out_shape=jax.ShapeDtypeStruct((b, d, s // d, n), BF16),
        compiler_params=_params("parallel", "parallel"),
        name=f"proj_dil{d}",
    )(h3, w, c, sn)


def _rotary_tables(s, d):
    inv = ROPE_THETA ** (-jnp.arange(0, ROPE_DIM, 2, dtype=F32) / ROPE_DIM)
    ang = jnp.arange(s).astype(F32)[:, None] * inv[None, :]
    cos, sin = jnp.cos(ang), jnp.sin(ang)
    rest = LANES // 2 - 2 * ROPE_HALF
    ones = jnp.ones((s, rest), F32)
    zeros = jnp.zeros((s, rest), F32)
    c = jnp.concatenate([cos, cos, ones, cos, cos, ones], axis=1)
    sn = jnp.concatenate([-sin, -sin, zeros, sin, sin, zeros], axis=1)

    def arrange(t):
        t = t.reshape(s // d, d, LANES).transpose(1, 0, 2)
        return jnp.stack([t * SCORE_SCALE, t])

    return arrange(c), arrange(sn)


def _rotary_layout(t):
    lead = t.shape[:-1]
    t = t.reshape(lead + (t.shape[-1] // LANES, 2, HEAD_DIM))
    npair = t.shape[-3]
    a = t[..., :, :, 0:ROPE_HALF].reshape(lead + (npair, 2 * ROPE_HALF))
    b = t[..., :, :, ROPE_HALF:ROPE_DIM].reshape(lead + (npair, 2 * ROPE_HALF))
    rest0 = t[..., :, 0, ROPE_DIM:]
    rest1 = t[..., :, 1, ROPE_DIM:]
    return jnp.concatenate([a, rest0, b, rest1], axis=-1).reshape(lead + (npair * LANES,))


NA_UNROLL = 16


def _na_kernel(q_ref, k_ref, v_ref, g_ref, tb_ref, o_ref):
    rows = q_ref.shape[1] // GRID_W
    nkeys = NA_WIN_ROWS * GRID_W
    lane = lax.broadcasted_iota(jnp.int32, (GRID_W, LANES), 1)
    first = lane < HEAD_DIM

    def body(step, carry):
        units = []
        for i in range(NA_UNROLL):
            r = step * NA_UNROLL + i
            r0 = jnp.clip(r - NA_WIN_ROWS // 2, 0, rows - NA_WIN_ROWS)
            st = r0 - r + NA_WIN_ROWS - 1
            qs = pl.ds(pl.multiple_of(r * GRID_W, GRID_W), GRID_W)
            ks = pl.ds(pl.multiple_of(r0 * GRID_W, GRID_W), nkeys)
            q = q_ref[0, qs, :]
            kw = k_ref[0, ks, :]
            zero = jnp.zeros_like(q)
            q2 = jnp.concatenate([jnp.where(first, q, zero), jnp.where(first, zero, q)], axis=0)
            sc = lax.dot_general(q2, kw, (((1,), (1,)), ((), ())),
                                 preferred_element_type=F32)
            units.append((qs, ks, sc + tb_ref[0, st]))
        probs = []
        for qs, ks, sc in units:
            mx = jnp.max(sc, axis=-1, keepdims=True)
            e = jnp.exp2(sc - mx)
            probs.append((e.astype(BF16), jnp.sum(e, axis=-1, keepdims=True)))
        for (qs, ks, _), (p, den) in zip(units, probs):
            o2 = jnp.dot(p, v_ref[0, ks, :], preferred_element_type=F32) / den
            o = jnp.where(first, o2[0:GRID_W], o2[GRID_W:])
            o_ref[0, qs, :] = (o * _silu(g_ref[0, qs, :])).astype(o_ref.dtype)
        return carry

    lax.fori_loop(0, rows // NA_UNROLL, body, 0)


def _na_bias_table(rpb):
    h = rpb.shape[0]
    cq = np.arange(GRID_W)[:, None]
    ck = np.arange(GRID_W)[None, :]
    ws = np.clip(cq - NA_WIN_COLS // 2, 0, GRID_W - NA_WIN_COLS)
    in_win = (ck >= ws) & (ck < ws + NA_WIN_COLS)
    line = 2 * GRID_W
    lo = GRID_W - NA_WIN_COLS
    ext = jnp.pad(rpb.astype(F32) * LOG2E, ((0, 0), (0, 0), (lo, line - lo - rpb.shape[-1])))
    skew = jnp.broadcast_to(ext[:, :, None, :], ext.shape[:2] + (GRID_W, line))
    skew = skew.reshape(ext.shape[:2] + (GRID_W * line,))[:, :, :GRID_W * (line - 1)]
    toep = skew.reshape(ext.shape[:2] + (GRID_W, line - 1))[:, :, :, GRID_W - 1:]
    colb = jnp.where(in_win, toep, -jnp.inf)
    tb = jnp.stack([colb[:, st:st + NA_WIN_ROWS] for st in range(NA_WIN_ROWS)], axis=1)
    tb = tb.reshape(h // 2, 2, NA_WIN_ROWS, NA_WIN_ROWS, GRID_W, GRID_W)
    tb = tb.transpose(0, 2, 1, 4, 3, 5).reshape(h // 2, NA_WIN_ROWS, 2 * GRID_W, NA_WIN_ROWS * GRID_W)
    return tb


def _na_attention(qkvg, tb):
    b, s, _ = qkvg.shape
    npair = NA_WIDTH // LANES

    def col(off):
        return pl.BlockSpec((1, s, LANES), lambda hp, bi: (bi, 0, off + hp))

    return pl.pallas_call(
        _na_kernel,
        grid=(npair, b),
        in_specs=[col(0), col(npair), col(2 * npair), col(3 * npair),
                  pl.BlockSpec((1,) + tb.shape[1:], lambda hp, bi: (hp, 0, 0, 0))],
        out_specs=pl.BlockSpec((1, s, LANES), lambda hp, bi: (bi, 0, hp)),
        out_shape=jax.ShapeDtypeStruct((b, s, NA_WIDTH), BF16),
        compiler_params=_params("parallel", "parallel"),
        name="na_attention",
    )(qkvg, qkvg, qkvg, qkvg, tb)


DIL_QB = 2 * DIL_BLK
DIL_KB = 4 * DIL_BLK
DIL_UNROLL = 16


def _dil_kernel(q0, k0, v0, q1, k1, v1, q2, k2, v2, g_ref, o_ref, acc_ref, m_ref, w_ref):
    s = o_ref.shape[1]
    lane = lax.broadcasted_iota(jnp.int32, (DIL_QB, LANES), 1)
    first = lane < HEAD_DIM
    half_off = lane % (LANES // 2)
    q_first = (half_off < ROPE_HALF) | ((half_off >= 2 * ROPE_HALF) & (lane < LANES // 2))
    v_first = lax.broadcasted_iota(jnp.int32, (DIL_KB, LANES), 1) < HEAD_DIM
    qi = lax.broadcasted_iota(jnp.int32, (DIL_QB, DIL_KB), 0)
    ki = lax.broadcasted_iota(jnp.int32, (DIL_QB, DIL_KB), 1)

    groups = ((q0, k0, v0), (q1, k1, v1), (q2, k2, v2))
    first_group = len(groups) - 1
    for g in range(first_group, -1, -1):
        q_ref, k_ref, v_ref = groups[g]
        d = DIL_PAIRS[g][1]
        l = s // d
        nu = l // DIL_QB

        def trip(step, carry, g=g, d=d, l=l, nu=nu, q_ref=q_ref, k_ref=k_ref, v_ref=v_ref):
            units = []
            for i in range(DIL_UNROLL):
                idx = step * DIL_UNROLL + i
                rho = idx // nu
                u = idx % nu
                q_start = pl.multiple_of(u * DIL_QB, DIL_QB)
                k_start = pl.multiple_of(jnp.clip(u * DIL_QB - DIL_BLK, 0, l - DIL_KB), DIL_BLK)
                q = q_ref[0, rho, pl.ds(q_start, DIL_QB), :]
                kw = k_ref[0, rho, pl.ds(k_start, DIL_KB), :]
                valid = jnp.abs((ki + k_start) - (qi + q_start)) <= DIL_BLK
                zero = jnp.zeros_like(q)
                scs = []
                for h in range(2):
                    qh = jnp.where(q_first, q, zero) if h == 0 else jnp.where(q_first, zero, q)
                    sc = lax.dot_general(qh, kw, (((1,), (1,)), ((), ())),
                                         preferred_element_type=F32)
                    scs.append(jnp.where(valid, sc, -jnp.inf))
                units.append((rho, q_start, k_start, scs))
            soft = []
            for rho, q_start, k_start, scs in units:
                es, mxs = [], []
                for sc in scs:
                    mx = jnp.max(sc, axis=-1, keepdims=True)
                    es.append(jnp.exp2(sc - mx).astype(BF16))
                    mxs.append(mx)
                soft.append((es, jnp.where(first, mxs[0], mxs[1])))
            for (rho, q_start, k_start, _), (es, mx) in zip(units, soft):
                vw = v_ref[0, rho, pl.ds(k_start, DIL_KB), :]
                one = jnp.ones_like(vw)
                o0 = jnp.dot(es[0], jnp.where(v_first, vw, one), preferred_element_type=F32)
                o1 = jnp.dot(es[1], jnp.where(v_first, one, vw), preferred_element_type=F32)
                num = jnp.where(first, o0, o1)
                den = pltpu.roll(jnp.where(first, o1, o0), HEAD_DIM, 1)
                if d == 1:
                    rows = pl.ds(q_start, DIL_QB)
                else:
                    rows = pl.ds(rho + d * q_start, DIL_QB, stride=d)
                if g == first_group:
                    acc_ref[rows, :] = num
                    m_ref[rows, :] = mx
                    w_ref[rows, :] = den
                else:
                    m_old = m_ref[rows, :]
                    m_new = jnp.maximum(m_old, mx)
                    a = jnp.exp2(m_old - m_new)
                    bw = jnp.exp2(mx - m_new)
                    acc_ref[rows, :] = acc_ref[rows, :] * a + num * bw
                    w_ref[rows, :] = w_ref[rows, :] * a + den * bw
                    m_ref[rows, :] = m_new
            return carry

        lax.fori_loop(0, d * nu // DIL_UNROLL, trip, 0)

    def finish(i, carry):
        rows = pl.ds(pl.multiple_of(i * 256, 256), 256)
        y = acc_ref[rows, :] / w_ref[rows, :]
        o_ref[0, rows, :] = (y * _silu(g_ref[0, rows, :])).astype(o_ref.dtype)
        return carry

    lax.fori_loop(0, s // 256, finish, 0)


def _dil_attention(qkvs, gate, gate_off):
    b = gate.shape[0]
    s = gate.shape[1]
    npair = DIL_GROUP_WIDTH // LANES
    in_specs = []
    args = []
    for arr in qkvs:
        d, l = arr.shape[1], arr.shape[2]
        for part in range(3):
            in_specs.append(pl.BlockSpec((1, d, l, LANES),
                                         lambda bi, sp, part=part: (bi, 0, 0, part * npair + sp)))
            args.append(arr)
    in_specs.append(pl.BlockSpec((1, s, LANES), lambda bi, sp: (bi, 0, gate_off + sp)))
    args.append(gate)
    return pl.pallas_call(
        _dil_kernel,
        grid=(b, npair),
        in_specs=in_specs,
        out_specs=pl.BlockSpec((1, s, LANES), lambda bi, sp: (bi, 0, sp)),
        out_shape=jax.ShapeDtypeStruct((b, s, DIL_GROUP_WIDTH), BF16),
        scratch_shapes=[pltpu.VMEM((s, LANES), F32)] * 3,
        compiler_params=_params("parallel", "parallel"),
        name="dil_attention",
    )(*args)


def _split3(x):
    hi = x.astype(BF16)
    r1 = x - hi.astype(F32)
    mid = r1.astype(BF16)
    lo = (r1 - mid.astype(F32)).astype(BF16)
    return hi, mid, lo


SSM_HALO = 16
SSM_DT_COPIES = 2
SSM_UNROLL = 4


def _ssd_kernel(xs_ref, bm_ref, cm_ref, dt_ref, wx_ref, wb_ref, wc_ref, bx_ref, bb_ref, bc_ref,
                dtb_ref, a_ref, dsk_ref, y_ref, xs_s, bt_s, c_s, cb_s, acum_s, pt_s, st_s):
    s = xs_ref.shape[1]
    cl = SSM_CHUNK
    nchunk = s // cl
    hpg = SSM_HEADS_PER_GROUP
    npair = SSM_GROUP_WIDTH // LANES
    ncol = 2 * hpg

    def conv_chunk(c, carry):
        base = pl.multiple_of(c * cl, cl)
        prev_start = pl.multiple_of(jnp.maximum(base - SSM_HALO, 0), SSM_HALO)
        next_start = pl.multiple_of(jnp.minimum(base + cl, s - SSM_HALO), SSM_HALO)
        has_prev = jnp.where(c > 0, 1.0, 0.0)
        has_next = jnp.where(c < nchunk - 1, 1.0, 0.0)

        def conv(src, w_ref, bias_ref):
            ext = jnp.concatenate([src[0, pl.ds(prev_start, SSM_HALO), :].astype(F32) * has_prev,
                                   src[0, pl.ds(base, cl), :].astype(F32),
                                   src[0, pl.ds(next_start, SSM_HALO), :].astype(F32) * has_next], axis=0)
            acc = bias_ref[0]
            off = SSM_HALO - SSM_CONV // 2
            for tap in range(SSM_CONV):
                acc = acc + ext[off + tap:off + tap + cl, :] * w_ref[0, tap:tap + 1, :]
            return _silu(acc)

        xs_s[pl.ds(base, cl), :] = conv(xs_ref, wx_ref, bx_ref)
        bmat = conv(bm_ref, wb_ref, bb_ref).astype(BF16)
        cmat = conv(cm_ref, wc_ref, bc_ref).astype(BF16)
        bt_s[c] = bmat.astype(F32).T.astype(BF16)
        c_s[pl.ds(base, cl), :] = cmat
        cb_s[c] = lax.dot_general(cmat, bmat, (((1,), (1,)), ((), ())),
                                  preferred_element_type=F32)

        dt = jax.nn.softplus(dt_ref[0, pl.ds(base, cl), :] + dt_bias)
        pieces = _split3(dt * a_neg)
        acum_f = sum(jnp.dot(tril, p, preferred_element_type=F32) for p in pieces)
        acum_b = sum(jnp.dot(triu, p, preferred_element_type=F32) for p in pieces)
        acum = jnp.where(fwd_lane, acum_f, acum_b)
        a_end = jnp.where(fwd_lane[0:1], acum[cl - 1:cl, :], acum[0:1, :])
        w_state = dt * jnp.exp(a_end - acum)
        packed_t = jnp.where(lane < ncol, acum * LOG2E - jnp.log2(dt), w_state).T
        acum_s[c] = acum * LOG2E
        pt_s[c] = packed_t[0:SSM_DT_COPIES * ncol, :]
        return carry

    ri = lax.broadcasted_iota(jnp.int32, (cl, cl), 0)
    ci = lax.broadcasted_iota(jnp.int32, (cl, cl), 1)
    lane = lax.broadcasted_iota(jnp.int32, (cl, LANES), 1)
    first = lane < HEAD_DIM
    fwd_lane = (lane % ncol) < hpg
    tril = jnp.where(ci <= ri, 1.0, 0.0).astype(BF16)
    triu = jnp.where(ci >= ri, 1.0, 0.0).astype(BF16)
    dt_bias = dtb_ref[0]
    a_neg = a_ref[0]

    lax.fori_loop(0, nchunk, conv_chunk, 0, unroll=4)

    for direction in range(2):
        if direction == 0:
            keep = ci <= ri
            end_row = cl - 1
        else:
            keep = ci >= ri
            end_row = 0

        st_s[...] = jnp.zeros_like(st_s)

        def local_part(c, direction=direction, keep=keep, end_row=end_row):
            rows = pl.ds(pl.multiple_of(c * cl, cl), cl)
            acum = acum_s[c]
            packed_t = pt_s[c]

            xs = xs_s[rows, :]
            bt = bt_s[c].astype(F32)
            cb = cb_s[c]
            yd_blocks, new_blocks, scale_blocks = [], [], []
            for pr in range(npair):
                xblk = xs[:, pr * LANES:(pr + 1) * LANES].astype(BF16)
                lhs, cols = [], []
                for j in (direction * hpg + 2 * pr, direction * hpg + 2 * pr + 1):
                    col = jnp.broadcast_to(acum[:, j:j + 1], (cl, cl))
                    row = packed_t[j:j + 1, :]
                    ws_row = packed_t[ncol + j:ncol + j + 1, :]
                    seg_dt = jnp.exp2(jnp.where(keep, col - row, -jnp.inf))
                    lhs.append((cb * seg_dt).astype(BF16))
                    lhs.append((bt * ws_row).astype(BF16))
                    cols.append(col)
                prod = jnp.dot(jnp.concatenate(lhs, axis=0), xblk, preferred_element_type=F32)
                yd_blocks.append(jnp.where(first, prod[0:cl], prod[2 * cl:3 * cl]))
                new_blocks.append(jnp.where(first, prod[cl:2 * cl], prod[3 * cl:4 * cl]))
                scale_blocks.append(jnp.exp2(jnp.where(first, cols[0], cols[1])))
            return rows, xs, yd_blocks, new_blocks, scale_blocks

        def trip(step, carry, direction=direction, end_row=end_row):
            chunks = [step * SSM_UNROLL + i for i in range(SSM_UNROLL)]
            if direction == 1:
                chunks = [nchunk - 1 - c for c in chunks]
            parts = [local_part(c) for c in chunks]
            state = [st_s[:, pr * LANES:(pr + 1) * LANES] for pr in range(npair)]
            for rows, xs, yd_blocks, new_blocks, scale_blocks in parts:
                cmat = c_s[rows, :]
                y_blocks = []
                for pr in range(npair):
                    y_off = jnp.dot(cmat, state[pr].astype(BF16), preferred_element_type=F32)
                    y_blocks.append(yd_blocks[pr] + y_off * scale_blocks[pr])
                    state[pr] = (state[pr] * scale_blocks[pr][end_row:end_row + 1, :]
                                 + new_blocks[pr])
                y = jnp.concatenate(y_blocks, axis=1)
                if direction == 0:
                    y_ref[0, rows, :] = y + dsk_ref[0] * xs
                else:
                    y_ref[0, rows, :] = y_ref[0, rows, :] + y
            for pr in range(npair):
                st_s[:, pr * LANES:(pr + 1) * LANES] = state[pr]
            return carry

        lax.fori_loop(0, nchunk // SSM_UNROLL, trip, 0)


def _ssd(xbc, dtp, conv_w, conv_b, dt_bias, a_neg, d_skip):
    b, s, _ = xbc.shape
    gw = SSM_GROUP_WIDTH
    b_off = SSM_INNER // LANES
    c_off = b_off + SSM_GROUPS
    nchunk = s // SSM_CHUNK

    def seq(width, off):
        return pl.BlockSpec((1, s, width), lambda bi, g: (bi, 0, off + g))

    def par(rows, width, off):
        return pl.BlockSpec((1, rows, width), lambda bi, g: (0, 0, off + g))

    return pl.pallas_call(
        _ssd_kernel,
        grid=(b, SSM_GROUPS),
        in_specs=[seq(gw, 0), seq(LANES, b_off), seq(LANES, c_off), seq(LANES, 0),
                  par(SSM_CONV, gw, 0), par(SSM_CONV, LANES, b_off), par(SSM_CONV, LANES, c_off),
                  par(1, gw, 0), par(1, LANES, b_off), par(1, LANES, c_off),
                  pl.BlockSpec((1, 1, LANES), lambda bi, g: (g, 0, 0)),
                  pl.BlockSpec((1, 1, LANES), lambda bi, g: (g, 0, 0)),
                  pl.BlockSpec((1, 1, gw), lambda bi, g: (g, 0, 0))],
        out_specs=pl.BlockSpec((1, s, gw), lambda bi, g: (bi, 0, g)),
        out_shape=jax.ShapeDtypeStruct((b, s, SSM_INNER), F32),
        scratch_shapes=[pltpu.VMEM((s, gw), F32),
                        pltpu.VMEM((nchunk, SSM_STATE, SSM_CHUNK), BF16),
                        pltpu.VMEM((s, LANES), BF16),
                        pltpu.VMEM((nchunk, SSM_CHUNK, SSM_CHUNK), F32),
                        pltpu.VMEM((nchunk, SSM_CHUNK, LANES), F32),
                        pltpu.VMEM((nchunk, SSM_DT_COPIES * 2 * SSM_HEADS_PER_GROUP, SSM_CHUNK), F32),
                        pltpu.VMEM((SSM_STATE, gw), F32)],
        compiler_params=_params("parallel", "parallel"),
        name="ssd",
    )(xbc, xbc, xbc, dtp, conv_w, conv_w, conv_w, conv_b, conv_b, conv_b, dt_bias, a_neg, d_skip)


def _tail_kernel(x_ref, ya_ref, yb_ref, yc_ref, z_ref, ua_ref, ub_ref, uc_ref, p_ref,
                 nw_ref, woa_ref, wob_ref, woc_ref, wout_ref, pg_ref, wpg_ref, wple_ref, gn_ref,
                 o_ref, *h_ref, final):
    def mm(a, w_ref):
        return jnp.dot(a, w_ref[...], preferred_element_type=F32)

    ya = mm(ya_ref[...], woa_ref)
    yb = mm(yb_ref[...], wob_ref)
    yc_in = _rms(yc_ref[...] * _silu(z_ref[...]), nw_ref[...]).astype(BF16)
    yc = mm(yc_in, woc_ref)
    merged = (_sigmoid(ua_ref[...]) * ya + _sigmoid(ub_ref[...]) * yb
              + _sigmoid(uc_ref[...]) * yc)
    x1 = x_ref[...] + mm(merged.astype(BF16), wout_ref)
    gate = jax.nn.sigmoid(mm(_rms(x1, pg_ref[...]).astype(BF16), wpg_ref))
    x2 = x1 + mm(p_ref[...].astype(BF16), wple_ref) * gate
    if final:
        o_ref[...] = _rms(x2, gn_ref[...])
    else:
        o_ref[...] = x2
        h_ref[0][...] = _rms(x2, gn_ref[...]).astype(BF16)


MISC_WIDTH = SSM_INNER + DIL_GROUP_WIDTH + 3 * D_MODEL
MISC_GB_LANE_BLOCK = SSM_INNER // LANES
MISC_U_BLOCK = (SSM_INNER + DIL_GROUP_WIDTH) // D_MODEL


def _tail(x2d, ya, yb, yc, misc, p2d, nw, woa, wob, woc, wout, pg, wpg, wple, gn, final, tm=512):
    m = x2d.shape[0]
    out_specs = pl.BlockSpec((tm, D_MODEL), lambda i: (i, 0))
    out_shape = jax.ShapeDtypeStruct((m, D_MODEL), F32)
    if not final:
        out_specs = [out_specs, pl.BlockSpec((tm, D_MODEL), lambda i: (i, 0))]
        out_shape = [out_shape, jax.ShapeDtypeStruct((m, D_MODEL), BF16)]

    def rows(width, off=0):
        return pl.BlockSpec((tm, width), lambda i: (i, off))

    def whole(arr):
        return pl.BlockSpec(arr.shape, lambda i: (0, 0), pipeline_mode=pl.Buffered(1))

    ub = MISC_U_BLOCK
    return pl.pallas_call(
        functools.partial(_tail_kernel, final=final),
        grid=(m // tm,),
        in_specs=[rows(D_MODEL), rows(NA_WIDTH), rows(DIL_GROUP_WIDTH), rows(SSM_INNER),
                  rows(SSM_INNER, 0), rows(D_MODEL, ub), rows(D_MODEL, ub + 1), rows(D_MODEL, ub + 2),
                  rows(PLE_DIM),
                  whole(nw), whole(woa), whole(wob), whole(woc), whole(wout), whole(pg), whole(wpg),
                  whole(wple), whole(gn)],
        out_specs=out_specs,
        out_shape=out_shape,
        compiler_params=_params("parallel"),
        name="tail",
    )(x2d, ya, yb, yc, misc, misc, misc, misc, p2d, nw, woa, wob, woc, wout, pg, wpg, wple, gn)


def _prep_weights(w_in, conv_w, conv_b, a_log, dt_bias, d_skip):
    depth = w_in.shape[0]
    offs = np.concatenate([[0], np.cumsum(IN_SPLITS)])
    w16 = w_in.astype(BF16)
    (qa, ka, va, ga, qb, kb, vb, gb, xbc, z, dtr, ua, ub, uc) = [
        w16[:, :, int(offs[i]):int(offs[i + 1])] for i in range(len(IN_SPLITS))]
    w_a = w16[:, :, :int(offs[4])]
    gw = DIL_GROUP_WIDTH
    w_b = [jnp.concatenate([_rotary_layout(qb[:, :, g * gw:(g + 1) * gw]),
                            _rotary_layout(kb[:, :, g * gw:(g + 1) * gw]),
                            vb[:, :, g * gw:(g + 1) * gw]], axis=2)
           for g in range(len(DIL_PAIRS))]
    w_misc = jnp.concatenate([z, gb, ua, ub, uc], axis=2)

    hpg = SSM_HEADS_PER_GROUP

    def per_group(t):
        lead = t.shape[:-2]
        t = t.reshape(lead + (2, SSM_GROUPS, hpg))
        t = jnp.moveaxis(t, -2, -3).reshape(lead + (SSM_GROUPS, 2 * hpg))
        t = jnp.tile(t, (1,) * (t.ndim - 1) + (SSM_DT_COPIES,))
        pad = [(0, 0)] * (t.ndim - 1) + [(0, LANES - 2 * hpg * SSM_DT_COPIES)]
        return jnp.pad(t, pad).reshape(lead + (SSM_GROUPS * LANES,))

    w_dt = per_group(dtr.reshape(depth, D_MODEL, 2, SSM_HEADS))
    w_xbc = xbc
    dtb = per_group(dt_bias.astype(F32)).reshape(depth, SSM_GROUPS, 1, LANES)
    a_neg = per_group(-jnp.exp(a_log.astype(F32))).reshape(depth, SSM_GROUPS, 1, LANES)
    dsk = jnp.repeat(d_skip.astype(F32), HEAD_DIM, axis=1).reshape(depth, SSM_GROUPS, 1, SSM_GROUP_WIDTH)
    cw = conv_w.astype(F32).reshape(depth, 1, SSM_CONV, SSM_CONV_DIM)
    cbias = conv_b.astype(F32).reshape(depth, 1, 1, SSM_CONV_DIM)
    return w_a, w_b, w_misc, w_xbc, w_dt, dtb, a_neg, dsk, cw, cbias


def kernel(x, p, norm_w, w_in, na_rpb, conv_w, conv_b, a_log, dt_bias, d_skip, ssm_norm_w,
           w_oa, w_ob, w_oc, w_out, ple_norm_w, w_ple, w_ple_gate, final_norm_w):
    b, s, dm = x.shape
    depth = w_in.shape[0]
    m = b * s
    w_a, w_b, w_misc, w_xbc, w_dt, dtb, a_neg, dsk, cw, cbias = _prep_weights(
        w_in, conv_w, conv_b, a_log, dt_bias, d_skip)
    tabs = [_rotary_tables(s, d) for _, d in DIL_PAIRS]
    na_col_scale = jnp.concatenate([jnp.full((1, NA_WIDTH), SCORE_SCALE, F32),
                                    jnp.ones((1, 3 * NA_WIDTH), F32)], axis=1)
    bias_tables = _na_bias_table(na_rpb.reshape((-1,) + na_rpb.shape[2:]))
    bias_tables = bias_tables.reshape((depth, NA_HEADS // 2) + bias_tables.shape[1:])
    row = lambda v: v.astype(F32).reshape(1, -1)

    x2d = x.reshape(m, dm)
    h = _norm(x2d, row(norm_w[0]))
    for i in range(depth):
        final = i == depth - 1
        qkvg = _proj(h, w_a[i], 2 * NA_WIDTH, BF16, col_scale=na_col_scale).reshape(b, s, 4 * NA_WIDTH)
        misc = _proj(h, w_misc[i], MISC_WIDTH // 2, BF16)
        xbc = _proj(h, w_xbc[i], SSM_CONV_DIM, BF16).reshape(b, s, SSM_CONV_DIM)
        dtp = _proj(h, w_dt[i], SSM_GROUPS * LANES, F32).reshape(b, s, SSM_GROUPS * LANES)
        h3 = h.reshape(b, s, dm)
        qkvs = [_proj_dil(h3, w_b[gi][i], tabs[gi], d) for gi, (_, d) in enumerate(DIL_PAIRS)]

        ya = _na_attention(qkvg, bias_tables[i]).reshape(m, NA_WIDTH)
        yb = _dil_attention(qkvs, misc.reshape(b, s, MISC_WIDTH), MISC_GB_LANE_BLOCK)
        yb = yb.reshape(m, DIL_GROUP_WIDTH)
        yc = _ssd(xbc, dtp, cw[i], cbias[i], dtb[i], a_neg[i], dsk[i]).reshape(m, SSM_INNER)

        gn = row(final_norm_w) if final else row(norm_w[i + 1])
        out = _tail(x2d, ya, yb, yc, misc, p[i].reshape(m, PLE_DIM), row(ssm_norm_w[i]),
                    w_oa[i].astype(BF16), w_ob[i].astype(BF16), w_oc[i].astype(BF16),
                    w_out[i].astype(BF16), row(ple_norm_w[i]), w_ple_gate[i].astype(BF16),
                    w_ple[i].astype(BF16), gn, final=final)
        if final:
            x2d = out
        else:
            x2d, h = out
    return x2d.reshape(b, s, dm)
```

```python
import functools
import math

import numpy as np
import jax
import jax.numpy as jnp
from jax import lax
from jax.experimental import pallas as pl
from jax.experimental.pallas import tpu as pltpu

F32 = jnp.float32
BF16 = jnp.bfloat16

LANES = 128
VMEM_LIMIT_BYTES = 56 * 1024 * 1024

D_MODEL = 1024
GRID_W = 64
HEAD_DIM = 64
EPS = 1e-6
PLE_DIM = 256

NA_HEADS = 16
NA_WIDTH = NA_HEADS * HEAD_DIM
NA_WIN_ROWS = 8
NA_WIN_COLS = 16

DIL_PAIRS = ((128, 1), (512, 4), (2048, 16))
DIL_HEADS_PER_GROUP = 8
DIL_GROUP_WIDTH = DIL_HEADS_PER_GROUP * HEAD_DIM
DIL_WIDTH = DIL_GROUP_WIDTH * len(DIL_PAIRS)
DIL_BLK = 64
ROPE_THETA = 500000.0
ROPE_DIM = HEAD_DIM // 4
ROPE_HALF = ROPE_DIM // 2

SSM_INNER = 1536
SSM_HEADS = 24
SSM_GROUPS = 4
SSM_HEADS_PER_GROUP = SSM_HEADS // SSM_GROUPS
SSM_GROUP_WIDTH = SSM_HEADS_PER_GROUP * HEAD_DIM
SSM_STATE = 128
SSM_CONV = 5
SSM_CHUNK = 128
SSM_CONV_DIM = SSM_INNER + 2 * SSM_GROUPS * SSM_STATE

IN_SPLITS = (NA_WIDTH, NA_WIDTH, NA_WIDTH, NA_WIDTH,
             DIL_WIDTH, DIL_WIDTH, DIL_WIDTH, DIL_GROUP_WIDTH,
             SSM_CONV_DIM, SSM_INNER, 2 * SSM_HEADS,
             D_MODEL, D_MODEL, D_MODEL)

LOG2E = math.log2(math.e)
SCORE_SCALE = HEAD_DIM ** -0.5 * LOG2E


def _params(*semantics):
    return pltpu.CompilerParams(dimension_semantics=semantics,
                                vmem_limit_bytes=VMEM_LIMIT_BYTES)


def _rms(x, g):
    return x * lax.rsqrt(jnp.mean(x * x, axis=-1, keepdims=True) + EPS) * g


def _silu(x):
    x = x.astype(F32)
    return x * jax.nn.sigmoid(x)


def _sigmoid(x):
    return jax.nn.sigmoid(x.astype(F32))


def _norm_kernel(x_ref, g_ref, h_ref):
    h_ref[...] = _rms(x_ref[...], g_ref[...]).astype(h_ref.dtype)


def _norm(x2d, g, tm=1024):
    m, k = x2d.shape
    return pl.pallas_call(
        _norm_kernel,
        grid=(m // tm,),
        in_specs=[pl.BlockSpec((tm, k), lambda i: (i, 0)), pl.BlockSpec((1, k), lambda i: (0, 0))],
        out_specs=pl.BlockSpec((tm, k), lambda i: (i, 0)),
        out_shape=jax.ShapeDtypeStruct((m, k), BF16),
        compiler_params=_params("parallel"),
        name="norm",
    )(x2d, g)


def _proj_kernel(h_ref, w_ref, *rest):
    o_ref = rest[-1]
    acc = jnp.dot(h_ref[...], w_ref[...], preferred_element_type=F32)
    if len(rest) == 2:
        acc = acc * rest[0][...]
    o_ref[...] = acc.astype(o_ref.dtype)


def _proj(h2d, w, tn, out_dtype, tm=1024, col_scale=None):
    m, k = h2d.shape
    n = w.shape[1]
    in_specs = [pl.BlockSpec((tm, k), lambda i, j: (i, 0)),
                pl.BlockSpec((k, tn), lambda i, j: (0, j))]
    args = [h2d, w]
    if col_scale is not None:
        in_specs.append(pl.BlockSpec((1, tn), lambda i, j: (0, j)))
        args.append(col_scale)
    return pl.pallas_call(
        _proj_kernel,
        grid=(m // tm, n // tn),
        in_specs=in_specs,
        out_specs=pl.BlockSpec((tm, tn), lambda i, j: (i, j)),
        out_shape=jax.ShapeDtypeStruct((m, n), out_dtype),
        compiler_params=_params("parallel", "arbitrary"),
        name="proj",
    )(*args)


MXU_N = 256


REGROUP_ROWS = 256


def _regroup_rows(h, d):
    tm = h.shape[0]
    if d == 1:
        return h
    per = REGROUP_ROWS // d
    r = lax.broadcasted_iota(jnp.int32, (REGROUP_ROWS, REGROUP_ROWS), 0)
    c = lax.broadcasted_iota(jnp.int32, (REGROUP_ROWS, REGROUP_ROWS), 1)
    perm = jnp.where(c == (r % per) * d + r // per, 1.0, 0.0).astype(BF16)
    nblk = tm // REGROUP_ROWS
    blocks = [jnp.dot(perm, h[i * REGROUP_ROWS:(i + 1) * REGROUP_ROWS],
                      preferred_element_type=F32).astype(BF16) for i in range(nblk)]
    return jnp.concatenate([blocks[i][rho * per:(rho + 1) * per]
                            for rho in range(d) for i in range(nblk)], axis=0)


def _proj_dil_kernel(h_ref, w_ref, c_ref, s_ref, o_ref):
    nres, rows = o_ref.shape[1:3]
    k, n = w_ref.shape
    tm = nres * rows
    gw = DIL_GROUP_WIDTH
    hp = _regroup_rows(h_ref[0], nres)
    for nb in range(n // MXU_N):
        lo = nb * MXU_N
        acc = jnp.dot(hp, w_ref[:, lo:lo + MXU_N], preferred_element_type=F32)
        part = lo // gw
        if part == 2:
            o_ref[0, :, :, lo:lo + MXU_N] = acc.astype(o_ref.dtype).reshape(nres, rows, MXU_N)
            continue
        c = c_ref[part].reshape(tm, LANES)
        sn = s_ref[part].reshape(tm, LANES)
        for cb in range(MXU_N // LANES):
            blk = acc[:, cb * LANES:(cb + 1) * LANES]
            rot = blk * c + pltpu.roll(blk, LANES // 2, 1) * sn
            o_ref[0, :, :, lo + cb * LANES:lo + (cb + 1) * LANES] = (
                rot.astype(o_ref.dtype).reshape(nres, rows, LANES))


def _proj_dil(h3, w, tabs, d, tm=1024):
    b, s, k = h3.shape
    n = w.shape[1]
    rows = tm // d
    c, sn = tabs
    tab_spec = pl.BlockSpec((2, d, rows, LANES), lambda bi, i: (0, 0, i, 0))
    return pl.pallas_call(
        _proj_dil_kernel,
        grid=(b, s // tm),
        in_specs=[pl.BlockSpec((1, tm, k), lambda bi, i: (bi, i, 0)),
                  pl.BlockSpec((k, n), lambda bi, i: (0, 0)),
                  tab_spec, tab_spec],
        out_specs=pl.BlockSpec((1, d, rows, n), lambda bi, i: (bi, 0, i, 0)),
        out_shape=jax.ShapeDtypeStruct((b, d, s // d, n), BF16),
        compiler_params=_params("parallel", "parallel"),
        name=f"proj_dil{d}",
    )(h3, w, c, sn)


def _rotary_tables(s, d):
    inv = ROPE_THETA ** (-jnp.arange(0, ROPE_DIM, 2, dtype=F32) / ROPE_DIM)
    ang = jnp.arange(s).astype(F32)[:, None] * inv[None, :]
    cos, sin = jnp.cos(ang), jnp.sin(ang)
    rest = LANES // 2 - 2 * ROPE_HALF
    ones = jnp.ones((s, rest), F32)
    zeros = jnp.zeros((s, rest), F32)
    c = jnp.concatenate([cos, cos, ones, cos, cos, ones], axis=1)
    sn = jnp.concatenate([-sin, -sin, zeros, sin, sin, zeros], axis=1)

    def arrange(t):
        t = t.reshape(s // d, d, LANES).transpose(1, 0, 2)
        return jnp.stack([t * SCORE_SCALE, t])

    return arrange(c), arrange(sn)


def _rotary_layout(t):
    lead = t.shape[:-1]
    t = t.reshape(lead + (t.shape[-1] // LANES, 2, HEAD_DIM))
    npair = t.shape[-3]
    a = t[..., :, :, 0:ROPE_HALF].reshape(lead + (npair, 2 * ROPE_HALF))
    b = t[..., :, :, ROPE_HALF:ROPE_DIM].reshape(lead + (npair, 2 * ROPE_HALF))
    rest0 = t[..., :, 0, ROPE_DIM:]
    rest1 = t[..., :, 1, ROPE_DIM:]
    return jnp.concatenate([a, rest0, b, rest1], axis=-1).reshape(lead + (npair * LANES,))


NA_UNROLL = 32


def _na_kernel(q_ref, k_ref, v_ref, g_ref, tb_ref, o_ref):
    rows = q_ref.shape[1] // GRID_W
    nkeys = NA_WIN_ROWS * GRID_W
    lane = lax.broadcasted_iota(jnp.int32, (GRID_W, LANES), 1)
    first = lane < HEAD_DIM

    def body(step, carry):
        units = []
        for i in range(NA_UNROLL):
            r = step * NA_UNROLL + i
            r0 = jnp.clip(r - NA_WIN_ROWS // 2, 0, rows - NA_WIN_ROWS)
            st = r0 - r + NA_WIN_ROWS - 1
            qs = pl.ds(pl.multiple_of(r * GRID_W, GRID_W), GRID_W)
            ks = pl.ds(pl.multiple_of(r0 * GRID_W, GRID_W), nkeys)
            q = q_ref[0, qs, :]
            kw = k_ref[0, ks, :]
            zero = jnp.zeros_like(q)
            q2 = jnp.concatenate([jnp.where(first, q, zero), jnp.where(first, zero, q)], axis=0)
            sc = lax.dot_general(q2, kw, (((1,), (1,)), ((), ())),
                                 preferred_element_type=F32)
            units.append((qs, ks, sc + tb_ref[0, st]))
        probs = []
        for qs, ks, sc in units:
            mx = jnp.max(sc, axis=-1, keepdims=True)
            e = jnp.exp2(sc - mx)
            probs.append((e.astype(BF16), jnp.sum(e, axis=-1, keepdims=True)))
        for (qs, ks, _), (p, den) in zip(units, probs):
            o2 = jnp.dot(p, v_ref[0, ks, :], preferred_element_type=F32) / den
            o = jnp.where(first, o2[0:GRID_W], o2[GRID_W:])
            o_ref[0, qs, :] = (o * _silu(g_ref[0, qs, :])).astype(o_ref.dtype)
        return carry

    lax.fori_loop(0, rows // NA_UNROLL, body, 0)


def _na_bias_table(rpb):
    h = rpb.shape[0]
    cq = np.arange(GRID_W)[:, None]
    ck = np.arange(GRID_W)[None, :]
    ws = np.clip(cq - NA_WIN_COLS // 2, 0, GRID_W - NA_WIN_COLS)
    in_win = (ck >= ws) & (ck < ws + NA_WIN_COLS)
    line = 2 * GRID_W
    lo = GRID_W - NA_WIN_COLS
    ext = jnp.pad(rpb.astype(F32) * LOG2E, ((0, 0), (0, 0), (lo, line - lo - rpb.shape[-1])))
    skew = jnp.broadcast_to(ext[:, :, None, :], ext.shape[:2] + (GRID_W, line))
    skew = skew.reshape(ext.shape[:2] + (GRID_W * line,))[:, :, :GRID_W * (line - 1)]
    toep = skew.reshape(ext.shape[:2] + (GRID_W, line - 1))[:, :, :, GRID_W - 1:]
    colb = jnp.where(in_win, toep, -jnp.inf)
    tb = jnp.stack([colb[:, st:st + NA_WIN_ROWS] for st in range(NA_WIN_ROWS)], axis=1)
    tb = tb.reshape(h // 2, 2, NA_WIN_ROWS, NA_WIN_ROWS, GRID_W, GRID_W)
    tb = tb.transpose(0, 2, 1, 4, 3, 5).reshape(h // 2, NA_WIN_ROWS, 2 * GRID_W, NA_WIN_ROWS * GRID_W)
    return tb


def _na_attention(qkvg, tb):
    b, s, _ = qkvg.shape
    npair = NA_WIDTH // LANES

    def col(off):
        return pl.BlockSpec((1, s, LANES), lambda hp, bi: (bi, 0, off + hp))

    return pl.pallas_call(
        _na_kernel,
        grid=(npair, b),
        in_specs=[col(0), col(npair), col(2 * npair), col(3 * npair),
                  pl.BlockSpec((1,) + tb.shape[1:], lambda hp, bi: (hp, 0, 0, 0))],
        out_specs=pl.BlockSpec((1, s, LANES), lambda hp, bi: (bi, 0, hp)),
        out_shape=jax.ShapeDtypeStruct((b, s, NA_WIDTH), BF16),
        compiler_params=_params("parallel", "parallel"),
        name="na_attention",
    )(qkvg, qkvg, qkvg, qkvg, tb)


DIL_QB = 2 * DIL_BLK
DIL_KB = 4 * DIL_BLK
DIL_UNROLL = 32


def _dil_kernel(q0, k0, v0, q1, k1, v1, q2, k2, v2, g_ref, o_ref, acc_ref, m_ref, w_ref):
    s = o_ref.shape[1]
    lane = lax.broadcasted_iota(jnp.int32, (DIL_QB, LANES), 1)
    first = lane < HEAD_DIM
    half_off = lane % (LANES // 2)
    q_first = (half_off < ROPE_HALF) | ((half_off >= 2 * ROPE_HALF) & (lane < LANES // 2))
    v_first = lax.broadcasted_iota(jnp.int32, (DIL_KB, LANES), 1) < HEAD_DIM
    qi = lax.broadcasted_iota(jnp.int32, (DIL_QB, DIL_KB), 0)
    ki = lax.broadcasted_iota(jnp.int32, (DIL_QB, DIL_KB), 1)

    groups = ((q0, k0, v0), (q1, k1, v1), (q2, k2, v2))
    first_group = len(groups) - 1
    for g in range(first_group, -1, -1):
        q_ref, k_ref, v_ref = groups[g]
        d = DIL_PAIRS[g][1]
        l = s // d
        nu = l // DIL_QB

        def trip(step, carry, g=g, d=d, l=l, nu=nu, q_ref=q_ref, k_ref=k_ref, v_ref=v_ref):
            units = []
            for i in range(DIL_UNROLL):
                idx = step * DIL_UNROLL + i
                rho = idx // nu
                u = idx % nu
                q_start = pl.multiple_of(u * DIL_QB, DIL_QB)
                k_start = pl.multiple_of(jnp.clip(u * DIL_QB - DIL_BLK, 0, l - DIL_KB), DIL_BLK)
                q = q_ref[0, rho, pl.ds(q_start, DIL_QB), :]
                kw = k_ref[0, rho, pl.ds(k_start, DIL_KB), :]
                valid = jnp.abs((ki + k_start) - (qi + q_start)) <= DIL_BLK
                zero = jnp.zeros_like(q)
                scs = []
                for h in range(2):
                    qh = jnp.where(q_first, q, zero) if h == 0 else jnp.where(q_first, zero, q)
                    sc = lax.dot_general(qh, kw, (((1,), (1,)), ((), ())),
                                         preferred_element_type=F32)
                    scs.append(jnp.where(valid, sc, -jnp.inf))
                units.append((rho, q_start, k_start, scs))
            soft = []
            for rho, q_start, k_start, scs in units:
                es, mxs = [], []
                for sc in scs:
                    mx = jnp.max(sc, axis=-1, keepdims=True)
                    es.append(jnp.exp2(sc - mx).astype(BF16))
                    mxs.append(mx)
                soft.append((es, jnp.where(first, mxs[0], mxs[1])))
            for (rho, q_start, k_start, _), (es, mx) in zip(units, soft):
                vw = v_ref[0, rho, pl.ds(k_start, DIL_KB), :]
                one = jnp.ones_like(vw)
                o0 = jnp.dot(es[0], jnp.where(v_first, vw, one), preferred_element_type=F32)
                o1 = jnp.dot(es[1], jnp.where(v_first, one, vw), preferred_element_type=F32)
                num = jnp.where(first, o0, o1)
                den = pltpu.roll(jnp.where(first, o1, o0), HEAD_DIM, 1)
                if d == 1:
                    rows = pl.ds(q_start, DIL_QB)
                else:
                    rows = pl.ds(rho + d * q_start, DIL_QB, stride=d)
                if g == first_group:
                    acc_ref[rows, :] = num
                    m_ref[rows, :] = mx
                    w_ref[rows, :] = den
                else:
                    m_old = m_ref[rows, :]
                    m_new = jnp.maximum(m_old, mx)
                    a = jnp.exp2(m_old - m_new)
                    bw = jnp.exp2(mx - m_new)
                    acc_ref[rows, :] = acc_ref[rows, :] * a + num * bw
                    w_ref[rows, :] = w_ref[rows, :] * a + den * bw
                    m_ref[rows, :] = m_new
            return carry

        lax.fori_loop(0, d * nu // DIL_UNROLL, trip, 0)

    def finish(i, carry):
        rows = pl.ds(pl.multiple_of(i * 256, 256), 256)
        y = acc_ref[rows, :] / w_ref[rows, :]
        o_ref[0, rows, :] = (y * _silu(g_ref[0, rows, :])).astype(o_ref.dtype)
        return carry

    lax.fori_loop(0, s // 256, finish, 0)


def _dil_attention(qkvs, gate, gate_off):
    b = gate.shape[0]
    s = gate.shape[1]
    npair = DIL_GROUP_WIDTH // LANES
    in_specs = []
    args = []
    for arr in qkvs:
        d, l = arr.shape[1], arr.shape[2]
        for part in range(3):
            in_specs.append(pl.BlockSpec((1, d, l, LANES),
                                         lambda bi, sp, part=part: (bi, 0, 0, part * npair + sp)))
            args.append(arr)
    in_specs.append(pl.BlockSpec((1, s, LANES), lambda bi, sp: (bi, 0, gate_off + sp)))
    args.append(gate)
    return pl.pallas_call(
        _dil_kernel,
        grid=(b, npair),
        in_specs=in_specs,
        out_specs=pl.BlockSpec((1, s, LANES), lambda bi, sp: (bi, 0, sp)),
        out_shape=jax.ShapeDtypeStruct((b, s, DIL_GROUP_WIDTH), BF16),
        scratch_shapes=[pltpu.VMEM((s, LANES), F32)] * 3,
        compiler_params=_params("parallel", "parallel"),
        name="dil_attention",
    )(*args)


def _split3(x):
    hi = x.astype(BF16)
    r1 = x - hi.astype(F32)
    mid = r1.astype(BF16)
    lo = (r1 - mid.astype(F32)).astype(BF16)
    return hi, mid, lo


SSM_HALO = 16
SSM_DT_COPIES = 2
SSM_UNROLL = 4


def _ssd_kernel(xs_ref, bm_ref, cm_ref, dt_ref, wx_ref, wb_ref, wc_ref, bx_ref, bb_ref, bc_ref,
                dtb_ref, a_ref, dsk_ref, y_ref, xs_s, bt_s, c_s, cb_s, acum_s, pt_s, st_s):
    s = xs_ref.shape[1]
    cl = SSM_CHUNK
    nchunk = s // cl
    hpg = SSM_HEADS_PER_GROUP
    npair = SSM_GROUP_WIDTH // LANES
    ncol = 2 * hpg

    def conv_chunk(c, carry):
        base = pl.multiple_of(c * cl, cl)
        prev_start = pl.multiple_of(jnp.maximum(base - SSM_HALO, 0), SSM_HALO)
        next_start = pl.multiple_of(jnp.minimum(base + cl, s - SSM_HALO), SSM_HALO)
        has_prev = jnp.where(c > 0, 1.0, 0.0)
        has_next = jnp.where(c < nchunk - 1, 1.0, 0.0)

        def conv(src, w_ref, bias_ref):
            ext = jnp.concatenate([src[0, pl.ds(prev_start, SSM_HALO), :].astype(F32) * has_prev,
                                   src[0, pl.ds(base, cl), :].astype(F32),
                                   src[0, pl.ds(next_start, SSM_HALO), :].astype(F32) * has_next], axis=0)
            acc = bias_ref[0]
            off = SSM_HALO - SSM_CONV // 2
            for tap in range(SSM_CONV):
                acc = acc + ext[off + tap:off + tap + cl, :] * w_ref[0, tap:tap + 1, :]
            return _silu(acc)

        xs_s[pl.ds(base, cl), :] = conv(xs_ref, wx_ref, bx_ref)
        bmat = conv(bm_ref, wb_ref, bb_ref).astype(BF16)
        cmat = conv(cm_ref, wc_ref, bc_ref).astype(BF16)
        bt_s[c] = bmat.astype(F32).T.astype(BF16)
        c_s[pl.ds(base, cl), :] = cmat
        cb_s[c] = lax.dot_general(cmat, bmat, (((1,), (1,)), ((), ())),
                                  preferred_element_type=F32)

        dt = jax.nn.softplus(dt_ref[0, pl.ds(base, cl), :] + dt_bias)
        pieces = _split3(dt * a_neg)
        acum_f = sum(jnp.dot(tril, p, preferred_element_type=F32) for p in pieces)
        acum_b = sum(jnp.dot(triu, p, preferred_element_type=F32) for p in pieces)
        acum = jnp.where(fwd_lane, acum_f, acum_b)
        a_end = jnp.where(fwd_lane[0:1], acum[cl - 1:cl, :], acum[0:1, :])
        w_state = dt * jnp.exp(a_end - acum)
        packed_t = jnp.where(lane < ncol, acum * LOG2E - jnp.log2(dt), w_state).T
        acum_s[c] = acum * LOG2E
        pt_s[c] = packed_t[0:SSM_DT_COPIES * ncol, :]
        return carry

    ri = lax.broadcasted_iota(jnp.int32, (cl, cl), 0)
    ci = lax.broadcasted_iota(jnp.int32, (cl, cl), 1)
    lane = lax.broadcasted_iota(jnp.int32, (cl, LANES), 1)
    first = lane < HEAD_DIM
    fwd_lane = (lane % ncol) < hpg
    tril = jnp.where(ci <= ri, 1.0, 0.0).astype(BF16)
    triu = jnp.where(ci >= ri, 1.0, 0.0).astype(BF16)
    dt_bias = dtb_ref[0]
    a_neg = a_ref[0]

    lax.fori_loop(0, nchunk, conv_chunk, 0, unroll=4)

    for direction in range(2):
        if direction == 0:
            keep = ci <= ri
            end_row = cl - 1
        else:
            keep = ci >= ri
            end_row = 0

        st_s[...] = jnp.zeros_like(st_s)

        def local_part(c, direction=direction, keep=keep, end_row=end_row):
            rows = pl.ds(pl.multiple_of(c * cl, cl), cl)
            acum = acum_s[c]
            packed_t = pt_s[c]

            xs = xs_s[rows, :]
            bt = bt_s[c].astype(F32)
            cb = cb_s[c]
            yd_blocks, new_blocks, scale_blocks = [], [], []
            for pr in range(npair):
                xblk = xs[:, pr * LANES:(pr + 1) * LANES].astype(BF16)
                lhs, cols = [], []
                for j in (direction * hpg + 2 * pr, direction * hpg + 2 * pr + 1):
                    col = jnp.broadcast_to(acum[:, j:j + 1], (cl, cl))
                    row = packed_t[j:j + 1, :]
                    ws_row = packed_t[ncol + j:ncol + j + 1, :]
                    seg_dt = jnp.exp2(jnp.where(keep, col - row, -jnp.inf))
                    lhs.append((cb * seg_dt).astype(BF16))
                    lhs.append((bt * ws_row).astype(BF16))
                    cols.append(col)
                prod = jnp.dot(jnp.concatenate(lhs, axis=0), xblk, preferred_element_type=F32)
                yd_blocks.append(jnp.where(first, prod[0:cl], prod[2 * cl:3 * cl]))
                new_blocks.append(jnp.where(first, prod[cl:2 * cl], prod[3 * cl:4 * cl]))
                scale_blocks.append(jnp.exp2(jnp.where(first, cols[0], cols[1])))
            return rows, xs, yd_blocks, new_blocks, scale_blocks

        def trip(step, carry, direction=direction, end_row=end_row):
            chunks = [step * SSM_UNROLL + i for i in range(SSM_UNROLL)]
            if direction == 1:
                chunks = [nchunk - 1 - c for c in chunks]
            parts = [local_part(c) for c in chunks]
            state = [st_s[:, pr * LANES:(pr + 1) * LANES] for pr in range(npair)]
            for rows, xs, yd_blocks, new_blocks, scale_blocks in parts:
                cmat = c_s[rows, :]
                y_blocks = []
                for pr in range(npair):
                    y_off = jnp.dot(cmat, state[pr].astype(BF16), preferred_element_type=F32)
                    y_blocks.append(yd_blocks[pr] + y_off * scale_blocks[pr])
                    state[pr] = (state[pr] * scale_blocks[pr][end_row:end_row + 1, :]
                                 + new_blocks[pr])
                y = jnp.concatenate(y_blocks, axis=1)
                if direction == 0:
                    y_ref[0, rows, :] = y + dsk_ref[0] * xs
                else:
                    y_ref[0, rows, :] = y_ref[0, rows, :] + y
            for pr in range(npair):
                st_s[:, pr * LANES:(pr + 1) * LANES] = state[pr]
            return carry

        lax.fori_loop(0, nchunk // SSM_UNROLL, trip, 0)


def _ssd(xbc, dtp, conv_w, conv_b, dt_bias, a_neg, d_skip):
    b, s, _ = xbc.shape
    gw = SSM_GROUP_WIDTH
    b_off = SSM_INNER // LANES
    c_off = b_off + SSM_GROUPS
    nchunk = s // SSM_CHUNK

    def seq(width, off):
        return pl.BlockSpec((1, s, width), lambda bi, g: (bi, 0, off + g))

    def par(rows, width, off):
        return pl.BlockSpec((1, rows, width), lambda bi, g: (0, 0, off + g))

    return pl.pallas_call(
        _ssd_kernel,
        grid=(b, SSM_GROUPS),
        in_specs=[seq(gw, 0), seq(LANES, b_off), seq(LANES, c_off), seq(LANES, 0),
                  par(SSM_CONV, gw, 0), par(SSM_CONV, LANES, b_off), par(SSM_CONV, LANES, c_off),
                  par(1, gw, 0), par(1, LANES, b_off), par(1, LANES, c_off),
                  pl.BlockSpec((1, 1, LANES), lambda bi, g: (g, 0, 0)),
                  pl.BlockSpec((1, 1, LANES), lambda bi, g: (g, 0, 0)),
                  pl.BlockSpec((1, 1, gw), lambda bi, g: (g, 0, 0))],
        out_specs=pl.BlockSpec((1, s, gw), lambda bi, g: (bi, 0, g)),
        out_shape=jax.ShapeDtypeStruct((b, s, SSM_INNER), F32),
        scratch_shapes=[pltpu.VMEM((s, gw), F32),
                        pltpu.VMEM((nchunk, SSM_STATE, SSM_CHUNK), BF16),
                        pltpu.VMEM((s, LANES), BF16),
                        pltpu.VMEM((nchunk, SSM_CHUNK, SSM_CHUNK), F32),
                        pltpu.VMEM((nchunk, SSM_CHUNK, LANES), F32),
                        pltpu.VMEM((nchunk, SSM_DT_COPIES * 2 * SSM_HEADS_PER_GROUP, SSM_CHUNK), F32),
                        pltpu.VMEM((SSM_STATE, gw), F32)],
        compiler_params=_params("parallel", "parallel"),
        name="ssd",
    )(xbc, xbc, xbc, dtp, conv_w, conv_w, conv_w, conv_b, conv_b, conv_b, dt_bias, a_neg, d_skip)


def _tail_kernel(x_ref, ya_ref, yb_ref, yc_ref, z_ref, ua_ref, ub_ref, uc_ref, p_ref,
                 nw_ref, woa_ref, wob_ref, woc_ref, wout_ref, pg_ref, wpg_ref, wple_ref, gn_ref,
                 o_ref, *h_ref, final):
    def mm(a, w_ref):
        return jnp.dot(a, w_ref[...], preferred_element_type=F32)

    ya = mm(ya_ref[...], woa_ref)
    yb = mm(yb_ref[...], wob_ref)
    yc_in = _rms(yc_ref[...] * _silu(z_ref[...]), nw_ref[...]).astype(BF16)
    yc = mm(yc_in, woc_ref)
    merged = (_sigmoid(ua_ref[...]) * ya + _sigmoid(ub_ref[...]) * yb
              + _sigmoid(uc_ref[...]) * yc)
    x1 = x_ref[...] + mm(merged.astype(BF16), wout_ref)
    gate = jax.nn.sigmoid(mm(_rms(x1, pg_ref[...]).astype(BF16), wpg_ref))
    x2 = x1 + mm(p_ref[...].astype(BF16), wple_ref) * gate
    if final:
        o_ref[...] = _rms(x2, gn_ref[...])
    else:
        o_ref[...] = x2
        h_ref[0][...] = _rms(x2, gn_ref[...]).astype(BF16)


MISC_WIDTH = SSM_INNER + DIL_GROUP_WIDTH + 3 * D_MODEL
MISC_GB_LANE_BLOCK = SSM_INNER // LANES
MISC_U_BLOCK = (SSM_INNER + DIL_GROUP_WIDTH) // D_MODEL


def _tail(x2d, ya, yb, yc, misc, p2d, nw, woa, wob, woc, wout, pg, wpg, wple, gn, final, tm=512):
    m = x2d.shape[0]
    out_specs = pl.BlockSpec((tm, D_MODEL), lambda i: (i, 0))
    out_shape = jax.ShapeDtypeStruct((m, D_MODEL), F32)
    if not final:
        out_specs = [out_specs, pl.BlockSpec((tm, D_MODEL), lambda i: (i, 0))]
        out_shape = [out_shape, jax.ShapeDtypeStruct((m, D_MODEL), BF16)]

    def rows(width, off=0):
        return pl.BlockSpec((tm, width), lambda i: (i, off))

    def whole(arr):
        return pl.BlockSpec(arr.shape, lambda i: (0, 0), pipeline_mode=pl.Buffered(1))

    ub = MISC_U_BLOCK
    return pl.pallas_call(
        functools.partial(_tail_kernel, final=final),
        grid=(m // tm,),
        in_specs=[rows(D_MODEL), rows(NA_WIDTH), rows(DIL_GROUP_WIDTH), rows(SSM_INNER),
                  rows(SSM_INNER, 0), rows(D_MODEL, ub), rows(D_MODEL, ub + 1), rows(D_MODEL, ub + 2),
                  rows(PLE_DIM),
                  whole(nw), whole(woa), whole(wob), whole(woc), whole(wout), whole(pg), whole(wpg),
                  whole(wple), whole(gn)],
        out_specs=out_specs,
        out_shape=out_shape,
        compiler_params=_params("parallel"),
        name="tail",
    )(x2d, ya, yb, yc, misc, misc, misc, misc, p2d, nw, woa, wob, woc, wout, pg, wpg, wple, gn)


def _prep_weights(w_in, conv_w, conv_b, a_log, dt_bias, d_skip):
    depth = w_in.shape[0]
    offs = np.concatenate([[0], np.cumsum(IN_SPLITS)])
    w16 = w_in.astype(BF16)
    (qa, ka, va, ga, qb, kb, vb, gb, xbc, z, dtr, ua, ub, uc) = [
        w16[:, :, int(offs[i]):int(offs[i + 1])] for i in range(len(IN_SPLITS))]
    w_a = w16[:, :, :int(offs[4])]
    gw = DIL_GROUP_WIDTH
    w_b = [jnp.concatenate([_rotary_layout(qb[:, :, g * gw:(g + 1) * gw]),
                            _rotary_layout(kb[:, :, g * gw:(g + 1) * gw]),
                            vb[:, :, g * gw:(g + 1) * gw]], axis=2)
           for g in range(len(DIL_PAIRS))]
    w_misc = jnp.concatenate([z, gb, ua, ub, uc], axis=2)

    hpg = SSM_HEADS_PER_GROUP

    def per_group(t):
        lead = t.shape[:-2]
        t = t.reshape(lead + (2, SSM_GROUPS, hpg))
        t = jnp.moveaxis(t, -2, -3).reshape(lead + (SSM_GROUPS, 2 * hpg))
        t = jnp.tile(t, (1,) * (t.ndim - 1) + (SSM_DT_COPIES,))
        pad = [(0, 0)] * (t.ndim - 1) + [(0, LANES - 2 * hpg * SSM_DT_COPIES)]
        return jnp.pad(t, pad).reshape(lead + (SSM_GROUPS * LANES,))

    w_dt = per_group(dtr.reshape(depth, D_MODEL, 2, SSM_HEADS))
    w_xbc = xbc
    dtb = per_group(dt_bias.astype(F32)).reshape(depth, SSM_GROUPS, 1, LANES)
    a_neg = per_group(-jnp.exp(a_log.astype(F32))).reshape(depth, SSM_GROUPS, 1, LANES)
    dsk = jnp.repeat(d_skip.astype(F32), HEAD_DIM, axis=1).reshape(depth, SSM_GROUPS, 1, SSM_GROUP_WIDTH)
    cw = conv_w.astype(F32).reshape(depth, 1, SSM_CONV, SSM_CONV_DIM)
    cbias = conv_b.astype(F32).reshape(depth, 1, 1, SSM_CONV_DIM)
    return w_a, w_b, w_misc, w_xbc, w_dt, dtb, a_neg, dsk, cw, cbias


def kernel(x, p, norm_w, w_in, na_rpb, conv_w, conv_b, a_log, dt_bias, d_skip, ssm_norm_w,
           w_oa, w_ob, w_oc, w_out, ple_norm_w, w_ple, w_ple_gate, final_norm_w):
    b, s, dm = x.shape
    depth = w_in.shape[0]
    m = b * s
    w_a, w_b, w_misc, w_xbc, w_dt, dtb, a_neg, dsk, cw, cbias = _prep_weights(
        w_in, conv_w, conv_b, a_log, dt_bias, d_skip)
    tabs = [_rotary_tables(s, d) for _, d in DIL_PAIRS]
    na_col_scale = jnp.concatenate([jnp.full((1, NA_WIDTH), SCORE_SCALE, F32),
                                    jnp.ones((1, 3 * NA_WIDTH), F32)], axis=1)
    bias_tables = _na_bias_table(na_rpb.reshape((-1,) + na_rpb.shape[2:]))
    bias_tables = bias_tables.reshape((depth, NA_HEADS // 2) + bias_tables.shape[1:])
    row = lambda v: v.astype(F32).reshape(1, -1)

    x2d = x.reshape(m, dm)
    h = _norm(x2d, row(norm_w[0]))
    for i in range(depth):
        final = i == depth - 1
        qkvg = _proj(h, w_a[i], 2 * NA_WIDTH, BF16, col_scale=na_col_scale).reshape(b, s, 4 * NA_WIDTH)
        misc = _proj(h, w_misc[i], MISC_WIDTH // 2, BF16)
        xbc = _proj(h, w_xbc[i], SSM_CONV_DIM, BF16).reshape(b, s, SSM_CONV_DIM)
        dtp = _proj(h, w_dt[i], SSM_GROUPS * LANES, F32).reshape(b, s, SSM_GROUPS * LANES)
        h3 = h.reshape(b, s, dm)
        qkvs = [_proj_dil(h3, w_b[gi][i], tabs[gi], d) for gi, (_, d) in enumerate(DIL_PAIRS)]

        ya = _na_attention(qkvg, bias_tables[i]).reshape(m, NA_WIDTH)
        yb = _dil_attention(qkvs, misc.reshape(b, s, MISC_WIDTH), MISC_GB_LANE_BLOCK)
        yb = yb.reshape(m, DIL_GROUP_WIDTH)
        yc = _ssd(xbc, dtp, cw[i], cbias[i], dtb[i], a_neg[i], dsk[i]).reshape(m, SSM_INNER)

        gn = row(final_norm_w) if final else row(norm_w[i + 1])
        out = _tail(x2d, ya, yb, yc, misc, p[i].reshape(m, PLE_DIM), row(ssm_norm_w[i]),
                    w_oa[i].astype(BF16), w_ob[i].astype(BF16), w_oc[i].astype(BF16),
                    w_out[i].astype(BF16), row(ple_norm_w[i]), w_ple_gate[i].astype(BF16),
                    w_ple[i].astype(BF16), gn, final=final)
        if final:
            x2d = out
        else:
            x2d, h = out
    return x2d.reshape(b, s, dm)
```

```python
import functools
import math

import numpy as np
import jax
import jax.numpy as jnp
from jax import lax
from jax.experimental import pallas as pl
from jax.experimental.pallas import tpu as pltpu

F32 = jnp.float32
BF16 = jnp.bfloat16

LANES = 128
VMEM_LIMIT_BYTES = 56 * 1024 * 1024

D_MODEL = 1024
GRID_W = 64
HEAD_DIM = 64
EPS = 1e-6
PLE_DIM = 256

NA_HEADS = 16
NA_WIDTH = NA_HEADS * HEAD_DIM
NA_WIN_ROWS = 8
NA_WIN_COLS = 16

DIL_PAIRS = ((128, 1), (512, 4), (2048, 16))
DIL_HEADS_PER_GROUP = 8
DIL_GROUP_WIDTH = DIL_HEADS_PER_GROUP * HEAD_DIM
DIL_WIDTH = DIL_GROUP_WIDTH * len(DIL_PAIRS)
DIL_BLK = 64
ROPE_THETA = 500000.0
ROPE_DIM = HEAD_DIM // 4
ROPE_HALF = ROPE_DIM // 2

SSM_INNER = 1536
SSM_HEADS = 24
SSM_GROUPS = 4
SSM_HEADS_PER_GROUP = SSM_HEADS // SSM_GROUPS
SSM_GROUP_WIDTH = SSM_HEADS_PER_GROUP * HEAD_DIM
SSM_STATE = 128
SSM_CONV = 5
SSM_CHUNK = 128
SSM_CONV_DIM = SSM_INNER + 2 * SSM_GROUPS * SSM_STATE

IN_SPLITS = (NA_WIDTH, NA_WIDTH, NA_WIDTH, NA_WIDTH,
             DIL_WIDTH, DIL_WIDTH, DIL_WIDTH, DIL_GROUP_WIDTH,
             SSM_CONV_DIM, SSM_INNER, 2 * SSM_HEADS,
             D_MODEL, D_MODEL, D_MODEL)

LOG2E = math.log2(math.e)
SCORE_SCALE = HEAD_DIM ** -0.5 * LOG2E


def _params(*semantics):
    return pltpu.CompilerParams(dimension_semantics=semantics,
                                vmem_limit_bytes=VMEM_LIMIT_BYTES)


def _rms(x, g):
    return x * lax.rsqrt(jnp.mean(x * x, axis=-1, keepdims=True) + EPS) * g


def _silu(x):
    x = x.astype(F32)
    return x * jax.nn.sigmoid(x)


def _sigmoid(x):
    return jax.nn.sigmoid(x.astype(F32))


def _norm_kernel(x_ref, g_ref, h_ref):
    h_ref[...] = _rms(x_ref[...], g_ref[...]).astype(h_ref.dtype)


def _norm(x2d, g, tm=1024):
    m, k = x2d.shape
    return pl.pallas_call(
        _norm_kernel,
        grid=(m // tm,),
        in_specs=[pl.BlockSpec((tm, k), lambda i: (i, 0)), pl.BlockSpec((1, k), lambda i: (0, 0))],
        out_specs=pl.BlockSpec((tm, k), lambda i: (i, 0)),
        out_shape=jax.ShapeDtypeStruct((m, k), BF16),
        compiler_params=_params("parallel"),
        name="norm",
    )(x2d, g)


def _proj_kernel(h_ref, w_ref, *rest):
    o_ref = rest[-1]
    acc = jnp.dot(h_ref[...], w_ref[...], preferred_element_type=F32)
    if len(rest) == 2:
        acc = acc * rest[0][...]
    o_ref[...] = acc.astype(o_ref.dtype)


def _proj(h2d, w, tn, out_dtype, tm=2048, col_scale=None):
    m, k = h2d.shape
    n = w.shape[1]
    in_specs = [pl.BlockSpec((tm, k), lambda i, j: (i, 0)),
                pl.BlockSpec((k, tn), lambda i, j: (0, j))]
    args = [h2d, w]
    if col_scale is not None:
        in_specs.append(pl.BlockSpec((1, tn), lambda i, j: (0, j)))
        args.append(col_scale)
    return pl.pallas_call(
        _proj_kernel,
        grid=(m // tm, n // tn),
        in_specs=in_specs,
        out_specs=pl.BlockSpec((tm, tn), lambda i, j: (i, j)),
        out_shape=jax.ShapeDtypeStruct((m, n), out_dtype),
        compiler_params=_params("parallel", "arbitrary"),
        name="proj",
    )(*args)


MXU_N = 256


REGROUP_ROWS = 256


def _regroup_rows(h, d):
    tm = h.shape[0]
    if d == 1:
        return h
    per = REGROUP_ROWS // d
    r = lax.broadcasted_iota(jnp.int32, (REGROUP_ROWS, REGROUP_ROWS), 0)
    c = lax.broadcasted_iota(jnp.int32, (REGROUP_ROWS, REGROUP_ROWS), 1)
    perm = jnp.where(c == (r % per) * d + r // per, 1.0, 0.0).astype(BF16)
    nblk = tm // REGROUP_ROWS
    blocks = [jnp.dot(perm, h[i * REGROUP_ROWS:(i + 1) * REGROUP_ROWS],
                      preferred_element_type=F32).astype(BF16) for i in range(nblk)]
    return jnp.concatenate([blocks[i][rho * per:(rho + 1) * per]
                            for rho in range(d) for i in range(nblk)], axis=0)


def _proj_dil_kernel(h_ref, w_ref, c_ref, s_ref, o_ref):
    nres, rows = o_ref.shape[1:3]
    k, n = w_ref.shape
    tm = nres * rows
    gw = DIL_GROUP_WIDTH
    hp = _regroup_rows(h_ref[0], nres)
    for nb in range(n // MXU_N):
        lo = nb * MXU_N
        acc = jnp.dot(hp, w_ref[:, lo:lo + MXU_N], preferred_element_type=F32)
        part = lo // gw
        if part == 2:
            o_ref[0, :, :, lo:lo + MXU_N] = acc.astype(o_ref.dtype).reshape(nres, rows, MXU_N)
            continue
        c = c_ref[part].reshape(tm, LANES)
        sn = s_ref[part].reshape(tm, LANES)
        for cb in range(MXU_N // LANES):
            blk = acc[:, cb * LANES:(cb + 1) * LANES]
            rot = blk * c + pltpu.roll(blk, LANES // 2, 1) * sn
            o_ref[0, :, :, lo + cb * LANES:lo + (cb + 1) * LANES] = (
                rot.astype(o_ref.dtype).reshape(nres, rows, LANES))


def _proj_dil(h3, w, tabs, d, tm=1024):
    b, s, k = h3.shape
    n = w.shape[1]
    rows = tm // d
    c, sn = tabs
    tab_spec = pl.BlockSpec((2, d, rows, LANES), lambda bi, i: (0, 0, i, 0))
    return pl.pallas_call(
        _proj_dil_kernel,
        grid=(b, s // tm),
        in_specs=[pl.BlockSpec((1, tm, k), lambda bi, i: (bi, i, 0)),
                  pl.BlockSpec((k, n), lambda bi, i: (0, 0)),
                  tab_spec, tab_spec],
        out_specs=pl.BlockSpec((1, d, rows, n), lambda bi, i: (bi, 0, i, 0)),
        out_shape=jax.ShapeDtypeStruct((b, d, s // d, n), BF16),
        compiler_params=_params("parallel", "parallel"),
        name=f"proj_dil{d}",
    )(h3, w, c, sn)


def _rotary_tables(s, d):
    inv = ROPE_THETA ** (-jnp.arange(0, ROPE_DIM, 2, dtype=F32) / ROPE_DIM)
    ang = jnp.arange(s).astype(F32)[:, None] * inv[None, :]
    cos, sin = jnp.cos(ang), jnp.sin(ang)
    rest = LANES // 2 - 2 * ROPE_HALF
    ones = jnp.ones((s, rest), F32)
    zeros = jnp.zeros((s, rest), F32)
    c = jnp.concatenate([cos, cos, ones, cos, cos, ones], axis=1)
    sn = jnp.concatenate([-sin, -sin, zeros, sin, sin, zeros], axis=1)

    def arrange(t):
        t = t.reshape(s // d, d, LANES).transpose(1, 0, 2)
        return jnp.stack([t * SCORE_SCALE, t])

    return arrange(c), arrange(sn)


def _rotary_layout(t):
    lead = t.shape[:-1]
    t = t.reshape(lead + (t.shape[-1] // LANES, 2, HEAD_DIM))
    npair = t.shape[-3]
    a = t[..., :, :, 0:ROPE_HALF].reshape(lead + (npair, 2 * ROPE_HALF))
    b = t[..., :, :, ROPE_HALF:ROPE_DIM].reshape(lead + (npair, 2 * ROPE_HALF))
    rest0 = t[..., :, 0, ROPE_DIM:]
    rest1 = t[..., :, 1, ROPE_DIM:]
    return jnp.concatenate([a, rest0, b, rest1], axis=-1).reshape(lead + (npair * LANES,))


NA_UNROLL = 32


def _na_kernel(q_ref, k_ref, v_ref, g_ref, tb_ref, o_ref):
    rows = q_ref.shape[1] // GRID_W
    nkeys = NA_WIN_ROWS * GRID_W
    lane = lax.broadcasted_iota(jnp.int32, (GRID_W, LANES), 1)
    first = lane < HEAD_DIM

    def body(step, carry):
        units = []
        for i in range(NA_UNROLL):
            r = step * NA_UNROLL + i
            r0 = jnp.clip(r - NA_WIN_ROWS // 2, 0, rows - NA_WIN_ROWS)
            st = r0 - r + NA_WIN_ROWS - 1
            qs = pl.ds(pl.multiple_of(r * GRID_W, GRID_W), GRID_W)
            ks = pl.ds(pl.multiple_of(r0 * GRID_W, GRID_W), nkeys)
            q = q_ref[0, qs, :]
            kw = k_ref[0, ks, :]
            zero = jnp.zeros_like(q)
            q2 = jnp.concatenate([jnp.where(first, q, zero), jnp.where(first, zero, q)], axis=0)
            sc = lax.dot_general(q2, kw, (((1,), (1,)), ((), ())),
                                 preferred_element_type=F32)
            units.append((qs, ks, sc + tb_ref[0, st]))
        probs = []
        for qs, ks, sc in units:
            mx = jnp.max(sc, axis=-1, keepdims=True)
            e = jnp.exp2(sc - mx)
            probs.append((e.astype(BF16), jnp.sum(e, axis=-1, keepdims=True)))
        for (qs, ks, _), (p, den) in zip(units, probs):
            o2 = jnp.dot(p, v_ref[0, ks, :], preferred_element_type=F32) / den
            o = jnp.where(first, o2[0:GRID_W], o2[GRID_W:])
            o_ref[0, qs, :] = (o * _silu(g_ref[0, qs, :])).astype(o_ref.dtype)
        return carry

    lax.fori_loop(0, rows // NA_UNROLL, body, 0)


def _na_bias_table(rpb):
    h = rpb.shape[0]
    cq = np.arange(GRID_W)[:, None]
    ck = np.arange(GRID_W)[None, :]
    ws = np.clip(cq - NA_WIN_COLS // 2, 0, GRID_W - NA_WIN_COLS)
    in_win = (ck >= ws) & (ck < ws + NA_WIN_COLS)
    line = 2 * GRID_W
    lo = GRID_W - NA_WIN_COLS
    ext = jnp.pad(rpb.astype(F32) * LOG2E, ((0, 0), (0, 0), (lo, line - lo - rpb.shape[-1])))
    skew = jnp.broadcast_to(ext[:, :, None, :], ext.shape[:2] + (GRID_W, line))
    skew = skew.reshape(ext.shape[:2] + (GRID_W * line,))[:, :, :GRID_W * (line - 1)]
    toep = skew.reshape(ext.shape[:2] + (GRID_W, line - 1))[:, :, :, GRID_W - 1:]
    colb = jnp.where(in_win, toep, -jnp.inf)
    tb = jnp.stack([colb[:, st:st + NA_WIN_ROWS] for st in range(NA_WIN_ROWS)], axis=1)
    tb = tb.reshape(h // 2, 2, NA_WIN_ROWS, NA_WIN_ROWS, GRID_W, GRID_W)
    tb = tb.transpose(0, 2, 1, 4, 3, 5).reshape(h // 2, NA_WIN_ROWS, 2 * GRID_W, NA_WIN_ROWS * GRID_W)
    return tb


def _na_attention(qkvg, tb):
    b, s, _ = qkvg.shape
    npair = NA_WIDTH // LANES

    def col(off):
        return pl.BlockSpec((1, s, LANES), lambda hp, bi: (bi, 0, off + hp))

    return pl.pallas_call(
        _na_kernel,
        grid=(npair, b),
        in_specs=[col(0), col(npair), col(2 * npair), col(3 * npair),
                  pl.BlockSpec((1,) + tb.shape[1:], lambda hp, bi: (hp, 0, 0, 0))],
        out_specs=pl.BlockSpec((1, s, LANES), lambda hp, bi: (bi, 0, hp)),
        out_shape=jax.ShapeDtypeStruct((b, s, NA_WIDTH), BF16),
        compiler_params=_params("parallel", "parallel"),
        name="na_attention",
    )(qkvg, qkvg, qkvg, qkvg, tb)


DIL_QB = 2 * DIL_BLK
DIL_KB = 4 * DIL_BLK
DIL_UNROLL = 32


def _dil_kernel(q0, k0, v0, q1, k1, v1, q2, k2, v2, g_ref, o_ref, acc_ref, m_ref, w_ref):
    s = o_ref.shape[1]
    lane = lax.broadcasted_iota(jnp.int32, (DIL_QB, LANES), 1)
    first = lane < HEAD_DIM
    half_off = lane % (LANES // 2)
    q_first = (half_off < ROPE_HALF) | ((half_off >= 2 * ROPE_HALF) & (lane < LANES // 2))
    v_first = lax.broadcasted_iota(jnp.int32, (DIL_KB, LANES), 1) < HEAD_DIM
    qi = lax.broadcasted_iota(jnp.int32, (DIL_QB, DIL_KB), 0)
    ki = lax.broadcasted_iota(jnp.int32, (DIL_QB, DIL_KB), 1)

    groups = ((q0, k0, v0), (q1, k1, v1), (q2, k2, v2))
    first_group = len(groups) - 1
    for g in range(first_group, -1, -1):
        q_ref, k_ref, v_ref = groups[g]
        d = DIL_PAIRS[g][1]
        l = s // d
        nu = l // DIL_QB

        def trip(step, carry, g=g, d=d, l=l, nu=nu, q_ref=q_ref, k_ref=k_ref, v_ref=v_ref):
            units = []
            for i in range(DIL_UNROLL):
                idx = step * DIL_UNROLL + i
                rho = idx // nu
                u = idx % nu
                q_start = pl.multiple_of(u * DIL_QB, DIL_QB)
                k_start = pl.multiple_of(jnp.clip(u * DIL_QB - DIL_BLK, 0, l - DIL_KB), DIL_BLK)
                q = q_ref[0, rho, pl.ds(q_start, DIL_QB), :]
                kw = k_ref[0, rho, pl.ds(k_start, DIL_KB), :]
                valid = jnp.abs((ki + k_start) - (qi + q_start)) <= DIL_BLK
                zero = jnp.zeros_like(q)
                scs = []
                for h in range(2):
                    qh = jnp.where(q_first, q, zero) if h == 0 else jnp.where(q_first, zero, q)
                    sc = lax.dot_general(qh, kw, (((1,), (1,)), ((), ())),
                                         preferred_element_type=F32)
                    scs.append(jnp.where(valid, sc, -jnp.inf))
                units.append((rho, q_start, k_start, scs))
            soft = []
            for rho, q_start, k_start, scs in units:
                es, mxs = [], []
                for sc in scs:
                    mx = jnp.max(sc, axis=-1, keepdims=True)
                    es.append(jnp.exp2(sc - mx).astype(BF16))
                    mxs.append(mx)
                soft.append((es, jnp.where(first, mxs[0], mxs[1])))
            for (rho, q_start, k_start, _), (es, mx) in zip(units, soft):
                vw = v_ref[0, rho, pl.ds(k_start, DIL_KB), :]
                one = jnp.ones_like(vw)
                o0 = jnp.dot(es[0], jnp.where(v_first, vw, one), preferred_element_type=F32)
                o1 = jnp.dot(es[1], jnp.where(v_first, one, vw), preferred_element_type=F32)
                num = jnp.where(first, o0, o1)
                den = pltpu.roll(jnp.where(first, o1, o0), HEAD_DIM, 1)
                if d == 1:
                    rows = pl.ds(q_start, DIL_QB)
                else:
                    rows = pl.ds(rho + d * q_start, DIL_QB, stride=d)
                if g == first_group:
                    acc_ref[rows, :] = num
                    m_ref[rows, :] = mx
                    w_ref[rows, :] = den
                else:
                    m_old = m_ref[rows, :]
                    m_new = jnp.maximum(m_old, mx)
                    a = jnp.exp2(m_old - m_new)
                    bw = jnp.exp2(mx - m_new)
                    acc_ref[rows, :] = acc_ref[rows, :] * a + num * bw
                    w_ref[rows, :] = w_ref[rows, :] * a + den * bw
                    m_ref[rows, :] = m_new
            return carry

        lax.fori_loop(0, d * nu // DIL_UNROLL, trip, 0)

    def finish(i, carry):
        rows = pl.ds(pl.multiple_of(i * 256, 256), 256)
        y = acc_ref[rows, :] / w_ref[rows, :]
        o_ref[0, rows, :] = (y * _silu(g_ref[0, rows, :])).astype(o_ref.dtype)
        return carry

    lax.fori_loop(0, s // 256, finish, 0)


def _dil_attention(qkvs, gate, gate_off):
    b = gate.shape[0]
    s = gate.shape[1]
    npair = DIL_GROUP_WIDTH // LANES
    in_specs = []
    args = []
    for arr in qkvs:
        d, l = arr.shape[1], arr.shape[2]
        for part in range(3):
            in_specs.append(pl.BlockSpec((1, d, l, LANES),
                                         lambda bi, sp, part=part: (bi, 0, 0, part * npair + sp)))
            args.append(arr)
    in_specs.append(pl.BlockSpec((1, s, LANES), lambda bi, sp: (bi, 0, gate_off + sp)))
    args.append(gate)
    return pl.pallas_call(
        _dil_kernel,
        grid=(b, npair),
        in_specs=in_specs,
        out_specs=pl.BlockSpec((1, s, LANES), lambda bi, sp: (bi, 0, sp)),
        out_shape=jax.ShapeDtypeStruct((b, s, DIL_GROUP_WIDTH), BF16),
        scratch_shapes=[pltpu.VMEM((s, LANES), F32)] * 3,
        compiler_params=_params("parallel", "parallel"),
        name="dil_attention",
    )(*args)


def _split3(x):
    hi = x.astype(BF16)
    r1 = x - hi.astype(F32)
    mid = r1.astype(BF16)
    lo = (r1 - mid.astype(F32)).astype(BF16)
    return hi, mid, lo


SSM_HALO = 16
SSM_DT_COPIES = 2
SSM_UNROLL = 4


def _ssd_kernel(xs_ref, bm_ref, cm_ref, dt_ref, wx_ref, wb_ref, wc_ref, bx_ref, bb_ref, bc_ref,
                dtb_ref, a_ref, dsk_ref, y_ref, xs_s, bt_s, c_s, cb_s, acum_s, pt_s, st_s):
    s = xs_ref.shape[1]
    cl = SSM_CHUNK
    nchunk = s // cl
    hpg = SSM_HEADS_PER_GROUP
    npair = SSM_GROUP_WIDTH // LANES
    ncol = 2 * hpg

    def conv_chunk(c, carry):
        base = pl.multiple_of(c * cl, cl)
        prev_start = pl.multiple_of(jnp.maximum(base - SSM_HALO, 0), SSM_HALO)
        next_start = pl.multiple_of(jnp.minimum(base + cl, s - SSM_HALO), SSM_HALO)
        has_prev = jnp.where(c > 0, 1.0, 0.0)
        has_next = jnp.where(c < nchunk - 1, 1.0, 0.0)

        def conv(src, w_ref, bias_ref):
            ext = jnp.concatenate([src[0, pl.ds(prev_start, SSM_HALO), :].astype(F32) * has_prev,
                                   src[0, pl.ds(base, cl), :].astype(F32),
                                   src[0, pl.ds(next_start, SSM_HALO), :].astype(F32) * has_next], axis=0)
            acc = bias_ref[0]
            off = SSM_HALO - SSM_CONV // 2
            for tap in range(SSM_CONV):
                acc = acc + ext[off + tap:off + tap + cl, :] * w_ref[0, tap:tap + 1, :]
            return _silu(acc)

        xs_s[pl.ds(base, cl), :] = conv(xs_ref, wx_ref, bx_ref)
        bmat = conv(bm_ref, wb_ref, bb_ref).astype(BF16)
        cmat = conv(cm_ref, wc_ref, bc_ref).astype(BF16)
        bt_s[c] = bmat.astype(F32).T.astype(BF16)
        c_s[pl.ds(base, cl), :] = cmat
        cb_s[c] = lax.dot_general(cmat, bmat, (((1,), (1,)), ((), ())),
                                  preferred_element_type=F32)

        dt = jax.nn.softplus(dt_ref[0, pl.ds(base, cl), :] + dt_bias)
        pieces = _split3(dt * a_neg)
        acum_f = sum(jnp.dot(tril, p, preferred_element_type=F32) for p in pieces)
        acum_b = sum(jnp.dot(triu, p, preferred_element_type=F32) for p in pieces)
        acum = jnp.where(fwd_lane, acum_f, acum_b)
        a_end = jnp.where(fwd_lane[0:1], acum[cl - 1:cl, :], acum[0:1, :])
        w_state = dt * jnp.exp(a_end - acum)
        packed_t = jnp.where(lane < ncol, acum * LOG2E - jnp.log2(dt), w_state).T
        acum_s[c] = acum * LOG2E
        pt_s[c] = packed_t[0:SSM_DT_COPIES * ncol, :]
        return carry

    ri = lax.broadcasted_iota(jnp.int32, (cl, cl), 0)
    ci = lax.broadcasted_iota(jnp.int32, (cl, cl), 1)
    lane = lax.broadcasted_iota(jnp.int32, (cl, LANES), 1)
    first = lane < HEAD_DIM
    fwd_lane = (lane % ncol) < hpg
    tril = jnp.where(ci <= ri, 1.0, 0.0).astype(BF16)
    triu = jnp.where(ci >= ri, 1.0, 0.0).astype(BF16)
    dt_bias = dtb_ref[0]
    a_neg = a_ref[0]

    lax.fori_loop(0, nchunk, conv_chunk, 0, unroll=4)

    for direction in range(2):
        if direction == 0:
            keep = ci <= ri
            end_row = cl - 1
        else:
            keep = ci >= ri
            end_row = 0

        st_s[...] = jnp.zeros_like(st_s)

        def local_part(c, direction=direction, keep=keep, end_row=end_row):
            rows = pl.ds(pl.multiple_of(c * cl, cl), cl)
            acum = acum_s[c]
            packed_t = pt_s[c]

            xs = xs_s[rows, :]
            bt = bt_s[c].astype(F32)
            cb = cb_s[c]
            yd_blocks, new_blocks, scale_blocks = [], [], []
            for pr in range(npair):
                xblk = xs[:, pr * LANES:(pr + 1) * LANES].astype(BF16)
                lhs, cols = [], []
                for j in (direction * hpg + 2 * pr, direction * hpg + 2 * pr + 1):
                    col = jnp.broadcast_to(acum[:, j:j + 1], (cl, cl))
                    row = packed_t[j:j + 1, :]
                    ws_row = packed_t[ncol + j:ncol + j + 1, :]
                    seg_dt = jnp.exp2(jnp.where(keep, col - row, -jnp.inf))
                    lhs.append((cb * seg_dt).astype(BF16))
                    lhs.append((bt * ws_row).astype(BF16))
                    cols.append(col)
                prod = jnp.dot(jnp.concatenate(lhs, axis=0), xblk, preferred_element_type=F32)
                yd_blocks.append(jnp.where(first, prod[0:cl], prod[2 * cl:3 * cl]))
                new_blocks.append(jnp.where(first, prod[cl:2 * cl], prod[3 * cl:4 * cl]))
                scale_blocks.append(jnp.exp2(jnp.where(first, cols[0], cols[1])))
            return rows, xs, yd_blocks, new_blocks, scale_blocks

        def trip(step, carry, direction=direction, end_row=end_row):
            chunks = [step * SSM_UNROLL + i for i in range(SSM_UNROLL)]
            if direction == 1:
                chunks = [nchunk - 1 - c for c in chunks]
            parts = [local_part(c) for c in chunks]
            state = [st_s[:, pr * LANES:(pr + 1) * LANES] for pr in range(npair)]
            for rows, xs, yd_blocks, new_blocks, scale_blocks in parts:
                cmat = c_s[rows, :]
                y_blocks = []
                for pr in range(npair):
                    y_off = jnp.dot(cmat, state[pr].astype(BF16), preferred_element_type=F32)
                    y_blocks.append(yd_blocks[pr] + y_off * scale_blocks[pr])
                    state[pr] = (state[pr] * scale_blocks[pr][end_row:end_row + 1, :]
                                 + new_blocks[pr])
                y = jnp.concatenate(y_blocks, axis=1)
                if direction == 0:
                    y_ref[0, rows, :] = y + dsk_ref[0] * xs
                else:
                    y_ref[0, rows, :] = y_ref[0, rows, :] + y
            for pr in range(npair):
                st_s[:, pr * LANES:(pr + 1) * LANES] = state[pr]
            return carry

        lax.fori_loop(0, nchunk // SSM_UNROLL, trip, 0)


def _ssd(xbc, dtp, conv_w, conv_b, dt_bias, a_neg, d_skip):
    b, s, _ = xbc.shape
    gw = SSM_GROUP_WIDTH
    b_off = SSM_INNER // LANES
    c_off = b_off + SSM_GROUPS
    nchunk = s // SSM_CHUNK

    def seq(width, off):
        return pl.BlockSpec((1, s, width), lambda bi, g: (bi, 0, off + g))

    def par(rows, width, off):
        return pl.BlockSpec((1, rows, width), lambda bi, g: (0, 0, off + g))

    return pl.pallas_call(
        _ssd_kernel,
        grid=(b, SSM_GROUPS),
        in_specs=[seq(gw, 0), seq(LANES, b_off), seq(LANES, c_off), seq(LANES, 0),
                  par(SSM_CONV, gw, 0), par(SSM_CONV, LANES, b_off), par(SSM_CONV, LANES, c_off),
                  par(1, gw, 0), par(1, LANES, b_off), par(1, LANES, c_off),
                  pl.BlockSpec((1, 1, LANES), lambda bi, g: (g, 0, 0)),
                  pl.BlockSpec((1, 1, LANES), lambda bi, g: (g, 0, 0)),
                  pl.BlockSpec((1, 1, gw), lambda bi, g: (g, 0, 0))],
        out_specs=pl.BlockSpec((1, s, gw), lambda bi, g: (bi, 0, g)),
        out_shape=jax.ShapeDtypeStruct((b, s, SSM_INNER), F32),
        scratch_shapes=[pltpu.VMEM((s, gw), F32),
                        pltpu.VMEM((nchunk, SSM_STATE, SSM_CHUNK), BF16),
                        pltpu.VMEM((s, LANES), BF16),
                        pltpu.VMEM((nchunk, SSM_CHUNK, SSM_CHUNK), F32),
                        pltpu.VMEM((nchunk, SSM_CHUNK, LANES), F32),
                        pltpu.VMEM((nchunk, SSM_DT_COPIES * 2 * SSM_HEADS_PER_GROUP, SSM_CHUNK), F32),
                        pltpu.VMEM((SSM_STATE, gw), F32)],
        compiler_params=_params("parallel", "parallel"),
        name="ssd",
    )(xbc, xbc, xbc, dtp, conv_w, conv_w, conv_w, conv_b, conv_b, conv_b, dt_bias, a_neg, d_skip)


def _tail_kernel(x_ref, ya_ref, yb_ref, yc_ref, z_ref, ua_ref, ub_ref, uc_ref, p_ref,
                 nw_ref, woa_ref, wob_ref, woc_ref, wout_ref, pg_ref, wpg_ref, wple_ref, gn_ref,
                 o_ref, *h_ref, final):
    def mm(a, w_ref):
        return jnp.dot(a, w_ref[...], preferred_element_type=F32)

    ya = mm(ya_ref[...], woa_ref)
    yb = mm(yb_ref[...], wob_ref)
    yc_in = _rms(yc_ref[...] * _silu(z_ref[...]), nw_ref[...]).astype(BF16)
    yc = mm(yc_in, woc_ref)
    merged = (_sigmoid(ua_ref[...]) * ya + _sigmoid(ub_ref[...]) * yb
              + _sigmoid(uc_ref[...]) * yc)
    x1 = x_ref[...] + mm(merged.astype(BF16), wout_ref)
    gate = jax.nn.sigmoid(mm(_rms(x1, pg_ref[...]).astype(BF16), wpg_ref))
    x2 = x1 + mm(p_ref[...].astype(BF16), wple_ref) * gate
    if final:
        o_ref[...] = _rms(x2, gn_ref[...])
    else:
        o_ref[...] = x2
        h_ref[0][...] = _rms(x2, gn_ref[...]).astype(BF16)


MISC_WIDTH = SSM_INNER + DIL_GROUP_WIDTH + 3 * D_MODEL
MISC_GB_LANE_BLOCK = SSM_INNER // LANES
MISC_U_BLOCK = (SSM_INNER + DIL_GROUP_WIDTH) // D_MODEL


def _tail(x2d, ya, yb, yc, misc, p2d, nw, woa, wob, woc, wout, pg, wpg, wple, gn, final, tm=512):
    m = x2d.shape[0]
    out_specs = pl.BlockSpec((tm, D_MODEL), lambda i: (i, 0))
    out_shape = jax.ShapeDtypeStruct((m, D_MODEL), F32)
    if not final:
        out_specs = [out_specs, pl.BlockSpec((tm, D_MODEL), lambda i: (i, 0))]
        out_shape = [out_shape, jax.ShapeDtypeStruct((m, D_MODEL), BF16)]

    def rows(width, off=0):
        return pl.BlockSpec((tm, width), lambda i: (i, off))

    def whole(arr):
        return pl.BlockSpec(arr.shape, lambda i: (0, 0), pipeline_mode=pl.Buffered(1))

    ub = MISC_U_BLOCK
    return pl.pallas_call(
        functools.partial(_tail_kernel, final=final),
        grid=(m // tm,),
        in_specs=[rows(D_MODEL), rows(NA_WIDTH), rows(DIL_GROUP_WIDTH), rows(SSM_INNER),
                  rows(SSM_INNER, 0), rows(D_MODEL, ub), rows(D_MODEL, ub + 1), rows(D_MODEL, ub + 2),
                  rows(PLE_DIM),
                  whole(nw), whole(woa), whole(wob), whole(woc), whole(wout), whole(pg), whole(wpg),
                  whole(wple), whole(gn)],
        out_specs=out_specs,
        out_shape=out_shape,
        compiler_params=_params("parallel"),
        name="tail",
    )(x2d, ya, yb, yc, misc, misc, misc, misc, p2d, nw, woa, wob, woc, wout, pg, wpg, wple, gn)


def _prep_weights(w_in, conv_w, conv_b, a_log, dt_bias, d_skip):
    depth = w_in.shape[0]
    offs = np.concatenate([[0], np.cumsum(IN_SPLITS)])
    w16 = w_in.astype(BF16)
    (qa, ka, va, ga, qb, kb, vb, gb, xbc, z, dtr, ua, ub, uc) = [
        w16[:, :, int(offs[i]):int(offs[i + 1])] for i in range(len(IN_SPLITS))]
    w_a = w16[:, :, :int(offs[4])]
    gw = DIL_GROUP_WIDTH
    w_b = [jnp.concatenate([_rotary_layout(qb[:, :, g * gw:(g + 1) * gw]),
                            _rotary_layout(kb[:, :, g * gw:(g + 1) * gw]),
                            vb[:, :, g * gw:(g + 1) * gw]], axis=2)
           for g in range(len(DIL_PAIRS))]
    w_misc = jnp.concatenate([z, gb, ua, ub, uc], axis=2)

    hpg = SSM_HEADS_PER_GROUP

    def per_group(t):
        lead = t.shape[:-2]
        t = t.reshape(lead + (2, SSM_GROUPS, hpg))
        t = jnp.moveaxis(t, -2, -3).reshape(lead + (SSM_GROUPS, 2 * hpg))
        t = jnp.tile(t, (1,) * (t.ndim - 1) + (SSM_DT_COPIES,))
        pad = [(0, 0)] * (t.ndim - 1) + [(0, LANES - 2 * hpg * SSM_DT_COPIES)]
        return jnp.pad(t, pad).reshape(lead + (SSM_GROUPS * LANES,))

    w_dt = per_group(dtr.reshape(depth, D_MODEL, 2, SSM_HEADS))
    w_xbc = xbc
    dtb = per_group(dt_bias.astype(F32)).reshape(depth, SSM_GROUPS, 1, LANES)
    a_neg = per_group(-jnp.exp(a_log.astype(F32))).reshape(depth, SSM_GROUPS, 1, LANES)
    dsk = jnp.repeat(d_skip.astype(F32), HEAD_DIM, axis=1).reshape(depth, SSM_GROUPS, 1, SSM_GROUP_WIDTH)
    cw = conv_w.astype(F32).reshape(depth, 1, SSM_CONV, SSM_CONV_DIM)
    cbias = conv_b.astype(F32).reshape(depth, 1, 1, SSM_CONV_DIM)
    return w_a, w_b, w_misc, w_xbc, w_dt, dtb, a_neg, dsk, cw, cbias


def kernel(x, p, norm_w, w_in, na_rpb, conv_w, conv_b, a_log, dt_bias, d_skip, ssm_norm_w,
           w_oa, w_ob, w_oc, w_out, ple_norm_w, w_ple, w_ple_gate, final_norm_w):
    b, s, dm = x.shape
    depth = w_in.shape[0]
    m = b * s
    w_a, w_b, w_misc, w_xbc, w_dt, dtb, a_neg, dsk, cw, cbias = _prep_weights(
        w_in, conv_w, conv_b, a_log, dt_bias, d_skip)
    tabs = [_rotary_tables(s, d) for _, d in DIL_PAIRS]
    na_col_scale = jnp.concatenate([jnp.full((1, NA_WIDTH), SCORE_SCALE, F32),
                                    jnp.ones((1, 3 * NA_WIDTH), F32)], axis=1)
    bias_tables = _na_bias_table(na_rpb.reshape((-1,) + na_rpb.shape[2:]))
    bias_tables = bias_tables.reshape((depth, NA_HEADS // 2) + bias_tables.shape[1:])
    row = lambda v: v.astype(F32).reshape(1, -1)

    x2d = x.reshape(m, dm)
    h = _norm(x2d, row(norm_w[0]))
    for i in range(depth):
        final = i == depth - 1
        qkvg = _proj(h, w_a[i], 2 * NA_WIDTH, BF16, col_scale=na_col_scale).reshape(b, s, 4 * NA_WIDTH)
        misc = _proj(h, w_misc[i], MISC_WIDTH // 2, BF16)
        xbc = _proj(h, w_xbc[i], SSM_CONV_DIM, BF16).reshape(b, s, SSM_CONV_DIM)
        dtp = _proj(h, w_dt[i], SSM_GROUPS * LANES, F32).reshape(b, s, SSM_GROUPS * LANES)
        h3 = h.reshape(b, s, dm)
        qkvs = [_proj_dil(h3, w_b[gi][i], tabs[gi], d) for gi, (_, d) in enumerate(DIL_PAIRS)]

        ya = _na_attention(qkvg, bias_tables[i]).reshape(m, NA_WIDTH)
        yb = _dil_attention(qkvs, misc.reshape(b, s, MISC_WIDTH), MISC_GB_LANE_BLOCK)
        yb = yb.reshape(m, DIL_GROUP_WIDTH)
        yc = _ssd(xbc, dtp, cw[i], cbias[i], dtb[i], a_neg[i], dsk[i]).reshape(m, SSM_INNER)

        gn = row(final_norm_w) if final else row(norm_w[i + 1])
        out = _tail(x2d, ya, yb, yc, misc, p[i].reshape(m, PLE_DIM), row(ssm_norm_w[i]),
                    w_oa[i].astype(BF16), w_ob[i].astype(BF16), w_oc[i].astype(BF16),
                    w_out[i].astype(BF16), row(ple_norm_w[i]), w_ple_gate[i].astype(BF16),
                    w_ple[i].astype(BF16), gn, final=final)
        if final:
            x2d = out
        else:
            x2d, h = out
    return x2d.reshape(b, s, dm)
```

```python
import functools
import math

import numpy as np
import jax
import jax.numpy as jnp
from jax import lax
from jax.experimental import pallas as pl
from jax.experimental.pallas import tpu as pltpu

F32 = jnp.float32
BF16 = jnp.bfloat16

LANES = 128
VMEM_LIMIT_BYTES = 56 * 1024 * 1024

D_MODEL = 1024
GRID_W = 64
HEAD_DIM = 64
EPS = 1e-6
PLE_DIM = 256

NA_HEADS = 16
NA_WIDTH = NA_HEADS * HEAD_DIM
NA_WIN_ROWS = 8
NA_WIN_COLS = 16

DIL_PAIRS = ((128, 1), (512, 4), (2048, 16))
DIL_HEADS_PER_GROUP = 8
DIL_GROUP_WIDTH = DIL_HEADS_PER_GROUP * HEAD_DIM
DIL_WIDTH = DIL_GROUP_WIDTH * len(DIL_PAIRS)
DIL_BLK = 64
ROPE_THETA = 500000.0
ROPE_DIM = HEAD_DIM // 4
ROPE_HALF = ROPE_DIM // 2

SSM_INNER = 1536
SSM_HEADS = 24
SSM_GROUPS = 4
SSM_HEADS_PER_GROUP = SSM_HEADS // SSM_GROUPS
SSM_GROUP_WIDTH = SSM_HEADS_PER_GROUP * HEAD_DIM
SSM_STATE = 128
SSM_CONV = 5
SSM_CHUNK = 128
SSM_CONV_DIM = SSM_INNER + 2 * SSM_GROUPS * SSM_STATE

IN_SPLITS = (NA_WIDTH, NA_WIDTH, NA_WIDTH, NA_WIDTH,
             DIL_WIDTH, DIL_WIDTH, DIL_WIDTH, DIL_GROUP_WIDTH,
             SSM_CONV_DIM, SSM_INNER, 2 * SSM_HEADS,
             D_MODEL, D_MODEL, D_MODEL)

LOG2E = math.log2(math.e)
SCORE_SCALE = HEAD_DIM ** -0.5 * LOG2E


def _params(*semantics):
    return pltpu.CompilerParams(dimension_semantics=semantics,
                                vmem_limit_bytes=VMEM_LIMIT_BYTES)


def _rms(x, g):
    return x * lax.rsqrt(jnp.mean(x * x, axis=-1, keepdims=True) + EPS) * g


def _silu(x):
    x = x.astype(F32)
    return x * jax.nn.sigmoid(x)


def _sigmoid(x):
    return jax.nn.sigmoid(x.astype(F32))


def _norm_kernel(x_ref, g_ref, h_ref):
    h_ref[...] = _rms(x_ref[...], g_ref[...]).astype(h_ref.dtype)


def _norm(x2d, g, tm=1024):
    m, k = x2d.shape
    return pl.pallas_call(
        _norm_kernel,
        grid=(m // tm,),
        in_specs=[pl.BlockSpec((tm, k), lambda i: (i, 0)), pl.BlockSpec((1, k), lambda i: (0, 0))],
        out_specs=pl.BlockSpec((tm, k), lambda i: (i, 0)),
        out_shape=jax.ShapeDtypeStruct((m, k), BF16),
        compiler_params=_params("parallel"),
        name="norm",
    )(x2d, g)


def _proj_kernel(h_ref, w_ref, *rest):
    o_ref = rest[-1]
    acc = jnp.dot(h_ref[...], w_ref[...], preferred_element_type=F32)
    if len(rest) == 2:
        acc = acc * rest[0][...]
    o_ref[...] = acc.astype(o_ref.dtype)


def _proj(h2d, w, tn, out_dtype, tm=2048, col_scale=None):
    m, k = h2d.shape
    n = w.shape[1]
    in_specs = [pl.BlockSpec((tm, k), lambda i, j: (i, 0)),
                pl.BlockSpec((k, tn), lambda i, j: (0, j))]
    args = [h2d, w]
    if col_scale is not None:
        in_specs.append(pl.BlockSpec((1, tn), lambda i, j: (0, j)))
        args.append(col_scale)
    return pl.pallas_call(
        _proj_kernel,
        grid=(m // tm, n // tn),
        in_specs=in_specs,
        out_specs=pl.BlockSpec((tm, tn), lambda i, j: (i, j)),
        out_shape=jax.ShapeDtypeStruct((m, n), out_dtype),
        compiler_params=_params("parallel", "arbitrary"),
        name="proj",
    )(*args)


def _proj_split_kernel(h_ref, w_ref, lo_ref, hi_ref):
    n_lo = lo_ref.shape[1]
    acc = jnp.dot(h_ref[...], w_ref[...], preferred_element_type=F32)
    lo_ref[...] = acc[:, :n_lo].astype(lo_ref.dtype)
    hi_ref[...] = acc[:, n_lo:].astype(hi_ref.dtype)


def _proj_split(h2d, w, n_lo, lo_dtype, hi_dtype, tm=1024):
    m, k = h2d.shape
    n = w.shape[1]
    return pl.pallas_call(
        _proj_split_kernel,
        grid=(m // tm,),
        in_specs=[pl.BlockSpec((tm, k), lambda i: (i, 0)),
                  pl.BlockSpec((k, n), lambda i: (0, 0), pipeline_mode=pl.Buffered(1))],
        out_specs=[pl.BlockSpec((tm, n_lo), lambda i: (i, 0)),
                   pl.BlockSpec((tm, n - n_lo), lambda i: (i, 0))],
        out_shape=[jax.ShapeDtypeStruct((m, n_lo), lo_dtype),
                   jax.ShapeDtypeStruct((m, n - n_lo), hi_dtype)],
        compiler_params=_params("parallel"),
        name="proj_split",
    )(h2d, w)


MXU_N = 256


REGROUP_ROWS = 256


def _regroup_rows(h, d):
    tm = h.shape[0]
    if d == 1:
        return h
    per = REGROUP_ROWS // d
    r = lax.broadcasted_iota(jnp.int32, (REGROUP_ROWS, REGROUP_ROWS), 0)
    c = lax.broadcasted_iota(jnp.int32, (REGROUP_ROWS, REGROUP_ROWS), 1)
    perm = jnp.where(c == (r % per) * d + r // per, 1.0, 0.0).astype(BF16)
    nblk = tm // REGROUP_ROWS
    blocks = [jnp.dot(perm, h[i * REGROUP_ROWS:(i + 1) * REGROUP_ROWS],
                      preferred_element_type=F32).astype(BF16) for i in range(nblk)]
    return jnp.concatenate([blocks[i][rho * per:(rho + 1) * per]
                            for rho in range(d) for i in range(nblk)], axis=0)


def _proj_dil_kernel(h_ref, w_ref, c_ref, s_ref, o_ref):
    nres, rows = o_ref.shape[1:3]
    k, n = w_ref.shape
    tm = nres * rows
    gw = DIL_GROUP_WIDTH
    hp = _regroup_rows(h_ref[0], nres)
    for nb in range(n // MXU_N):
        lo = nb * MXU_N
        acc = jnp.dot(hp, w_ref[:, lo:lo + MXU_N], preferred_element_type=F32)
        part = lo // gw
        if part == 2:
            o_ref[0, :, :, lo:lo + MXU_N] = acc.astype(o_ref.dtype).reshape(nres, rows, MXU_N)
            continue
        c = c_ref[part].reshape(tm, LANES)
        sn = s_ref[part].reshape(tm, LANES)
        for cb in range(MXU_N // LANES):
            blk = acc[:, cb * LANES:(cb + 1) * LANES]
            rot = blk * c + pltpu.roll(blk, LANES // 2, 1) * sn
            o_ref[0, :, :, lo + cb * LANES:lo + (cb + 1) * LANES] = (
                rot.astype(o_ref.dtype).reshape(nres, rows, LANES))


def _proj_dil(h3, w, tabs, d, tm=1024):
    b, s, k = h3.shape
    n = w.shape[1]
    rows = tm // d
    c, sn = tabs
    tab_spec = pl.BlockSpec((2, d, rows, LANES), lambda bi, i: (0, 0, i, 0))
    return pl.pallas_call(
        _proj_dil_kernel,
        grid=(b, s // tm),
        in_specs=[pl.BlockSpec((1, tm, k), lambda bi, i: (bi, i, 0)),
                  pl.BlockSpec((k, n), lambda bi, i: (0, 0)),
                  tab_spec, tab_spec],
        out_specs=pl.BlockSpec((1, d, rows, n), lambda bi, i: (bi, 0, i, 0)),
        out_shape=jax.ShapeDtypeStruct((b, d, s // d, n), BF16),
        compiler_params=_params("parallel", "parallel"),
        name=f"proj_dil{d}",
    )(h3, w, c, sn)


def _rotary_tables(s, d):
    inv = ROPE_THETA ** (-jnp.arange(0, ROPE_DIM, 2, dtype=F32) / ROPE_DIM)
    ang = jnp.arange(s).astype(F32)[:, None] * inv[None, :]
    cos, sin = jnp.cos(ang), jnp.sin(ang)
    rest = LANES // 2 - 2 * ROPE_HALF
    ones = jnp.ones((s, rest), F32)
    zeros = jnp.zeros((s, rest), F32)
    c = jnp.concatenate([cos, cos, ones, cos, cos, ones], axis=1)
    sn = jnp.concatenate([-sin, -sin, zeros, sin, sin, zeros], axis=1)

    def arrange(t):
        t = t.reshape(s // d, d, LANES).transpose(1, 0, 2)
        return jnp.stack([t * SCORE_SCALE, t])

    return arrange(c), arrange(sn)


def _rotary_layout(t):
    lead = t.shape[:-1]
    t = t.reshape(lead + (t.shape[-1] // LANES, 2, HEAD_DIM))
    npair = t.shape[-3]
    a = t[..., :, :, 0:ROPE_HALF].reshape(lead + (npair, 2 * ROPE_HALF))
    b = t[..., :, :, ROPE_HALF:ROPE_DIM].reshape(lead + (npair, 2 * ROPE_HALF))
    rest0 = t[..., :, 0, ROPE_DIM:]
    rest1 = t[..., :, 1, ROPE_DIM:]
    return jnp.concatenate([a, rest0, b, rest1], axis=-1).reshape(lead + (npair * LANES,))


NA_UNROLL = 32


def _na_kernel(q_ref, k_ref, v_ref, g_ref, tb_ref, o_ref):
    rows = q_ref.shape[1] // GRID_W
    nkeys = NA_WIN_ROWS * GRID_W
    lane = lax.broadcasted_iota(jnp.int32, (GRID_W, LANES), 1)
    first = lane < HEAD_DIM

    def body(step, carry):
        units = []
        for i in range(NA_UNROLL):
            r = step * NA_UNROLL + i
            r0 = jnp.clip(r - NA_WIN_ROWS // 2, 0, rows - NA_WIN_ROWS)
            st = r0 - r + NA_WIN_ROWS - 1
            qs = pl.ds(pl.multiple_of(r * GRID_W, GRID_W), GRID_W)
            ks = pl.ds(pl.multiple_of(r0 * GRID_W, GRID_W), nkeys)
            q = q_ref[0, qs, :]
            kw = k_ref[0, ks, :]
            zero = jnp.zeros_like(q)
            q2 = jnp.concatenate([jnp.where(first, q, zero), jnp.where(first, zero, q)], axis=0)
            sc = lax.dot_general(q2, kw, (((1,), (1,)), ((), ())),
                                 preferred_element_type=F32)
            units.append((qs, ks, sc + tb_ref[0, st]))
        probs = []
        for qs, ks, sc in units:
            mx = jnp.max(sc, axis=-1, keepdims=True)
            e = jnp.exp2(sc - mx)
            probs.append((e.astype(BF16), jnp.sum(e, axis=-1, keepdims=True)))
        for (qs, ks, _), (p, den) in zip(units, probs):
            o2 = jnp.dot(p, v_ref[0, ks, :], preferred_element_type=F32) / den
            o = jnp.where(first, o2[0:GRID_W], o2[GRID_W:])
            o_ref[0, qs, :] = (o * _silu(g_ref[0, qs, :])).astype(o_ref.dtype)
        return carry

    lax.fori_loop(0, rows // NA_UNROLL, body, 0)


def _na_bias_table(rpb):
    h = rpb.shape[0]
    cq = np.arange(GRID_W)[:, None]
    ck = np.arange(GRID_W)[None, :]
    ws = np.clip(cq - NA_WIN_COLS // 2, 0, GRID_W - NA_WIN_COLS)
    in_win = (ck >= ws) & (ck < ws + NA_WIN_COLS)
    line = 2 * GRID_W
    lo = GRID_W - NA_WIN_COLS
    ext = jnp.pad(rpb.astype(F32) * LOG2E, ((0, 0), (0, 0), (lo, line - lo - rpb.shape[-1])))
    skew = jnp.broadcast_to(ext[:, :, None, :], ext.shape[:2] + (GRID_W, line))
    skew = skew.reshape(ext.shape[:2] + (GRID_W * line,))[:, :, :GRID_W * (line - 1)]
    toep = skew.reshape(ext.shape[:2] + (GRID_W, line - 1))[:, :, :, GRID_W - 1:]
    colb = jnp.where(in_win, toep, -jnp.inf)
    tb = jnp.stack([colb[:, st:st + NA_WIN_ROWS] for st in range(NA_WIN_ROWS)], axis=1)
    tb = tb.reshape(h // 2, 2, NA_WIN_ROWS, NA_WIN_ROWS, GRID_W, GRID_W)
    tb = tb.transpose(0, 2, 1, 4, 3, 5).reshape(h // 2, NA_WIN_ROWS, 2 * GRID_W, NA_WIN_ROWS * GRID_W)
    return tb


def _na_attention(qkvg, tb):
    b, s, _ = qkvg.shape
    npair = NA_WIDTH // LANES

    def col(off):
        return pl.BlockSpec((1, s, LANES), lambda hp, bi: (bi, 0, off + hp))

    return pl.pallas_call(
        _na_kernel,
        grid=(npair, b),
        in_specs=[col(0), col(npair), col(2 * npair), col(3 * npair),
                  pl.BlockSpec((1,) + tb.shape[1:], lambda hp, bi: (hp, 0, 0, 0))],
        out_specs=pl.BlockSpec((1, s, LANES), lambda hp, bi: (bi, 0, hp)),
        out_shape=jax.ShapeDtypeStruct((b, s, NA_WIDTH), BF16),
        compiler_params=_params("parallel", "parallel"),
        name="na_attention",
    )(qkvg, qkvg, qkvg, qkvg, tb)


DIL_QB = 2 * DIL_BLK
DIL_KB = 4 * DIL_BLK
DIL_UNROLL = 32


def _dil_kernel(q0, k0, v0, q1, k1, v1, q2, k2, v2, g_ref, o_ref, acc_ref, m_ref, w_ref):
    s = o_ref.shape[1]
    lane = lax.broadcasted_iota(jnp.int32, (DIL_QB, LANES), 1)
    first = lane < HEAD_DIM
    half_off = lane % (LANES // 2)
    q_first = (half_off < ROPE_HALF) | ((half_off >= 2 * ROPE_HALF) & (lane < LANES // 2))
    v_first = lax.broadcasted_iota(jnp.int32, (DIL_KB, LANES), 1) < HEAD_DIM
    qi = lax.broadcasted_iota(jnp.int32, (DIL_QB, DIL_KB), 0)
    ki = lax.broadcasted_iota(jnp.int32, (DIL_QB, DIL_KB), 1)

    groups = ((q0, k0, v0), (q1, k1, v1), (q2, k2, v2))
    first_group = len(groups) - 1
    for g in range(first_group, -1, -1):
        q_ref, k_ref, v_ref = groups[g]
        d = DIL_PAIRS[g][1]
        l = s // d
        nu = l // DIL_QB

        def trip(step, carry, g=g, d=d, l=l, nu=nu, q_ref=q_ref, k_ref=k_ref, v_ref=v_ref):
            units = []
            for i in range(DIL_UNROLL):
                idx = step * DIL_UNROLL + i
                rho = idx // nu
                u = idx % nu
                q_start = pl.multiple_of(u * DIL_QB, DIL_QB)
                k_start = pl.multiple_of(jnp.clip(u * DIL_QB - DIL_BLK, 0, l - DIL_KB), DIL_BLK)
                q = q_ref[0, rho, pl.ds(q_start, DIL_QB), :]
                kw = k_ref[0, rho, pl.ds(k_start, DIL_KB), :]
                valid = jnp.abs((ki + k_start) - (qi + q_start)) <= DIL_BLK
                zero = jnp.zeros_like(q)
                scs = []
                for h in range(2):
                    qh = jnp.where(q_first, q, zero) if h == 0 else jnp.where(q_first, zero, q)
                    sc = lax.dot_general(qh, kw, (((1,), (1,)), ((), ())),
                                         preferred_element_type=F32)
                    scs.append(jnp.where(valid, sc, -jnp.inf))
                units.append((rho, q_start, k_start, scs))
            soft = []
            for rho, q_start, k_start, scs in units:
                es, mxs = [], []
                for sc in scs:
                    mx = jnp.max(sc, axis=-1, keepdims=True)
                    es.append(jnp.exp2(sc - mx).astype(BF16))
                    mxs.append(mx)
                soft.append((es, jnp.where(first, mxs[0], mxs[1])))
            for (rho, q_start, k_start, _), (es, mx) in zip(units, soft):
                vw = v_ref[0, rho, pl.ds(k_start, DIL_KB), :]
                one = jnp.ones_like(vw)
                o0 = jnp.dot(es[0], jnp.where(v_first, vw, one), preferred_element_type=F32)
                o1 = jnp.dot(es[1], jnp.where(v_first, one, vw), preferred_element_type=F32)
                num = jnp.where(first, o0, o1)
                den = pltpu.roll(jnp.where(first, o1, o0), HEAD_DIM, 1)
                if d == 1:
                    rows = pl.ds(q_start, DIL_QB)
                else:
                    rows = pl.ds(rho + d * q_start, DIL_QB, stride=d)
                if g == first_group:
                    acc_ref[rows, :] = num
                    m_ref[rows, :] = mx
                    w_ref[rows, :] = den
                else:
                    m_old = m_ref[rows, :]
                    m_new = jnp.maximum(m_old, mx)
                    a = jnp.exp2(m_old - m_new)
                    bw = jnp.exp2(mx - m_new)
                    acc_ref[rows, :] = acc_ref[rows, :] * a + num * bw
                    w_ref[rows, :] = w_ref[rows, :] * a + den * bw
                    m_ref[rows, :] = m_new
            return carry

        lax.fori_loop(0, d * nu // DIL_UNROLL, trip, 0)

    def finish(i, carry):
        rows = pl.ds(pl.multiple_of(i * 256, 256), 256)
        y = acc_ref[rows, :] / w_ref[rows, :]
        o_ref[0, rows, :] = (y * _silu(g_ref[0, rows, :])).astype(o_ref.dtype)
        return carry

    lax.fori_loop(0, s // 256, finish, 0)


def _dil_attention(qkvs, gate, gate_off):
    b = gate.shape[0]
    s = gate.shape[1]
    npair = DIL_GROUP_WIDTH // LANES
    in_specs = []
    args = []
    for arr in qkvs:
        d, l = arr.shape[1], arr.shape[2]
        for part in range(3):
            in_specs.append(pl.BlockSpec((1, d, l, LANES),
                                         lambda bi, sp, part=part: (bi, 0, 0, part * npair + sp)))
            args.append(arr)
    in_specs.append(pl.BlockSpec((1, s, LANES), lambda bi, sp: (bi, 0, gate_off + sp)))
    args.append(gate)
    return pl.pallas_call(
        _dil_kernel,
        grid=(b, npair),
        in_specs=in_specs,
        out_specs=pl.BlockSpec((1, s, LANES), lambda bi, sp: (bi, 0, sp)),
        out_shape=jax.ShapeDtypeStruct((b, s, DIL_GROUP_WIDTH), BF16),
        scratch_shapes=[pltpu.VMEM((s, LANES), F32)] * 3,
        compiler_params=_params("parallel", "parallel"),
        name="dil_attention",
    )(*args)


def _split3(x):
    hi = x.astype(BF16)
    r1 = x - hi.astype(F32)
    mid = r1.astype(BF16)
    lo = (r1 - mid.astype(F32)).astype(BF16)
    return hi, mid, lo


SSM_HALO = 16
SSM_DT_COPIES = 2
SSM_UNROLL = 4


def _ssd_kernel(xs_ref, bm_ref, cm_ref, dt_ref, wx_ref, wb_ref, wc_ref, bx_ref, bb_ref, bc_ref,
                dtb_ref, a_ref, dsk_ref, y_ref, xs_s, bt_s, c_s, cb_s, acum_s, pt_s, st_s):
    s = xs_ref.shape[1]
    cl = SSM_CHUNK
    nchunk = s // cl
    hpg = SSM_HEADS_PER_GROUP
    npair = SSM_GROUP_WIDTH // LANES
    ncol = 2 * hpg

    def conv_chunk(c, carry):
        base = pl.multiple_of(c * cl, cl)
        prev_start = pl.multiple_of(jnp.maximum(base - SSM_HALO, 0), SSM_HALO)
        next_start = pl.multiple_of(jnp.minimum(base + cl, s - SSM_HALO), SSM_HALO)
        has_prev = jnp.where(c > 0, 1.0, 0.0)
        has_next = jnp.where(c < nchunk - 1, 1.0, 0.0)

        def conv(src, w_ref, bias_ref):
            ext = jnp.concatenate([src[0, pl.ds(prev_start, SSM_HALO), :].astype(F32) * has_prev,
                                   src[0, pl.ds(base, cl), :].astype(F32),
                                   src[0, pl.ds(next_start, SSM_HALO), :].astype(F32) * has_next], axis=0)
            acc = bias_ref[0]
            off = SSM_HALO - SSM_CONV // 2
            for tap in range(SSM_CONV):
                acc = acc + ext[off + tap:off + tap + cl, :] * w_ref[0, tap:tap + 1, :]
            return _silu(acc)

        xs_s[pl.ds(base, cl), :] = conv(xs_ref, wx_ref, bx_ref)
        bmat = conv(bm_ref, wb_ref, bb_ref).astype(BF16)
        cmat = conv(cm_ref, wc_ref, bc_ref).astype(BF16)
        bt_s[c] = bmat.astype(F32).T.astype(BF16)
        c_s[pl.ds(base, cl), :] = cmat
        cb_s[c] = lax.dot_general(cmat, bmat, (((1,), (1,)), ((), ())),
                                  preferred_element_type=F32)

        dt = jax.nn.softplus(dt_ref[0, pl.ds(base, cl), :] + dt_bias)
        pieces = _split3(dt * a_neg)
        acum_f = sum(jnp.dot(tril, p, preferred_element_type=F32) for p in pieces)
        acum_b = sum(jnp.dot(triu, p, preferred_element_type=F32) for p in pieces)
        acum = jnp.where(fwd_lane, acum_f, acum_b)
        a_end = jnp.where(fwd_lane[0:1], acum[cl - 1:cl, :], acum[0:1, :])
        w_state = dt * jnp.exp(a_end - acum)
        packed_t = jnp.where(lane < ncol, acum * LOG2E - jnp.log2(dt), w_state).T
        acum_s[c] = acum * LOG2E
        pt_s[c] = packed_t[0:SSM_DT_COPIES * ncol, :]
        return carry

    ri = lax.broadcasted_iota(jnp.int32, (cl, cl), 0)
    ci = lax.broadcasted_iota(jnp.int32, (cl, cl), 1)
    lane = lax.broadcasted_iota(jnp.int32, (cl, LANES), 1)
    first = lane < HEAD_DIM
    fwd_lane = (lane % ncol) < hpg
    tril = jnp.where(ci <= ri, 1.0, 0.0).astype(BF16)
    triu = jnp.where(ci >= ri, 1.0, 0.0).astype(BF16)
    dt_bias = dtb_ref[0]
    a_neg = a_ref[0]

    lax.fori_loop(0, nchunk, conv_chunk, 0, unroll=4)

    for direction in range(2):
        if direction == 0:
            keep = ci <= ri
            end_row = cl - 1
        else:
            keep = ci >= ri
            end_row = 0

        st_s[...] = jnp.zeros_like(st_s)

        def local_part(c, direction=direction, keep=keep, end_row=end_row):
            rows = pl.ds(pl.multiple_of(c * cl, cl), cl)
            acum = acum_s[c]
            packed_t = pt_s[c]

            xs = xs_s[rows, :]
            bt = bt_s[c].astype(F32)
            cb = cb_s[c]
            yd_blocks, new_blocks, scale_blocks = [], [], []
            for pr in range(npair):
                xblk = xs[:, pr * LANES:(pr + 1) * LANES].astype(BF16)
                lhs, cols = [], []
                for j in (direction * hpg + 2 * pr, direction * hpg + 2 * pr + 1):
                    col = jnp.broadcast_to(acum[:, j:j + 1], (cl, cl))
                    row = packed_t[j:j + 1, :]
                    ws_row = packed_t[ncol + j:ncol + j + 1, :]
                    seg_dt = jnp.exp2(jnp.where(keep, col - row, -jnp.inf))
                    lhs.append((cb * seg_dt).astype(BF16))
                    lhs.append((bt * ws_row).astype(BF16))
                    cols.append(col)
                prod = jnp.dot(jnp.concatenate(lhs, axis=0), xblk, preferred_element_type=F32)
                yd_blocks.append(jnp.where(first, prod[0:cl], prod[2 * cl:3 * cl]))
                new_blocks.append(jnp.where(first, prod[cl:2 * cl], prod[3 * cl:4 * cl]))
                scale_blocks.append(jnp.exp2(jnp.where(first, cols[0], cols[1])))
            return rows, xs, yd_blocks, new_blocks, scale_blocks

        def trip(step, carry, direction=direction, end_row=end_row):
            chunks = [step * SSM_UNROLL + i for i in range(SSM_UNROLL)]
            if direction == 1:
                chunks = [nchunk - 1 - c for c in chunks]
            parts = [local_part(c) for c in chunks]
            state = [st_s[:, pr * LANES:(pr + 1) * LANES] for pr in range(npair)]
            for rows, xs, yd_blocks, new_blocks, scale_blocks in parts:
                cmat = c_s[rows, :]
                y_blocks = []
                for pr in range(npair):
                    y_off = jnp.dot(cmat, state[pr].astype(BF16), preferred_element_type=F32)
                    y_blocks.append(yd_blocks[pr] + y_off * scale_blocks[pr])
                    state[pr] = (state[pr] * scale_blocks[pr][end_row:end_row + 1, :]
                                 + new_blocks[pr])
                y = jnp.concatenate(y_blocks, axis=1)
                if direction == 0:
                    y_ref[0, rows, :] = y + dsk_ref[0] * xs
                else:
                    y_ref[0, rows, :] = y_ref[0, rows, :] + y
            for pr in range(npair):
                st_s[:, pr * LANES:(pr + 1) * LANES] = state[pr]
            return carry

        lax.fori_loop(0, nchunk // SSM_UNROLL, trip, 0)


def _ssd(xbc, dtp, conv_w, conv_b, dt_bias, a_neg, d_skip):
    b, s, _ = xbc.shape
    gw = SSM_GROUP_WIDTH
    b_off = SSM_INNER // LANES
    c_off = b_off + SSM_GROUPS
    nchunk = s // SSM_CHUNK

    def seq(width, off):
        return pl.BlockSpec((1, s, width), lambda bi, g: (bi, 0, off + g))

    def par(rows, width, off):
        return pl.BlockSpec((1, rows, width), lambda bi, g: (0, 0, off + g))

    return pl.pallas_call(
        _ssd_kernel,
        grid=(b, SSM_GROUPS),
        in_specs=[seq(gw, 0), seq(LANES, b_off), seq(LANES, c_off), seq(LANES, 0),
                  par(SSM_CONV, gw, 0), par(SSM_CONV, LANES, b_off), par(SSM_CONV, LANES, c_off),
                  par(1, gw, 0), par(1, LANES, b_off), par(1, LANES, c_off),
                  pl.BlockSpec((1, 1, LANES), lambda bi, g: (g, 0, 0)),
                  pl.BlockSpec((1, 1, LANES), lambda bi, g: (g, 0, 0)),
                  pl.BlockSpec((1, 1, gw), lambda bi, g: (g, 0, 0))],
        out_specs=pl.BlockSpec((1, s, gw), lambda bi, g: (bi, 0, g)),
        out_shape=jax.ShapeDtypeStruct((b, s, SSM_INNER), F32),
        scratch_shapes=[pltpu.VMEM((s, gw), F32),
                        pltpu.VMEM((nchunk, SSM_STATE, SSM_CHUNK), BF16),
                        pltpu.VMEM((s, LANES), BF16),
                        pltpu.VMEM((nchunk, SSM_CHUNK, SSM_CHUNK), F32),
                        pltpu.VMEM((nchunk, SSM_CHUNK, LANES), F32),
                        pltpu.VMEM((nchunk, SSM_DT_COPIES * 2 * SSM_HEADS_PER_GROUP, SSM_CHUNK), F32),
                        pltpu.VMEM((SSM_STATE, gw), F32)],
        compiler_params=_params("parallel", "parallel"),
        name="ssd",
    )(xbc, xbc, xbc, dtp, conv_w, conv_w, conv_w, conv_b, conv_b, conv_b, dt_bias, a_neg, d_skip)


def _tail_kernel(x_ref, ya_ref, yb_ref, yc_ref, z_ref, ua_ref, ub_ref, uc_ref, p_ref,
                 nw_ref, woa_ref, wob_ref, woc_ref, wout_ref, pg_ref, wpg_ref, wple_ref, gn_ref,
                 o_ref, *h_ref, final):
    def mm(a, w_ref):
        return jnp.dot(a, w_ref[...], preferred_element_type=F32)

    ya = mm(ya_ref[...], woa_ref)
    yb = mm(yb_ref[...], wob_ref)
    yc_in = _rms(yc_ref[...] * _silu(z_ref[...]), nw_ref[...]).astype(BF16)
    yc = mm(yc_in, woc_ref)
    merged = (_sigmoid(ua_ref[...]) * ya + _sigmoid(ub_ref[...]) * yb
              + _sigmoid(uc_ref[...]) * yc)
    x1 = x_ref[...] + mm(merged.astype(BF16), wout_ref)
    gate = jax.nn.sigmoid(mm(_rms(x1, pg_ref[...]).astype(BF16), wpg_ref))
    x2 = x1 + mm(p_ref[...].astype(BF16), wple_ref) * gate
    if final:
        o_ref[...] = _rms(x2, gn_ref[...])
    else:
        o_ref[...] = x2
        h_ref[0][...] = _rms(x2, gn_ref[...]).astype(BF16)


MISC_WIDTH = SSM_INNER + DIL_GROUP_WIDTH + 3 * D_MODEL
MISC_GB_LANE_BLOCK = SSM_INNER // LANES
MISC_U_BLOCK = (SSM_INNER + DIL_GROUP_WIDTH) // D_MODEL


def _tail(x2d, ya, yb, yc, misc, p2d, nw, woa, wob, woc, wout, pg, wpg, wple, gn, final, tm=512):
    m = x2d.shape[0]
    out_specs = pl.BlockSpec((tm, D_MODEL), lambda i: (i, 0))
    out_shape = jax.ShapeDtypeStruct((m, D_MODEL), F32)
    if not final:
        out_specs = [out_specs, pl.BlockSpec((tm, D_MODEL), lambda i: (i, 0))]
        out_shape = [out_shape, jax.ShapeDtypeStruct((m, D_MODEL), BF16)]

    def rows(width, off=0):
        return pl.BlockSpec((tm, width), lambda i: (i, off))

    def whole(arr):
        return pl.BlockSpec(arr.shape, lambda i: (0, 0), pipeline_mode=pl.Buffered(1))

    ub = MISC_U_BLOCK
    return pl.pallas_call(
        functools.partial(_tail_kernel, final=final),
        grid=(m // tm,),
        in_specs=[rows(D_MODEL), rows(NA_WIDTH), rows(DIL_GROUP_WIDTH), rows(SSM_INNER),
                  rows(SSM_INNER, 0), rows(D_MODEL, ub), rows(D_MODEL, ub + 1), rows(D_MODEL, ub + 2),
                  rows(PLE_DIM),
                  whole(nw), whole(woa), whole(wob), whole(woc), whole(wout), whole(pg), whole(wpg),
                  whole(wple), whole(gn)],
        out_specs=out_specs,
        out_shape=out_shape,
        compiler_params=_params("parallel"),
        name="tail",
    )(x2d, ya, yb, yc, misc, misc, misc, misc, p2d, nw, woa, wob, woc, wout, pg, wpg, wple, gn)


def _prep_weights(w_in, conv_w, conv_b, a_log, dt_bias, d_skip):
    depth = w_in.shape[0]
    offs = np.concatenate([[0], np.cumsum(IN_SPLITS)])
    w16 = w_in.astype(BF16)
    (qa, ka, va, ga, qb, kb, vb, gb, xbc, z, dtr, ua, ub, uc) = [
        w16[:, :, int(offs[i]):int(offs[i + 1])] for i in range(len(IN_SPLITS))]
    w_a = w16[:, :, :int(offs[4])]
    gw = DIL_GROUP_WIDTH
    w_b = [jnp.concatenate([_rotary_layout(qb[:, :, g * gw:(g + 1) * gw]),
                            _rotary_layout(kb[:, :, g * gw:(g + 1) * gw]),
                            vb[:, :, g * gw:(g + 1) * gw]], axis=2)
           for g in range(len(DIL_PAIRS))]
    w_misc = jnp.concatenate([z, gb, ua, ub, uc], axis=2)

    hpg = SSM_HEADS_PER_GROUP

    def per_group(t):
        lead = t.shape[:-2]
        t = t.reshape(lead + (2, SSM_GROUPS, hpg))
        t = jnp.moveaxis(t, -2, -3).reshape(lead + (SSM_GROUPS, 2 * hpg))
        t = jnp.tile(t, (1,) * (t.ndim - 1) + (SSM_DT_COPIES,))
        pad = [(0, 0)] * (t.ndim - 1) + [(0, LANES - 2 * hpg * SSM_DT_COPIES)]
        return jnp.pad(t, pad).reshape(lead + (SSM_GROUPS * LANES,))

    w_dt = per_group(dtr.reshape(depth, D_MODEL, 2, SSM_HEADS))
    w_xdt = jnp.concatenate([xbc, w_dt], axis=2)
    dtb = per_group(dt_bias.astype(F32)).reshape(depth, SSM_GROUPS, 1, LANES)
    a_neg = per_group(-jnp.exp(a_log.astype(F32))).reshape(depth, SSM_GROUPS, 1, LANES)
    dsk = jnp.repeat(d_skip.astype(F32), HEAD_DIM, axis=1).reshape(depth, SSM_GROUPS, 1, SSM_GROUP_WIDTH)
    cw = conv_w.astype(F32).reshape(depth, 1, SSM_CONV, SSM_CONV_DIM)
    cbias = conv_b.astype(F32).reshape(depth, 1, 1, SSM_CONV_DIM)
    return w_a, w_b, w_misc, w_xdt, dtb, a_neg, dsk, cw, cbias


def kernel(x, p, norm_w, w_in, na_rpb, conv_w, conv_b, a_log, dt_bias, d_skip, ssm_norm_w,
           w_oa, w_ob, w_oc, w_out, ple_norm_w, w_ple, w_ple_gate, final_norm_w):
    b, s, dm = x.shape
    depth = w_in.shape[0]
    m = b * s
    w_a, w_b, w_misc, w_xdt, dtb, a_neg, dsk, cw, cbias = _prep_weights(
        w_in, conv_w, conv_b, a_log, dt_bias, d_skip)
    tabs = [_rotary_tables(s, d) for _, d in DIL_PAIRS]
    na_col_scale = jnp.concatenate([jnp.full((1, NA_WIDTH), SCORE_SCALE, F32),
                                    jnp.ones((1, 3 * NA_WIDTH), F32)], axis=1)
    bias_tables = _na_bias_table(na_rpb.reshape((-1,) + na_rpb.shape[2:]))
    bias_tables = bias_tables.reshape((depth, NA_HEADS // 2) + bias_tables.shape[1:])
    row = lambda v: v.astype(F32).reshape(1, -1)

    x2d = x.reshape(m, dm)
    h = _norm(x2d, row(norm_w[0]))
    for i in range(depth):
        final = i == depth - 1
        qkvg = _proj(h, w_a[i], 2 * NA_WIDTH, BF16, col_scale=na_col_scale).reshape(b, s, 4 * NA_WIDTH)
        misc = _proj(h, w_misc[i], MISC_WIDTH // 2, BF16)
        xbc, dtp = _proj_split(h, w_xdt[i], SSM_CONV_DIM, BF16, F32)
        xbc = xbc.reshape(b, s, SSM_CONV_DIM)
        dtp = dtp.reshape(b, s, SSM_GROUPS * LANES)
        h3 = h.reshape(b, s, dm)
        qkvs = [_proj_dil(h3, w_b[gi][i], tabs[gi], d) for gi, (_, d) in enumerate(DIL_PAIRS)]

        ya = _na_attention(qkvg, bias_tables[i]).reshape(m, NA_WIDTH)
        yb = _dil_attention(qkvs, misc.reshape(b, s, MISC_WIDTH), MISC_GB_LANE_BLOCK)
        yb = yb.reshape(m, DIL_GROUP_WIDTH)
        yc = _ssd(xbc, dtp, cw[i], cbias[i], dtb[i], a_neg[i], dsk[i]).reshape(m, SSM_INNER)

        gn = row(final_norm_w) if final else row(norm_w[i + 1])
        out = _tail(x2d, ya, yb, yc, misc, p[i].reshape(m, PLE_DIM), row(ssm_norm_w[i]),
                    w_oa[i].astype(BF16), w_ob[i].astype(BF16), w_oc[i].astype(BF16),
                    w_out[i].astype(BF16), row(ple_norm_w[i]), w_ple_gate[i].astype(BF16),
                    w_ple[i].astype(BF16), gn, final=final)
        if final:
            x2d = out
        else:
            x2d, h = out
    return x2d.reshape(b, s, dm)
```
